```python
import jax, jax.numpy as jnp
from jax import lax
import numpy as np

D_MODEL = 2048
BATCH = 4
SEQ = 4096
DEPTH = 2

CHUNK = 64
Q_BLOCK = 128
SB_HEAD_DIM = 128
SB_WIDTH = D_MODEL // 4
SB_HEADS = SB_WIDTH // SB_HEAD_DIM
FOX_HEAD_DIM = 128
FOX_WIDTH = D_MODEL // 4
FOX_HEADS = FOX_WIDTH // FOX_HEAD_DIM
RWKV_HEAD_DIM = 64
RWKV_WIDTH = D_MODEL // 2
RWKV_HEADS = RWKV_WIDTH // RWKV_HEAD_DIM
DECAY_LORA = 64
AAA_LORA = 64
GATE_LORA = 160
RWKV_IN = 3 * RWKV_WIDTH + DECAY_LORA + AAA_LORA + GATE_LORA
N_BRANCHES = 3
N_IN = 3 * SB_WIDTH + 3 * FOX_WIDTH + FOX_HEADS + RWKV_IN + N_BRANCHES * D_MODEL
D_FF = 4 * D_MODEL
RMS_EPS = 1e-6
RWKV_GN_EPS = 64e-5

kernel_name = 'hybrid_sb_fox_rwkv7_block'


def _split(x, sizes):
    points = np.cumsum(sizes)[:-1].tolist()
    return jnp.split(x, points, axis=-1)


def _rms_norm(x, gain):
    xf = x.astype(jnp.float32)
    y = xf * lax.rsqrt(jnp.mean(xf * xf, axis=-1, keepdims=True) + RMS_EPS)
    return (y * gain.astype(jnp.float32)).astype(x.dtype)


def _heads(x, n_heads):
    b, s, _ = x.shape
    return x.reshape(b, s, n_heads, -1).transpose(0, 2, 1, 3)


def _merge_heads(x):
    b, h, s, d = x.shape
    return x.transpose(0, 2, 1, 3).reshape(b, s, h * d)


def _query_blocks(x):
    b, h, s = x.shape[:3]
    x = x.reshape((b, h, s // Q_BLOCK, Q_BLOCK) + x.shape[3:])
    return jnp.moveaxis(x, 2, 0)


def _unblock(o):
    nb, b, h, qb, d = o.shape
    return o.transpose(1, 2, 0, 3, 4).reshape(b, h, nb * qb, d)


def stick_breaking_attention(q, k, v):
    s, d = q.shape[2], q.shape[3]
    scale = d ** -0.5
    kpos = jnp.arange(s)

    def block(args):
        qb, start = args
        z = jnp.einsum('bhqd,bhkd->bhqk', qb, k).astype(jnp.float32) * scale
        qpos = start + jnp.arange(Q_BLOCK)
        strict = kpos[None, :] < qpos[:, None]
        log_beta = jax.nn.log_sigmoid(z)
        log_rest = jnp.where(strict, jax.nn.log_sigmoid(-z), 0.0)
        between = lax.cumsum(log_rest, axis=3, reverse=True) - log_rest
        a = jnp.where(strict, jnp.exp(log_beta + between), 0.0)
        return jnp.einsum('bhqk,bhkd->bhqd', a.astype(v.dtype), v)

    starts = jnp.arange(s // Q_BLOCK, dtype=jnp.int32) * Q_BLOCK
    return _unblock(lax.map(block, (_query_blocks(q), starts)))


def forgetting_attention(q, k, v, log_f):
    s, d = q.shape[2], q.shape[3]
    scale = d ** -0.5
    kpos = jnp.arange(s)
    c = lax.cumsum(log_f.astype(jnp.float32), axis=2)

    def block(args):
        qb, cb, start = args
        z = jnp.einsum('bhqd,bhkd->bhqk', qb, k).astype(jnp.float32) * scale
        z = z + (cb[..., :, None] - c[..., None, :])
        qpos = start + jnp.arange(Q_BLOCK)
        causal = kpos[None, :] <= qpos[:, None]
        p = jax.nn.softmax(jnp.where(causal, z, -jnp.inf), axis=-1)
        return jnp.einsum('bhqk,bhkd->bhqd', p.astype(v.dtype), v)

    starts = jnp.arange(s // Q_BLOCK, dtype=jnp.int32) * Q_BLOCK
    return _unblock(lax.map(block, (_query_blocks(q), _query_blocks(c), starts)))


def rwkv7_time_mix(z, mu, w0, w_up, a0, a_up, g_up, k_k, k_a, r_k, ln_w, ln_b):
    b, s, _ = z.shape
    h, n = RWKV_HEADS, RWKV_HEAD_DIM
    z_prev = jnp.pad(z[:, :-1], ((0, 0), (1, 0), (0, 0)))
    z = z + (z_prev - z) * mu
    r, k, v, wd, ad, gd = _split(z, (RWKV_WIDTH,) * 3 + (DECAY_LORA, AAA_LORA, GATE_LORA))
    w_log = -jax.nn.softplus(-(w0 + jnp.tanh(wd) @ w_up)) - 0.5
    a = jax.nn.sigmoid(a0 + ad @ a_up)
    g = jax.nn.sigmoid(gd) @ g_up
    kk_raw = k * k_k
    k = k * (1 + (a - 1) * k_a)

    def per_head(t):
        return t.reshape(b, s, h, n).astype(jnp.float32)

    r_h, k_h, v_h, a_h, kk = map(per_head, (r, k, v, a, kk_raw))
    kk = kk / jnp.maximum(jnp.sqrt(jnp.sum(kk * kk, axis=-1, keepdims=True)), 1e-12)
    decay = jnp.exp(-jnp.exp(per_head(w_log)))

    def time_major(t):
        return t.transpose(1, 0, 2, 3).reshape(s // CHUNK, CHUNK, b, h, n)

    inputs = tuple(time_major(t) for t in (r_h, decay, k_h, v_h, kk, kk * a_h))

    def frame(state, inp):
        r_t, w_t, k_t, v_t, kk_t, kka_t = inp
        sa = jnp.einsum('bhvk,bhk->bhv', state, kk_t)
        state = (state * w_t[:, :, None, :] - sa[..., None] * kka_t[:, :, None, :]
                 + v_t[..., None] * k_t[:, :, None, :])
        return state, jnp.einsum('bhvk,bhk->bhv', state, r_t)

    def chunk(state, inp):
        return lax.scan(frame, state, inp)

    _, y = lax.scan(chunk, jnp.zeros((b, h, n, n), jnp.float32), inputs)
    y = y.reshape(s, b, h, n).transpose(1, 0, 2, 3)
    mean = jnp.mean(y, axis=-1, keepdims=True)
    var = jnp.mean(jnp.square(y - mean), axis=-1, keepdims=True)
    y = ((y - mean) * lax.rsqrt(var + RWKV_GN_EPS) * ln_w.astype(jnp.float32).reshape(h, n)
         + ln_b.astype(jnp.float32).reshape(h, n))
    y = y + jnp.sum(r_h * k_h * r_k.astype(jnp.float32), axis=-1, keepdims=True) * v_h
    return y.reshape(b, s, RWKV_WIDTH).astype(z.dtype) * g


def hybrid_mixer(u, w_in, b_forget, rwkv_mu, rwkv_w0, rwkv_w_up, rwkv_a0, rwkv_a_up, rwkv_g_up,
                 rwkv_k_k, rwkv_k_a, rwkv_r_k, rwkv_ln_w, rwkv_ln_b,
                 w_branch_a, w_branch_b, w_branch_c, w_out):
    proj = u @ w_in
    sb, fox, rw, gates = _split(proj, (3 * SB_WIDTH, 3 * FOX_WIDTH + FOX_HEADS, RWKV_IN,
                                       N_BRANCHES * D_MODEL))
    q_a, k_a, v_a = _split(sb, (SB_WIDTH,) * 3)
    y_a = _merge_heads(stick_breaking_attention(_heads(q_a, SB_HEADS), _heads(k_a, SB_HEADS),
                                                _heads(v_a, SB_HEADS)))
    q_b, k_b, v_b, f_b = _split(fox, (FOX_WIDTH,) * 3 + (FOX_HEADS,))
    log_f = jax.nn.log_sigmoid((f_b + b_forget).astype(jnp.float32)).transpose(0, 2, 1)
    y_b = _merge_heads(forgetting_attention(_heads(q_b, FOX_HEADS), _heads(k_b, FOX_HEADS),
                                            _heads(v_b, FOX_HEADS), log_f))
    y_c = rwkv7_time_mix(rw, rwkv_mu, rwkv_w0, rwkv_w_up, rwkv_a0, rwkv_a_up, rwkv_g_up,
                         rwkv_k_k, rwkv_k_a, rwkv_r_k, rwkv_ln_w, rwkv_ln_b)
    g_a, g_b, g_c = _split(jax.nn.sigmoid(gates), (D_MODEL,) * 3)
    m = g_a * (y_a @ w_branch_a) + g_b * (y_b @ w_branch_b) + g_c * (y_c @ w_branch_c)
    return m @ w_out


def setup_inputs(seed: int = 0) -> dict:
    key = jax.random.key(seed)
    ks = jax.random.split(key, 24)
    L, D = DEPTH, D_MODEL

    def nrm(k, shape, scale):
        return jax.random.normal(k, shape, jnp.float32) * scale

    return {
        'x': nrm(ks[0], (BATCH, SEQ, D), 1.0),
        'norm_mix_pre': 1.0 + nrm(ks[1], (L, D), 0.05),
        'norm_mix_post': 1.0 + nrm(ks[2], (L, D), 0.05),
        'norm_mlp_pre': 1.0 + nrm(ks[3], (L, D), 0.05),
        'norm_mlp_post': 1.0 + nrm(ks[4], (L, D), 0.05),
        'w_in': nrm(ks[5], (L, D, N_IN), D ** -0.5),
        'b_forget': 1.0 + nrm(ks[6], (L, FOX_HEADS), 0.5),
        'rwkv_mu': jax.random.uniform(ks[7], (L, RWKV_IN), jnp.float32),
        'rwkv_w0': nrm(ks[8], (L, RWKV_WIDTH), 0.5),
        'rwkv_w_up': nrm(ks[9], (L, DECAY_LORA, RWKV_WIDTH), 0.5 * DECAY_LORA ** -0.5),
        'rwkv_a0': nrm(ks[10], (L, RWKV_WIDTH), 0.1),
        'rwkv_a_up': nrm(ks[11], (L, AAA_LORA, RWKV_WIDTH), 0.5 * AAA_LORA ** -0.5),
        'rwkv_g_up': nrm(ks[12], (L, GATE_LORA, RWKV_WIDTH), GATE_LORA ** -0.5),
        'rwkv_k_k': 0.85 + nrm(ks[13], (L, RWKV_WIDTH), 0.05),
        'rwkv_k_a': 1.0 + nrm(ks[14], (L, RWKV_WIDTH), 0.05),
        'rwkv_r_k': nrm(ks[15], (L, RWKV_HEADS, RWKV_HEAD_DIM), 0.1),
        'rwkv_ln_w': 1.0 + nrm(ks[16], (L, RWKV_WIDTH), 0.05),
        'rwkv_ln_b': nrm(ks[17], (L, RWKV_WIDTH), 0.01),
        'w_branch_a': nrm(ks[18], (L, SB_WIDTH, D), SB_WIDTH ** -0.5),
        'w_branch_b': nrm(ks[19], (L, FOX_WIDTH, D), FOX_WIDTH ** -0.5),
        'w_branch_c': nrm(ks[20], (L, RWKV_WIDTH, D), RWKV_WIDTH ** -0.5),
        'w_out': nrm(ks[21], (L, D, D), D ** -0.5),
        'w_mlp_up': nrm(ks[22], (L, D, D_FF), D ** -0.5),
        'w_mlp_down': nrm(ks[23], (L, D_FF, D), D_FF ** -0.5),
    }


def reference(x, norm_mix_pre, norm_mix_post, norm_mlp_pre, norm_mlp_post, w_in, b_forget,
              rwkv_mu, rwkv_w0, rwkv_w_up, rwkv_a0, rwkv_a_up, rwkv_g_up, rwkv_k_k, rwkv_k_a,
              rwkv_r_k, rwkv_ln_w, rwkv_ln_b, w_branch_a, w_branch_b, w_branch_c, w_out,
              w_mlp_up, w_mlp_down):
    for l in range(DEPTH):
        u = _rms_norm(x, norm_mix_pre[l])
        mix = hybrid_mixer(u, w_in[l], b_forget[l], rwkv_mu[l], rwkv_w0[l], rwkv_w_up[l],
                           rwkv_a0[l], rwkv_a_up[l], rwkv_g_up[l], rwkv_k_k[l], rwkv_k_a[l],
                           rwkv_r_k[l], rwkv_ln_w[l], rwkv_ln_b[l], w_branch_a[l], w_branch_b[l],
                           w_branch_c[l], w_out[l])
        x = x + _rms_norm(mix, norm_mix_post[l])
        hdn = jnp.square(jax.nn.relu(_rms_norm(x, norm_mlp_pre[l]) @ w_mlp_up[l]))
        x = x + _rms_norm(hdn @ w_mlp_down[l], norm_mlp_post[l])
    return x
```

```python
import functools

import jax
import jax.numpy as jnp
from jax import lax
from jax.experimental import pallas as pl
from jax.experimental.pallas import tpu as pltpu

F32 = jnp.float32
BF16 = jnp.bfloat16

RMS_EPS = 1e-6
RWKV_GN_EPS = 64e-5
ATTN_HEAD_DIM = 128
RWKV_HEAD_DIM = 64
RWKV_CHUNK = 64
RWKV_GROUP = 256
DECAY_LORA = 64
AAA_LORA = 64
GATE_LORA = 160
LANE = 128
MASK_VALUE = -1e30
VMEM_LIMIT = 56 * 1024 * 1024

SMALL_WD = 0
SMALL_AD = 128
SMALL_GD = 256
SMALL_F = 512
SMALL_WIDTH = 640


def _dot(a, b):
    return jnp.dot(a, b, preferred_element_type=F32)


def _dot_nt(a, b):
    return lax.dot_general(a, b, (((1,), (1,)), ((), ())), preferred_element_type=F32)


def _dot_tn(a, b):
    return lax.dot_general(a, b, (((0,), (0,)), ((), ())), preferred_element_type=F32)


def _split_dot_rhs(m_bf16, x, parts):
    acc = None
    rem = x
    for p in range(parts):
        h = rem.astype(BF16)
        t = _dot(m_bf16, h)
        acc = t if acc is None else acc + t
        if p + 1 < parts:
            rem = rem - h.astype(F32)
    return acc


def _split_dot_lhs(x, m_bf16, parts):
    acc = None
    rem = x
    for p in range(parts):
        h = rem.astype(BF16)
        t = _dot(h, m_bf16)
        acc = t if acc is None else acc + t
        if p + 1 < parts:
            rem = rem - h.astype(F32)
    return acc


def _log_sigmoid(z):
    return jnp.minimum(z, 0.0) - jnp.log(1.0 + jnp.exp(-jnp.abs(z)))


def _sigmoid(z):
    return 1.0 / (1.0 + jnp.exp(-z))


def _params(*sem):
    return pltpu.CompilerParams(dimension_semantics=sem, vmem_limit_bytes=VMEM_LIMIT)


def _norm_proj_kernel(x_ref, g_ref, w_ref, o_ref, u_ref, *, sigmoid_from):
    n = pl.program_id(1)

    @pl.when(n == 0)
    def _():
        x = x_ref[...]
        ms = jnp.mean(x * x, axis=-1, keepdims=True)
        u_ref[...] = (x * lax.rsqrt(ms + RMS_EPS) * g_ref[...]).astype(BF16)

    acc = _dot(u_ref[...], w_ref[...])
    if sigmoid_from is None:
        o_ref[...] = acc.astype(o_ref.dtype)
    else:
        @pl.when(n < sigmoid_from)
        def _():
            o_ref[...] = acc.astype(o_ref.dtype)

        @pl.when(n >= sigmoid_from)
        def _():
            o_ref[...] = _sigmoid(acc).astype(o_ref.dtype)


def _norm_proj(x2, gain, w, out_dtype, *, tm, tn, sigmoid_from_col=None):
    t, d = x2.shape
    n = w.shape[1]
    assert t % tm == 0 and n % tn == 0
    sig = None if sigmoid_from_col is None else sigmoid_from_col // tn
    if sigmoid_from_col is not None:
        assert sigmoid_from_col % tn == 0
    return pl.pallas_call(
        functools.partial(_norm_proj_kernel, sigmoid_from=sig),
        grid=(t // tm, n // tn),
        in_specs=[
            pl.BlockSpec((tm, d), lambda m, j: (m, 0)),
            pl.BlockSpec((1, d), lambda m, j: (0, 0)),
            pl.BlockSpec((d, tn), lambda m, j: (0, j)),
        ],
        out_specs=pl.BlockSpec((tm, tn), lambda m, j: (m, j)),
        out_shape=jax.ShapeDtypeStruct((t, n), out_dtype),
        scratch_shapes=[pltpu.VMEM((tm, d), BF16)],
        compiler_params=_params("parallel", "arbitrary"),
        name="norm_proj",
    )(x2, gain.reshape(1, d), w)


def _forget_cumsum_kernel(f_ref, b_ref, c_ref, *, cb):
    s = f_ref.shape[1]
    row = lax.broadcasted_iota(jnp.int32, (cb, cb), 0)
    col = lax.broadcasted_iota(jnp.int32, (cb, cb), 1)
    tri = jnp.where(col <= row, 1.0, 0.0).astype(BF16)
    carry = jnp.zeros((1, LANE), F32)
    for j in range(s // cb):
        lf = _log_sigmoid(f_ref[0, j * cb:(j + 1) * cb, :] + b_ref[...])
        c = _split_dot_rhs(tri, lf, 3) + carry
        c_ref[0, j * cb:(j + 1) * cb, :] = c
        carry = c[cb - 1:cb, :]


def _forget_cumsum(small3, b_forget_pad):
    b, s, _ = small3.shape
    cb = min(256, s)
    return pl.pallas_call(
        functools.partial(_forget_cumsum_kernel, cb=cb),
        grid=(b,),
        in_specs=[
            pl.BlockSpec((1, s, LANE), lambda i: (i, 0, SMALL_F // LANE)),
            pl.BlockSpec((1, LANE), lambda i: (0, 0)),
        ],
        out_specs=pl.BlockSpec((1, s, LANE), lambda i: (i, 0, 0)),
        out_shape=jax.ShapeDtypeStruct((b, s, LANE), F32),
        compiler_params=_params("parallel"),
        name="forget_cumsum",
    )(small3, b_forget_pad)


def _sb_kernel(q_ref, k_ref, v_ref, o_ref, *, tq, scale):
    i = pl.program_id(2)
    q = q_ref[0]
    row = lax.broadcasted_iota(jnp.int32, (tq, tq), 0)
    col = lax.broadcasted_iota(jnp.int32, (tq, tq), 1)
    upper = jnp.where(row > col, 1.0, 0.0).astype(BF16)
    strict = col < row

    def block(j, carry, acc, diagonal):
        start = pl.multiple_of(j * tq, tq)
        kb = k_ref[0, pl.ds(start, tq), :]
        vb = v_ref[0, pl.ds(start, tq), :]
        z = _dot_nt(q, kb) * scale
        log_beta = _log_sigmoid(z)
        log_rest = log_beta - z
        if diagonal:
            log_rest = jnp.where(strict, log_rest, 0.0)
        between = _split_dot_lhs(log_rest, upper, 2) + carry
        a = jnp.exp(log_beta + between)
        if diagonal:
            a = jnp.where(strict, a, 0.0)
        acc = acc + _dot(a.astype(BF16), vb)
        carry = carry + jnp.sum(log_rest, axis=1, keepdims=True)
        return carry, acc

    carry0 = jnp.zeros((tq, 1), F32)
    acc0 = jnp.zeros((tq, ATTN_HEAD_DIM), F32)
    carry, acc = block(i, carry0, acc0, True)

    def body(jj, state):
        return block(i - 1 - jj, state[0], state[1], False)

    carry, acc = lax.fori_loop(0, i, body, (carry, acc))
    o_ref[0] = acc.astype(o_ref.dtype)


def _sb_attention(proj3, *, q_col, k_col, v_col, heads, tq):
    b, s, _ = proj3.shape
    dh = ATTN_HEAD_DIM
    return pl.pallas_call(
        functools.partial(_sb_kernel, tq=tq, scale=dh ** -0.5),
        grid=(b, heads, s // tq),
        in_specs=[
            pl.BlockSpec((1, tq, dh), lambda bi, h, i: (bi, i, q_col + h)),
            pl.BlockSpec((1, s, dh), lambda bi, h, i: (bi, 0, k_col + h)),
            pl.BlockSpec((1, s, dh), lambda bi, h, i: (bi, 0, v_col + h)),
        ],
        out_specs=pl.BlockSpec((1, tq, dh), lambda bi, h, i: (bi, i, h)),
        out_shape=jax.ShapeDtypeStruct((b, s, heads * dh), BF16),
        compiler_params=_params("parallel", "parallel", "arbitrary"),
        name="sb_attention",
    )(proj3, proj3, proj3)


def _fox_kernel(q_ref, k_ref, v_ref, cq_ref, ck_ref, o_ref, *, tq, scale):
    i = pl.program_id(2)
    q = q_ref[0]
    cq = cq_ref[0, 0]
    row = lax.broadcasted_iota(jnp.int32, (tq, tq), 0)
    col = lax.broadcasted_iota(jnp.int32, (tq, tq), 1)

    def scores(j):
        start = pl.multiple_of(j * tq, tq)
        kb = k_ref[0, pl.ds(start, tq), :]
        ck = ck_ref[0, 0, pl.ds(j, 1), :]
        return _dot_nt(q, kb) * scale + (cq - ck), v_ref[0, pl.ds(start, tq), :]

    z, vb = scores(i)
    z = jnp.where(col <= row, z, MASK_VALUE)
    m = jnp.max(z, axis=1, keepdims=True)
    p = jnp.exp(z - m)
    l = jnp.sum(p, axis=1, keepdims=True)
    acc = _dot(p.astype(BF16), vb)

    def body(jj, state):
        m, l, acc = state
        z, vb = scores(i - 1 - jj)
        m_new = jnp.maximum(m, jnp.max(z, axis=1, keepdims=True))
        alpha = jnp.exp(m - m_new)
        p = jnp.exp(z - m_new)
        l = alpha * l + jnp.sum(p, axis=1, keepdims=True)
        acc = alpha * acc + _dot(p.astype(BF16), vb)
        return m_new, l, acc

    m, l, acc = lax.fori_loop(0, i, body, (m, l, acc))
    o_ref[0] = (acc / l).astype(o_ref.dtype)


def _fox_attention(proj3, cq, ck, *, q_col, k_col, v_col, heads, tq):
    b, s, _ = proj3.shape
    dh = ATTN_HEAD_DIM
    nb = s // tq
    return pl.pallas_call(
        functools.partial(_fox_kernel, tq=tq, scale=dh ** -0.5),
        grid=(b, heads, nb),
        in_specs=[
            pl.BlockSpec((1, tq, dh), lambda bi, h, i: (bi, i, q_col + h)),
            pl.BlockSpec((1, s, dh), lambda bi, h, i: (bi, 0, k_col + h)),
            pl.BlockSpec((1, s, dh), lambda bi, h, i: (bi, 0, v_col + h)),
            pl.BlockSpec((1, 1, tq, 1), lambda bi, h, i: (bi, h, i, 0)),
            pl.BlockSpec((1, 1, nb, tq), lambda bi, h, i: (bi, h, 0, 0)),
        ],
        out_specs=pl.BlockSpec((1, tq, dh), lambda bi, h, i: (bi, i, h)),
        out_shape=jax.ShapeDtypeStruct((b, s, heads * dh), BF16),
        compiler_params=_params("parallel", "parallel", "arbitrary"),
        name="fox_attention",
    )(proj3, proj3, proj3, cq, ck)


def _same_head_mask(n):
    row = lax.broadcasted_iota(jnp.int32, (n, n), 0)
    col = lax.broadcasted_iota(jnp.int32, (n, n), 1)
    shift = RWKV_HEAD_DIM.bit_length() - 1
    return jnp.right_shift(row, shift) == jnp.right_shift(col, shift)


def _rwkv_prep_kernel(z_ref, zs_ref, mu_ref, mus_ref, w0_ref, wup_ref, a0_ref, aup_ref, gup_ref,
                      kk_ref, ka_ref,
                      r_out, lw_out, k_out, v_out, kn_out, b_out, g_out,
                      zc_ref, zsc_ref, *, tm, width):
    si = pl.program_id(1)

    @pl.when(si == 0)
    def _():
        zc_ref[...] = jnp.zeros_like(zc_ref)
        zsc_ref[...] = jnp.zeros_like(zsc_ref)

    z = z_ref[0].astype(F32)
    zs = zs_ref[0]
    first = lax.broadcasted_iota(jnp.int32, (tm, 1), 0) == 0

    def shifted(x, carry_ref):
        prev = jnp.where(first, carry_ref[...], pltpu.roll(x, 1, 0))
        carry_ref[...] = x[tm - 1:tm, :]
        return prev

    z = z + (shifted(z, zc_ref) - z) * mu_ref[...]
    zs = zs + (shifted(zs, zsc_ref) - zs) * mus_ref[...]

    r = z[:, :width]
    k = z[:, width:2 * width]
    v = z[:, 2 * width:]
    wd = zs[:, SMALL_WD:SMALL_WD + LANE]
    ad = zs[:, SMALL_AD:SMALL_AD + LANE]
    gd = zs[:, SMALL_GD:SMALL_GD + 2 * LANE]

    wl = w0_ref[...] + _dot(jnp.tanh(wd).astype(BF16), wup_ref[...])
    softplus = jnp.maximum(-wl, 0.0) + jnp.log(1.0 + jnp.exp(-jnp.abs(wl)))
    lw_out[0] = -jnp.exp(-softplus - 0.5)
    a = _sigmoid(a0_ref[...] + _dot(ad.astype(BF16), aup_ref[...]))
    g_out[0] = _dot(_sigmoid(gd).astype(BF16), gup_ref[...]).astype(g_out.dtype)

    r_out[0] = r.astype(r_out.dtype)
    v_out[0] = v.astype(v_out.dtype)
    k_out[0] = (k * (1.0 + (a - 1.0) * ka_ref[...])).astype(k_out.dtype)

    kk_raw = k * kk_ref[...]
    ones_bd = jnp.where(_same_head_mask(RWKV_GROUP), 1.0, 0.0).astype(BF16)
    for gi in range(width // RWKV_GROUP):
        sl = slice(gi * RWKV_GROUP, (gi + 1) * RWKV_GROUP)
        x = kk_raw[:, sl]
        ss = _split_dot_lhs(x * x, ones_bd, 2)
        kn = x / jnp.maximum(jnp.sqrt(ss), 1e-12)
        kn_out[0, :, sl] = kn.astype(kn_out.dtype)
        b_out[0, :, sl] = (kn * a[:, sl]).astype(b_out.dtype)


def _rwkv_prep(proj3, small3, mu_rkv, mu_small, w0, w_up_p, a0, a_up_p, g_up_p, k_k, k_a, *, rkv_col, width, tm):
    b, s, _ = proj3.shape
    full = lambda shape: pl.BlockSpec(shape, lambda bi, si: (0,) * len(shape))
    tile = pl.BlockSpec((1, tm, width), lambda bi, si: (bi, si, 0))
    shp = lambda dt: jax.ShapeDtypeStruct((b, s, width), dt)
    return pl.pallas_call(
        functools.partial(_rwkv_prep_kernel, tm=tm, width=width),
        grid=(b, s // tm),
        in_specs=[
            pl.BlockSpec((1, tm, 3 * width), lambda bi, si: (bi, si, rkv_col)),
            pl.BlockSpec((1, tm, SMALL_WIDTH), lambda bi, si: (bi, si, 0)),
            full((1, 3 * width)), full((1, SMALL_WIDTH)),
            full((1, width)), full((LANE, width)),
            full((1, width)), full((LANE, width)),
            full((2 * LANE, width)),
            full((1, width)), full((1, width)),
        ],
        out_specs=[tile] * 7,
        out_shape=[shp(BF16), shp(F32), shp(BF16), shp(BF16), shp(BF16), shp(BF16), shp(BF16)],
        scratch_shapes=[pltpu.VMEM((1, 3 * width), F32), pltpu.VMEM((1, SMALL_WIDTH), F32)],
        compiler_params=_params("parallel", "arbitrary"),
        name="rwkv_prep",
    )(proj3, small3, mu_rkv, mu_small, w0, w_up_p, a0, a_up_p, g_up_p, k_k, k_a)


def _rwkv_scan_kernel(r_ref, lw_ref, k_ref, v_ref, kn_ref, b_ref, g_ref, rk_ref, lnw_ref, lnb_ref,
                      o_ref, s_ref, *, tt):
    st = pl.program_id(2)

    @pl.when(st == 0)
    def _():
        s_ref[...] = jnp.zeros_like(s_ref)

    c = RWKV_CHUNK
    w = RWKV_GROUP
    heads = w // RWKV_HEAD_DIM
    assert heads * c == w
    same = _same_head_mask(w)
    row = lax.broadcasted_iota(jnp.int32, (w, w), 0)
    col = lax.broadcasted_iota(jnp.int32, (w, w), 1)
    strict = same & (col < row)
    incl = same & (col <= row)
    crow = lax.broadcasted_iota(jnp.int32, (c, c), 0)
    ccol = lax.broadcasted_iota(jnp.int32, (c, c), 1)
    tri = jnp.where(ccol <= crow, 1.0, 0.0).astype(BF16)

    def tile(x):
        return jnp.concatenate([x] * heads, axis=0)

    def stack(x):
        return jnp.where(same, tile(x), 0.0).astype(BF16)

    def unstack(x):
        out = x[0:c]
        for h in range(1, heads):
            out = out + x[h * c:(h + 1) * c]
        return out

    state = s_ref[...]
    ys = []
    for ci in range(tt // c):
        sl = slice(ci * c, (ci + 1) * c)
        lw = lw_ref[0, sl, :]
        r = r_ref[0, sl, :].astype(F32)
        k = k_ref[0, sl, :].astype(F32)
        v = v_ref[0, sl, :].astype(F32)
        kn = kn_ref[0, sl, :].astype(F32)
        bb = b_ref[0, sl, :].astype(F32)

        cum = _split_dot_rhs(tri, lw, 3)
        last = cum[c - 1:c, :]
        e_neg = jnp.exp(-cum)
        e_end = jnp.exp(last - cum)
        r_s = stack(r * jnp.exp(cum))
        a_s = stack(kn * jnp.exp(cum - lw))
        v_s = stack(v)
        b_t = tile((bb * e_neg).astype(BF16))
        k_t = tile((k * e_neg).astype(BF16))
        b_end = stack(bb * e_end)
        k_end = stack(k * e_end)

        a_ab = jnp.where(strict, _dot_nt(a_s, b_t), 0.0)
        a_ak = jnp.where(strict, _dot_nt(a_s, k_t), 0.0)
        a_rb = jnp.where(incl, _dot_nt(r_s, b_t), 0.0)
        a_rk = jnp.where(incl, _dot_nt(r_s, k_t), 0.0)

        pw = -a_ab
        nrm = pw
        steps = c.bit_length() - 1
        for _ in range(steps - 1):
            pwb = pw.astype(BF16)
            pw = _dot(pwb, pwb)
            nrm = nrm + pw + _dot(nrm.astype(BF16), pw.astype(BF16))
        nrm_b = nrm.astype(BF16)

        a_hat = a_s.astype(F32) + _dot(nrm_b, a_s)
        akv = _dot(a_ak.astype(BF16), v_s)
        v_hat = akv + _dot(nrm_b, akv.astype(BF16))
        q_s = _dot(a_rk.astype(BF16), v_s)
        kv = _dot_tn(v_s, k_end)

        state_b = state.astype(BF16)
        u_s = -(_dot_nt(a_hat.astype(BF16), state_b) + v_hat)
        u_b = u_s.astype(BF16)
        y_s = _dot_nt(r_s, state_b) + _dot(a_rb.astype(BF16), u_b) + q_s
        state = jnp.exp(last) * state + _dot_tn(u_b, b_end) + kv
        ys.append(unstack(y_s))

    s_ref[...] = state
    y = jnp.concatenate(ys, axis=0)

    mean_bd = jnp.where(same, 1.0 / RWKV_HEAD_DIM, 0.0).astype(BF16)
    ones_bd = jnp.where(same, 1.0, 0.0).astype(BF16)
    mean = _split_dot_lhs(y, mean_bd, 2)
    d = y - mean
    var = _split_dot_lhs(d * d, mean_bd, 2)
    y = d * lax.rsqrt(var + RWKV_GN_EPS) * lnw_ref[...] + lnb_ref[...]
    r = r_ref[0].astype(F32)
    k = k_ref[0].astype(F32)
    v = v_ref[0].astype(F32)
    bonus = _split_dot_lhs(r * k * rk_ref[...], ones_bd, 2)
    y = y + bonus * v
    o_ref[0] = (y * g_ref[0].astype(F32)).astype(o_ref.dtype)


def _rwkv_scan(r, lw, k, v, kn, bb, g, r_k, ln_w, ln_b, *, tt):
    b, s, width = r.shape
    w = RWKV_GROUP
    tile = pl.BlockSpec((1, tt, w), lambda bi, gi, si: (bi, si, gi))
    vec = pl.BlockSpec((1, w), lambda bi, gi, si: (0, gi))
    return pl.pallas_call(
        functools.partial(_rwkv_scan_kernel, tt=tt),
        grid=(b, width // w, s // tt),
        in_specs=[tile] * 7 + [vec] * 3,
        out_specs=tile,
        out_shape=jax.ShapeDtypeStruct((b, s, width), BF16),
        scratch_shapes=[pltpu.VMEM((w, w), F32)],
        compiler_params=_params("parallel", "parallel", "arbitrary"),
        name="rwkv_scan",
    )(r, lw, k, v, kn, bb, g, r_k, ln_w, ln_b)


def _mix_out_kernel(ya_ref, yb_ref, yc_ref, gate_ref, x_ref, pa_ref, pb_ref, pc_ref, wo_ref, gain_ref, o_ref):
    d = x_ref.shape[1]
    m = gate_ref[:, 0:d].astype(F32) * _dot(ya_ref[...], pa_ref[...])
    m = m + gate_ref[:, d:2 * d].astype(F32) * _dot(yb_ref[...], pb_ref[...])
    m = m + gate_ref[:, 2 * d:3 * d].astype(F32) * _dot(yc_ref[...], pc_ref[...])
    o = _dot(m.astype(BF16), wo_ref[...])
    ms = jnp.mean(o * o, axis=-1, keepdims=True)
    o_ref[...] = x_ref[...] + o * lax.rsqrt(ms + RMS_EPS) * gain_ref[...]


def _mix_out(ya, yb, yc, proj2, x2, pa, pb, pc, wo, gain, *, gate_col_block, tm):
    t, d = x2.shape
    const = lambda shape: pl.BlockSpec(shape, lambda m: (0, 0), pipeline_mode=pl.Buffered(1))
    return pl.pallas_call(
        _mix_out_kernel,
        grid=(t // tm,),
        in_specs=[
            pl.BlockSpec((tm, ya.shape[1]), lambda m: (m, 0)),
            pl.BlockSpec((tm, yb.shape[1]), lambda m: (m, 0)),
            pl.BlockSpec((tm, yc.shape[1]), lambda m: (m, 0)),
            pl.BlockSpec((tm, 3 * d), lambda m: (m, gate_col_block)),
            pl.BlockSpec((tm, d), lambda m: (m, 0)),
            const(pa.shape), const(pb.shape), const(pc.shape), const(wo.shape),
            const((1, d)),
        ],
        out_specs=pl.BlockSpec((tm, d), lambda m: (m, 0)),
        out_shape=jax.ShapeDtypeStruct((t, d), F32),
        compiler_params=_params("parallel"),
        name="mix_out",
    )(ya, yb, yc, proj2, x2, pa, pb, pc, wo, gain.reshape(1, d))


def _mlp_kernel(x_ref, g1_ref, wu_ref, wd_ref, g2_ref, o_ref, u_ref, acc_ref):
    f = pl.program_id(1)

    @pl.when(f == 0)
    def _():
        x = x_ref[...]
        ms = jnp.mean(x * x, axis=-1, keepdims=True)
        u_ref[...] = (x * lax.rsqrt(ms + RMS_EPS) * g1_ref[...]).astype(BF16)
        acc_ref[...] = jnp.zeros_like(acc_ref)

    h = jnp.maximum(_dot(u_ref[...], wu_ref[...]), 0.0)
    acc_ref[...] += _dot((h * h).astype(BF16), wd_ref[...])

    @pl.when(f == pl.num_programs(1) - 1)
    def _():
        o = acc_ref[...]
        ms = jnp.mean(o * o, axis=-1, keepdims=True)
        o_ref[...] = x_ref[...] + o * lax.rsqrt(ms + RMS_EPS) * g2_ref[...]


def _mlp(x2, g1, wu, wd, g2, *, tm, tf):
    t, d = x2.shape
    ff = wu.shape[1]
    return pl.pallas_call(
        _mlp_kernel,
        grid=(t // tm, ff // tf),
        in_specs=[
            pl.BlockSpec((tm, d), lambda m, f: (m, 0)),
            pl.BlockSpec((1, d), lambda m, f: (0, 0)),
            pl.BlockSpec((d, tf), lambda m, f: (0, f)),
            pl.BlockSpec((tf, d), lambda m, f: (f, 0)),
            pl.BlockSpec((1, d), lambda m, f: (0, 0)),
        ],
        out_specs=pl.BlockSpec((tm, d), lambda m, f: (m, 0)),
        out_shape=jax.ShapeDtypeStruct((t, d), F32),
        scratch_shapes=[pltpu.VMEM((tm, d), BF16), pltpu.VMEM((tm, d), F32)],
        compiler_params=_params("parallel", "arbitrary"),
        name="mlp",
    )(x2, g1.reshape(1, d), wu, wd, g2.reshape(1, d))


def _pad_rows(w, rows):
    return jnp.pad(w, ((0, rows - w.shape[0]), (0, 0)))


def _pad_cols(w, cols):
    return jnp.pad(w, ((0, 0), (0, cols - w.shape[1])))


def _mixer_layer(x, norm_pre, norm_post, w_in, b_forget, mu, w0, w_up, a0, a_up, g_up, k_k, k_a, r_k,
                 ln_w, ln_b, pa, pb, pc, w_out):
    bsz, s, d = x.shape
    t = bsz * s
    sbw = pa.shape[0]
    fxw = pb.shape[0]
    rww = pc.shape[0]
    sb_heads = sbw // ATTN_HEAD_DIM
    fx_heads = fxw // ATTN_HEAD_DIM

    o_fox = 3 * sbw
    o_f = o_fox + 3 * fxw
    o_rw = o_f + fx_heads
    o_wd = o_rw + 3 * rww
    o_ad = o_wd + DECAY_LORA
    o_gd = o_ad + AAA_LORA
    o_gate = o_gd + GATE_LORA
    assert w_in.shape[1] == o_gate + 3 * d

    w_main = jnp.concatenate([w_in[:, :o_f], w_in[:, o_rw:o_wd], w_in[:, o_gate:]], axis=1).astype(BF16)
    gate_col = 3 * sbw + 3 * fxw + 3 * rww
    w_small = jnp.concatenate([
        _pad_cols(w_in[:, o_wd:o_ad], LANE), _pad_cols(w_in[:, o_ad:o_gd], LANE),
        _pad_cols(w_in[:, o_gd:o_gate], 2 * LANE), _pad_cols(w_in[:, o_f:o_rw], LANE)], axis=1).astype(BF16)
    mu_rkv = mu[None, :3 * rww]
    mu_small = jnp.concatenate([
        _pad_cols(mu[None, 3 * rww:3 * rww + DECAY_LORA], LANE),
        _pad_cols(mu[None, 3 * rww + DECAY_LORA:3 * rww + DECAY_LORA + AAA_LORA], LANE),
        _pad_cols(mu[None, 3 * rww + DECAY_LORA + AAA_LORA:], 2 * LANE),
        jnp.zeros((1, LANE), F32)], axis=1)

    x2 = x.reshape(t, d)
    proj = _norm_proj(x2, norm_pre, w_main, BF16, tm=min(1024, t), tn=512, sigmoid_from_col=gate_col)
    small = _norm_proj(x2, norm_pre, w_small, F32, tm=min(1024, t), tn=SMALL_WIDTH)
    proj3 = proj.reshape(bsz, s, -1)
    small3 = small.reshape(bsz, s, SMALL_WIDTH)

    tq = min(256, s)
    qa = 0
    ya = _sb_attention(proj3, q_col=qa, k_col=qa + sb_heads, v_col=qa + 2 * sb_heads, heads=sb_heads, tq=tq)

    cum = _forget_cumsum(small3, _pad_cols(b_forget[None, :], LANE))
    c_hs = cum[:, :, :fx_heads].transpose(0, 2, 1)
    qb = 3 * sb_heads
    yb = _fox_attention(proj3, c_hs[..., None], c_hs.reshape(bsz, fx_heads, s // tq, tq),
                        q_col=qb, k_col=qb + fx_heads, v_col=qb + 2 * fx_heads, heads=fx_heads, tq=tq)

    rkv_col = (3 * sbw + 3 * fxw) // (3 * rww)
    assert rkv_col * 3 * rww == 3 * sbw + 3 * fxw
    row = lambda p: p.reshape(1, rww)
    r, lw, k, v, kn, bb, g = _rwkv_prep(
        proj3, small3, mu_rkv, mu_small, row(w0), _pad_rows(w_up, LANE).astype(BF16), row(a0),
        _pad_rows(a_up, LANE).astype(BF16), _pad_rows(g_up, 2 * LANE).astype(BF16), row(k_k), row(k_a),
        rkv_col=rkv_col, width=rww, tm=min(256, s))
    yc = _rwkv_scan(r, lw, k, v, kn, bb, g, row(r_k), row(ln_w), row(ln_b), tt=min(256, s))

    assert gate_col % (3 * d) == 0
    out = _mix_out(ya.reshape(t, sbw), yb.reshape(t, fxw), yc.reshape(t, rww), proj, x2,
                   pa.astype(BF16), pb.astype(BF16), pc.astype(BF16), w_out.astype(BF16), norm_post,
                   gate_col_block=gate_col // (3 * d), tm=min(256, t))
    return out.reshape(bsz, s, d)


def kernel(x, norm_mix_pre, norm_mix_post, norm_mlp_pre, norm_mlp_post, w_in, b_forget, rwkv_mu, rwkv_w0,
           rwkv_w_up, rwkv_a0, rwkv_a_up, rwkv_g_up, rwkv_k_k, rwkv_k_a, rwkv_r_k, rwkv_ln_w, rwkv_ln_b,
           w_branch_a, w_branch_b, w_branch_c, w_out, w_mlp_up, w_mlp_down):
    bsz, s, d = x.shape
    t = bsz * s
    for l in range(w_in.shape[0]):
        x = _mixer_layer(x, norm_mix_pre[l], norm_mix_post[l], w_in[l], b_forget[l], rwkv_mu[l], rwkv_w0[l],
                         rwkv_w_up[l], rwkv_a0[l], rwkv_a_up[l], rwkv_g_up[l], rwkv_k_k[l], rwkv_k_a[l],
                         rwkv_r_k[l], rwkv_ln_w[l], rwkv_ln_b[l], w_branch_a[l], w_branch_b[l],
                         w_branch_c[l], w_out[l])
        x = _mlp(x.reshape(t, d), norm_mlp_pre[l], w_mlp_up[l].astype(BF16), w_mlp_down[l].astype(BF16),
                 norm_mlp_post[l], tm=min(512, t), tf=512).reshape(bsz, s, d)
    return x
```

```python
import functools

import jax
import jax.numpy as jnp
from jax import lax
from jax.experimental import pallas as pl
from jax.experimental.pallas import tpu as pltpu

F32 = jnp.float32
BF16 = jnp.bfloat16

RMS_EPS = 1e-6
RWKV_GN_EPS = 64e-5
ATTN_HEAD_DIM = 128
RWKV_HEAD_DIM = 64
RWKV_CHUNK = 64
RWKV_GROUP = 256
DECAY_LORA = 64
AAA_LORA = 64
GATE_LORA = 160
LANE = 128
MASK_VALUE = -1e30
LOG2E = 1.4426950408889634
SUFFIX_BLOCK = 256
VMEM_LIMIT = 56 * 1024 * 1024

SMALL_WD = 0
SMALL_AD = 128
SMALL_GD = 256
SMALL_F = 512
SMALL_WIDTH = 640


def _dot(a, b):
    return jnp.dot(a, b, preferred_element_type=F32)


def _dot_nt(a, b):
    return lax.dot_general(a, b, (((1,), (1,)), ((), ())), preferred_element_type=F32)


def _dot_tn(a, b):
    return lax.dot_general(a, b, (((0,), (0,)), ((), ())), preferred_element_type=F32)


def _split_dot_rhs(m_bf16, x, parts):
    acc = None
    rem = x
    for p in range(parts):
        h = rem.astype(BF16)
        t = _dot(m_bf16, h)
        acc = t if acc is None else acc + t
        if p + 1 < parts:
            rem = rem - h.astype(F32)
    return acc


def _split_dot_lhs(x, m_bf16, parts):
    acc = None
    rem = x
    for p in range(parts):
        h = rem.astype(BF16)
        t = _dot(h, m_bf16)
        acc = t if acc is None else acc + t
        if p + 1 < parts:
            rem = rem - h.astype(F32)
    return acc


def _log_sigmoid(z):
    return jnp.minimum(z, 0.0) - jnp.log(1.0 + jnp.exp(-jnp.abs(z)))


def _sigmoid(z):
    return 1.0 / (1.0 + jnp.exp(-z))


def _params(*sem):
    return pltpu.CompilerParams(dimension_semantics=sem, vmem_limit_bytes=VMEM_LIMIT)


def _repack_kernel(w_ref, o_ref):
    for l in range(o_ref.shape[0]):
        o_ref[l] = w_ref[:, l, :].astype(o_ref.dtype)


def _repack_w_in(w_t, segments, *, tn):
    n_in, layers, k = w_t.shape
    n_out = sum(width for _, width in segments)
    assert all(width % tn == 0 for _, width in segments)

    def src_row(j):
        row = j * tn
        packed = 0
        start = jnp.int32(0)
        for seg_start, width in segments:
            start = jnp.where(row >= packed, seg_start - packed, start)
            packed += width
        return row + start

    return pl.pallas_call(
        _repack_kernel,
        grid=(n_out // tn,),
        in_specs=[pl.BlockSpec((pl.Element(tn), pl.Element(layers), pl.Element(k)),
                               lambda j: (src_row(j), 0, 0))],
        out_specs=pl.BlockSpec((layers, tn, k), lambda j: (0, j, 0)),
        out_shape=jax.ShapeDtypeStruct((layers, n_out, k), BF16),
        compiler_params=_params("parallel"),
        name="repack_w_in",
    )(w_t)


def _norm_proj_kernel(x_ref, g_ref, w_ref, o_ref, u_ref, *, sigmoid_from):
    n = pl.program_id(1)

    @pl.when(n == 0)
    def _():
        x = x_ref[...]
        ms = jnp.mean(x * x, axis=-1, keepdims=True)
        u_ref[...] = (x * lax.rsqrt(ms + RMS_EPS) * g_ref[...]).astype(BF16)

    acc = _dot_nt(u_ref[...], w_ref[...].astype(BF16))
    if sigmoid_from is None:
        o_ref[...] = acc.astype(o_ref.dtype)
    else:
        @pl.when(n < sigmoid_from)
        def _():
            o_ref[...] = acc.astype(o_ref.dtype)

        @pl.when(n >= sigmoid_from)
        def _():
            o_ref[...] = _sigmoid(acc).astype(o_ref.dtype)


def _norm_proj(x2, gain, w_t, layer, out_dtype, *, tm, tn, sigmoid_from_col=None):
    t, d = x2.shape
    n = w_t.shape[1]
    assert t % tm == 0 and n % tn == 0
    sig = None if sigmoid_from_col is None else sigmoid_from_col // tn
    if sigmoid_from_col is not None:
        assert sigmoid_from_col % tn == 0
    return pl.pallas_call(
        functools.partial(_norm_proj_kernel, sigmoid_from=sig),
        grid=(t // tm, n // tn),
        in_specs=[
            pl.BlockSpec((tm, d), lambda m, j: (m, 0)),
            pl.BlockSpec((1, d), lambda m, j: (0, 0)),
            pl.BlockSpec((None, tn, d), lambda m, j: (layer, j, 0)),
        ],
        out_specs=pl.BlockSpec((tm, tn), lambda m, j: (m, j)),
        out_shape=jax.ShapeDtypeStruct((t, n), out_dtype),
        scratch_shapes=[pltpu.VMEM((tm, d), BF16)],
        compiler_params=_params("parallel", "arbitrary"),
        name="norm_proj",
    )(x2, gain.reshape(1, d), w_t)


def _forget_cumsum_kernel(f_ref, b_ref, c_ref, *, cb):
    s = f_ref.shape[1]
    row = lax.broadcasted_iota(jnp.int32, (cb, cb), 0)
    col = lax.broadcasted_iota(jnp.int32, (cb, cb), 1)
    tri = jnp.where(col <= row, 1.0, 0.0).astype(BF16)
    carry = jnp.zeros((1, LANE), F32)
    for j in range(s // cb):
        lf = _log_sigmoid(f_ref[0, j * cb:(j + 1) * cb, :] + b_ref[...])
        c = _split_dot_rhs(tri, lf, 3) + carry
        c_ref[0, j * cb:(j + 1) * cb, :] = c
        carry = c[cb - 1:cb, :]


def _forget_cumsum(small3, b_forget_pad):
    b, s, _ = small3.shape
    cb = min(256, s)
    return pl.pallas_call(
        functools.partial(_forget_cumsum_kernel, cb=cb),
        grid=(b,),
        in_specs=[
            pl.BlockSpec((1, s, LANE), lambda i: (i, 0, SMALL_F // LANE)),
            pl.BlockSpec((1, LANE), lambda i: (0, 0)),
        ],
        out_specs=pl.BlockSpec((1, s, LANE), lambda i: (i, 0, 0)),
        out_shape=jax.ShapeDtypeStruct((b, s, LANE), F32),
        compiler_params=_params("parallel"),
        name="forget_cumsum",
    )(small3, b_forget_pad)


def _sb_kernel(q_ref, k_ref, v_ref, o_ref, *, tq, scale):
    i = pl.program_id(2)
    sub = min(SUFFIX_BLOCK, tq)
    nsub = tq // sub
    q = (q_ref[0].astype(F32) * (scale * LOG2E)).astype(BF16)
    srow = lax.broadcasted_iota(jnp.int32, (sub, sub), 0)
    scol = lax.broadcasted_iota(jnp.int32, (sub, sub), 1)
    upper = jnp.where(srow > scol, 1.0, 0.0).astype(BF16)

    def block(j, carry, acc, diagonal):
        start = pl.multiple_of(j * tq, tq)
        kb = k_ref[0, pl.ds(start, tq), :]
        vb = v_ref[0, pl.ds(start, tq), :]
        w = _dot_nt(q, kb)
        log_beta = jnp.minimum(w, 0.0) - jnp.log(1.0 + jnp.exp2(-jnp.abs(w))) * LOG2E
        log_rest = log_beta - w
        if diagonal:
            strict = (lax.broadcasted_iota(jnp.int32, (tq, tq), 1)
                      < lax.broadcasted_iota(jnp.int32, (tq, tq), 0))
            log_rest = jnp.where(strict, log_rest, 0.0)
        pieces = [None] * nsub
        for sb in reversed(range(nsub)):
            x = log_rest[:, sb * sub:(sb + 1) * sub]
            pieces[sb] = _split_dot_lhs(x, upper, 2) + carry
            carry = carry + jnp.sum(x, axis=1, keepdims=True)
        between = pieces[0] if nsub == 1 else jnp.concatenate(pieces, axis=1)
        a = jnp.exp2(log_beta + between)
        if diagonal:
            a = jnp.where(strict, a, 0.0)
        acc = acc + _dot(a.astype(BF16), vb)
        return carry, acc

    carry0 = jnp.zeros((tq, 1), F32)
    acc0 = jnp.zeros((tq, ATTN_HEAD_DIM), F32)
    carry, acc = block(i, carry0, acc0, True)

    def body(jj, state):
        return block(i - 1 - jj, state[0], state[1], False)

    carry, acc = lax.fori_loop(0, i, body, (carry, acc))
    o_ref[0] = acc.astype(o_ref.dtype)


def _sb_attention(proj3, *, q_col, k_col, v_col, heads, tq):
    b, s, _ = proj3.shape
    dh = ATTN_HEAD_DIM
    return pl.pallas_call(
        functools.partial(_sb_kernel, tq=tq, scale=dh ** -0.5),
        grid=(b, heads, s // tq),
        in_specs=[
            pl.BlockSpec((1, tq, dh), lambda bi, h, i: (bi, i, q_col + h)),
            pl.BlockSpec((1, s, dh), lambda bi, h, i: (bi, 0, k_col + h)),
            pl.BlockSpec((1, s, dh), lambda bi, h, i: (bi, 0, v_col + h)),
        ],
        out_specs=pl.BlockSpec((1, tq, dh), lambda bi, h, i: (bi, i, h)),
        out_shape=jax.ShapeDtypeStruct((b, s, heads * dh), BF16),
        compiler_params=_params("parallel", "parallel", "arbitrary"),
        name="sb_attention",
    )(proj3, proj3, proj3)


def _fox_kernel(q_ref, k_ref, v_ref, cum_ref, ck_ref, o_ref, *, tq, scale):
    h = pl.program_id(1)
    i = pl.program_id(2)
    q = (q_ref[0].astype(F32) * (scale * LOG2E)).astype(BF16)
    lane = lax.broadcasted_iota(jnp.int32, (tq, LANE), 1)
    cq = jnp.sum(jnp.where(lane == h, cum_ref[0], 0.0), axis=1, keepdims=True) * LOG2E

    def scores(j):
        start = pl.multiple_of(j * tq, tq)
        kb = k_ref[0, pl.ds(start, tq), :]
        ck = ck_ref[0, 0, pl.ds(j, 1), :] * LOG2E
        return _dot_nt(q, kb) + (cq - ck), v_ref[0, pl.ds(start, tq), :]

    z, vb = scores(i)
    causal = (lax.broadcasted_iota(jnp.int32, (tq, tq), 1)
              <= lax.broadcasted_iota(jnp.int32, (tq, tq), 0))
    z = jnp.where(causal, z, MASK_VALUE)
    m = jnp.max(z, axis=1, keepdims=True)
    p = jnp.exp2(z - m)
    l = jnp.sum(p, axis=1, keepdims=True)
    acc = _dot(p.astype(BF16), vb)

    def body(jj, state):
        m, l, acc = state
        z, vb = scores(i - 1 - jj)
        m_new = jnp.maximum(m, jnp.max(z, axis=1, keepdims=True))
        alpha = jnp.exp2(m - m_new)
        p = jnp.exp2(z - m_new)
        l = alpha * l + jnp.sum(p, axis=1, keepdims=True)
        acc = alpha * acc + _dot(p.astype(BF16), vb)
        return m_new, l, acc

    m, l, acc = lax.fori_loop(0, i, body, (m, l, acc))
    o_ref[0] = (acc / l).astype(o_ref.dtype)


def _fox_attention(proj3, cum, ck, *, q_col, k_col, v_col, heads, tq):
    b, s, _ = proj3.shape
    dh = ATTN_HEAD_DIM
    nb = s // tq
    return pl.pallas_call(
        functools.partial(_fox_kernel, tq=tq, scale=dh ** -0.5),
        grid=(b, heads, nb),
        in_specs=[
            pl.BlockSpec((1, tq, dh), lambda bi, h, i: (bi, i, q_col + h)),
            pl.BlockSpec((1, s, dh), lambda bi, h, i: (bi, 0, k_col + h)),
            pl.BlockSpec((1, s, dh), lambda bi, h, i: (bi, 0, v_col + h)),
            pl.BlockSpec((1, tq, LANE), lambda bi, h, i: (bi, i, 0)),
            pl.BlockSpec((1, 1, nb, tq), lambda bi, h, i: (bi, h, 0, 0)),
        ],
        out_specs=pl.BlockSpec((1, tq, dh), lambda bi, h, i: (bi, i, h)),
        out_shape=jax.ShapeDtypeStruct((b, s, heads * dh), BF16),
        compiler_params=_params("parallel", "parallel", "arbitrary"),
        name="fox_attention",
    )(proj3, proj3, proj3, cum, ck)


def _same_head_mask(n):
    row = lax.broadcasted_iota(jnp.int32, (n, n), 0)
    col = lax.broadcasted_iota(jnp.int32, (n, n), 1)
    shift = RWKV_HEAD_DIM.bit_length() - 1
    return jnp.right_shift(row, shift) == jnp.right_shift(col, shift)


def _rwkv_prep_kernel(z_ref, zs_ref, mu_ref, mus_ref, w0_ref, wup_ref, a0_ref, aup_ref, gup_ref,
                      kk_ref, ka_ref,
                      r_out, lw_out, k_out, v_out, kn_out, b_out, g_out,
                      zc_ref, zsc_ref, *, tm, width):
    si = pl.program_id(1)

    @pl.when(si == 0)
    def _():
        zc_ref[...] = jnp.zeros_like(zc_ref)
        zsc_ref[...] = jnp.zeros_like(zsc_ref)

    z = z_ref[0].astype(F32)
    zs = zs_ref[0]
    first = lax.broadcasted_iota(jnp.int32, (tm, 1), 0) == 0

    def shifted(x, carry_ref):
        prev = jnp.where(first, carry_ref[...], pltpu.roll(x, 1, 0))
        carry_ref[...] = x[tm - 1:tm, :]
        return prev

    z = z + (shifted(z, zc_ref) - z) * mu_ref[...]
    zs = zs + (shifted(zs, zsc_ref) - zs) * mus_ref[...]

    r = z[:, :width]
    k = z[:, width:2 * width]
    v = z[:, 2 * width:]
    wd = zs[:, SMALL_WD:SMALL_WD + LANE]
    ad = zs[:, SMALL_AD:SMALL_AD + LANE]
    gd = zs[:, SMALL_GD:SMALL_GD + 2 * LANE]

    wl = w0_ref[...] + _dot(jnp.tanh(wd).astype(BF16), wup_ref[...])
    softplus = jnp.maximum(-wl, 0.0) + jnp.log(1.0 + jnp.exp(-jnp.abs(wl)))
    lw_out[0] = -jnp.exp(-softplus - 0.5)
    a = _sigmoid(a0_ref[...] + _dot(ad.astype(BF16), aup_ref[...]))
    g_out[0] = _dot(_sigmoid(gd).astype(BF16), gup_ref[...]).astype(g_out.dtype)

    r_out[0] = r.astype(r_out.dtype)
    v_out[0] = v.astype(v_out.dtype)
    k_out[0] = (k * (1.0 + (a - 1.0) * ka_ref[...])).astype(k_out.dtype)

    kk_raw = k * kk_ref[...]
    ones_bd = jnp.where(_same_head_mask(RWKV_GROUP), 1.0, 0.0).astype(BF16)
    for gi in range(width // RWKV_GROUP):
        sl = slice(gi * RWKV_GROUP, (gi + 1) * RWKV_GROUP)
        x = kk_raw[:, sl]
        ss = _split_dot_lhs(x * x, ones_bd, 2)
        kn = x / jnp.maximum(jnp.sqrt(ss), 1e-12)
        kn_out[0, :, sl] = kn.astype(kn_out.dtype)
        b_out[0, :, sl] = (kn * a[:, sl]).astype(b_out.dtype)


def _rwkv_prep(proj3, small3, mu_rkv, mu_small, w0, w_up_p, a0, a_up_p, g_up_p, k_k, k_a, *, rkv_col, width, tm):
    b, s, _ = proj3.shape
    full = lambda shape: pl.BlockSpec(shape, lambda bi, si: (0,) * len(shape))
    tile = pl.BlockSpec((1, tm, width), lambda bi, si: (bi, si, 0))
    shp = lambda dt: jax.ShapeDtypeStruct((b, s, width), dt)
    return pl.pallas_call(
        functools.partial(_rwkv_prep_kernel, tm=tm, width=width),
        grid=(b, s // tm),
        in_specs=[
            pl.BlockSpec((1, tm, 3 * width), lambda bi, si: (bi, si, rkv_col)),
            pl.BlockSpec((1, tm, SMALL_WIDTH), lambda bi, si: (bi, si, 0)),
            full((1, 3 * width)), full((1, SMALL_WIDTH)),
            full((1, width)), full((LANE, width)),
            full((1, width)), full((LANE, width)),
            full((2 * LANE, width)),
            full((1, width)), full((1, width)),
        ],
        out_specs=[tile] * 7,
        out_shape=[shp(BF16), shp(F32), shp(BF16), shp(BF16), shp(BF16), shp(BF16), shp(BF16)],
        scratch_shapes=[pltpu.VMEM((1, 3 * width), F32), pltpu.VMEM((1, SMALL_WIDTH), F32)],
        compiler_params=_params("parallel", "arbitrary"),
        name="rwkv_prep",
    )(proj3, small3, mu_rkv, mu_small, w0, w_up_p, a0, a_up_p, g_up_p, k_k, k_a)


def _rwkv_scan_kernel(r_ref, lw_ref, k_ref, v_ref, kn_ref, b_ref, g_ref, rk_ref, lnw_ref, lnb_ref,
                      o_ref, s_ref, *, tt):
    st = pl.program_id(2)

    @pl.when(st == 0)
    def _():
        s_ref[...] = jnp.zeros_like(s_ref)

    c = RWKV_CHUNK
    w = RWKV_GROUP
    heads = w // RWKV_HEAD_DIM
    assert heads * c == w
    same = _same_head_mask(w)
    row = lax.broadcasted_iota(jnp.int32, (w, w), 0)
    col = lax.broadcasted_iota(jnp.int32, (w, w), 1)
    strict = same & (col < row)
    incl = same & (col <= row)
    crow = lax.broadcasted_iota(jnp.int32, (c, c), 0)
    ccol = lax.broadcasted_iota(jnp.int32, (c, c), 1)
    tri = jnp.where(ccol <= crow, 1.0, 0.0).astype(BF16)

    def tile(x):
        return jnp.concatenate([x] * heads, axis=0)

    def stack(x):
        return jnp.where(same, tile(x), 0.0).astype(BF16)

    def unstack(x):
        out = x[0:c]
        for h in range(1, heads):
            out = out + x[h * c:(h + 1) * c]
        return out

    nchunks = tt // c

    chunks = []
    for ci in range(nchunks):
        sl = slice(ci * c, (ci + 1) * c)
        lw = lw_ref[0, sl, :]
        r = r_ref[0, sl, :].astype(F32)
        k = k_ref[0, sl, :].astype(F32)
        v = v_ref[0, sl, :].astype(F32)
        kn = kn_ref[0, sl, :].astype(F32)
        bb = b_ref[0, sl, :].astype(F32)

        cum = _split_dot_rhs(tri, lw, 3)
        last = cum[c - 1:c, :]
        e_neg = jnp.exp(-cum)
        e_end = jnp.exp(last - cum)
        r_s = stack(r * jnp.exp(cum))
        a_s = stack(kn * jnp.exp(cum - lw))
        v_s = stack(v)
        b_t = tile((bb * e_neg).astype(BF16))
        k_t = tile((k * e_neg).astype(BF16))
        chunks.append(dict(
            r_s=r_s, a_s=a_s, v_s=v_s, w_end=jnp.exp(last),
            b_end=stack(bb * e_end), k_end=stack(k * e_end),
            pw=jnp.where(strict, -_dot_nt(a_s, b_t), 0.0),
            a_ak=jnp.where(strict, _dot_nt(a_s, k_t), 0.0).astype(BF16),
            a_rb=jnp.where(incl, _dot_nt(r_s, b_t), 0.0).astype(BF16),
            a_rk=jnp.where(incl, _dot_nt(r_s, k_t), 0.0).astype(BF16)))

    for ch in chunks:
        ch["nrm"] = ch["pw"]
    for _ in range(c.bit_length() - 2):
        for ch in chunks:
            pwb = ch["pw"].astype(BF16)
            pw = _dot(pwb, pwb)
            ch["nrm"] = ch["nrm"] + pw + _dot(ch["nrm"].astype(BF16), pw.astype(BF16))
            ch["pw"] = pw

    for ch in chunks:
        nrm_b = ch["nrm"].astype(BF16)
        ch["a_hat"] = (ch["a_s"].astype(F32) + _dot(nrm_b, ch["a_s"])).astype(BF16)
        akv = _dot(ch["a_ak"], ch["v_s"])
        ch["v_hat"] = akv + _dot(nrm_b, akv.astype(BF16))
        ch["q_s"] = _dot(ch["a_rk"], ch["v_s"])
        ch["kv"] = _dot_tn(ch["v_s"], ch["k_end"])

    state = s_ref[...]
    ys = []
    for ch in chunks:
        state_b = state.astype(BF16)
        u_b = (-(_dot_nt(ch["a_hat"], state_b) + ch["v_hat"])).astype(BF16)
        y_s = _dot_nt(ch["r_s"], state_b) + _dot(ch["a_rb"], u_b) + ch["q_s"]
        state = ch["w_end"] * state + _dot_tn(u_b, ch["b_end"]) + ch["kv"]
        ys.append(unstack(y_s))

    s_ref[...] = state
    y = jnp.concatenate(ys, axis=0)

    mean_bd = jnp.where(same, 1.0 / RWKV_HEAD_DIM, 0.0).astype(BF16)
    ones_bd = jnp.where(same, 1.0, 0.0).astype(BF16)
    mean = _split_dot_lhs(y, mean_bd, 2)
    d = y - mean
    var = _split_dot_lhs(d * d, mean_bd, 2)
    y = d * lax.rsqrt(var + RWKV_GN_EPS) * lnw_ref[...] + lnb_ref[...]
    r = r_ref[0].astype(F32)
    k = k_ref[0].astype(F32)
    v = v_ref[0].astype(F32)
    bonus = _split_dot_lhs(r * k * rk_ref[...], ones_bd, 2)
    y = y + bonus * v
    o_ref[0] = (y * g_ref[0].astype(F32)).astype(o_ref.dtype)


def _rwkv_scan(r, lw, k, v, kn, bb, g, r_k, ln_w, ln_b, *, tt):
    b, s, width = r.shape
    w = RWKV_GROUP
    tile = pl.BlockSpec((1, tt, w), lambda bi, gi, si: (bi, si, gi))
    vec = pl.BlockSpec((1, w), lambda bi, gi, si: (0, gi))
    return pl.pallas_call(
        functools.partial(_rwkv_scan_kernel, tt=tt),
        grid=(b, width // w, s // tt),
        in_specs=[tile] * 7 + [vec] * 3,
        out_specs=tile,
        out_shape=jax.ShapeDtypeStruct((b, s, width), BF16),
        scratch_shapes=[pltpu.VMEM((w, w), F32)],
        compiler_params=_params("parallel", "parallel", "arbitrary"),
        name="rwkv_scan",
    )(r, lw, k, v, kn, bb, g, r_k, ln_w, ln_b)


def _mix_out_kernel(ya_ref, yb_ref, yc_ref, gate_ref, x_ref, pa_ref, pb_ref, pc_ref, wo_ref, gain_ref, o_ref):
    d = x_ref.shape[1]
    m = gate_ref[:, 0:d].astype(F32) * _dot(ya_ref[...], pa_ref[...])
    m = m + gate_ref[:, d:2 * d].astype(F32) * _dot(yb_ref[...], pb_ref[...])
    m = m + gate_ref[:, 2 * d:3 * d].astype(F32) * _dot(yc_ref[...], pc_ref[...])
    o = _dot(m.astype(BF16), wo_ref[...])
    ms = jnp.mean(o * o, axis=-1, keepdims=True)
    o_ref[...] = x_ref[...] + o * lax.rsqrt(ms + RMS_EPS) * gain_ref[...]


def _mix_out(ya, yb, yc, proj2, x2, pa, pb, pc, wo, gain, *, gate_col_block, tm):
    t, d = x2.shape
    const = lambda shape: pl.BlockSpec(shape, lambda m: (0, 0), pipeline_mode=pl.Buffered(1))
    return pl.pallas_call(
        _mix_out_kernel,
        grid=(t // tm,),
        in_specs=[
            pl.BlockSpec((tm, ya.shape[1]), lambda m: (m, 0)),
            pl.BlockSpec((tm, yb.shape[1]), lambda m: (m, 0)),
            pl.BlockSpec((tm, yc.shape[1]), lambda m: (m, 0)),
            pl.BlockSpec((tm, 3 * d), lambda m: (m, gate_col_block)),
            pl.BlockSpec((tm, d), lambda m: (m, 0)),
            const(pa.shape), const(pb.shape), const(pc.shape), const(wo.shape),
            const((1, d)),
        ],
        out_specs=pl.BlockSpec((tm, d), lambda m: (m, 0)),
        out_shape=jax.ShapeDtypeStruct((t, d), F32),
        compiler_params=_params("parallel"),
        name="mix_out",
    )(ya, yb, yc, proj2, x2, pa, pb, pc, wo, gain.reshape(1, d))


def _mlp_kernel(x_ref, g1_ref, wu_ref, wd_ref, g2_ref, o_ref, u_ref, acc_ref):
    f = pl.program_id(1)

    @pl.when(f == 0)
    def _():
        x = x_ref[...]
        ms = jnp.mean(x * x, axis=-1, keepdims=True)
        u_ref[...] = (x * lax.rsqrt(ms + RMS_EPS) * g1_ref[...]).astype(BF16)
        acc_ref[...] = jnp.zeros_like(acc_ref)

    h = jnp.maximum(_dot(u_ref[...], wu_ref[...]), 0.0)
    acc_ref[...] += _dot((h * h).astype(BF16), wd_ref[...])

    @pl.when(f == pl.num_programs(1) - 1)
    def _():
        o = acc_ref[...]
        ms = jnp.mean(o * o, axis=-1, keepdims=True)
        o_ref[...] = x_ref[...] + o * lax.rsqrt(ms + RMS_EPS) * g2_ref[...]


def _mlp(x2, g1, wu, wd, g2, *, tm, tf):
    t, d = x2.shape
    ff = wu.shape[1]
    return pl.pallas_call(
        _mlp_kernel,
        grid=(t // tm, ff // tf),
        in_specs=[
            pl.BlockSpec((tm, d), lambda m, f: (m, 0)),
            pl.BlockSpec((1, d), lambda m, f: (0, 0)),
            pl.BlockSpec((d, tf), lambda m, f: (0, f)),
            pl.BlockSpec((tf, d), lambda m, f: (f, 0)),
            pl.BlockSpec((1, d), lambda m, f: (0, 0)),
        ],
        out_specs=pl.BlockSpec((tm, d), lambda m, f: (m, 0)),
        out_shape=jax.ShapeDtypeStruct((t, d), F32),
        scratch_shapes=[pltpu.VMEM((tm, d), BF16), pltpu.VMEM((tm, d), F32)],
        compiler_params=_params("parallel", "arbitrary"),
        name="mlp",
    )(x2, g1.reshape(1, d), wu, wd, g2.reshape(1, d))


def _pad_rows(w, rows):
    return jnp.pad(w, ((0, rows - w.shape[0]), (0, 0)))


def _pad_cols(w, cols):
    return jnp.pad(w, ((0, 0), (0, cols - w.shape[1])))


def _prepare_w_in(w_in, d, sbw, fxw, rww):
    fx_heads = fxw // ATTN_HEAD_DIM
    o_f = 3 * sbw + 3 * fxw
    o_rw = o_f + fx_heads
    o_wd = o_rw + 3 * rww
    o_ad = o_wd + DECAY_LORA
    o_gd = o_ad + AAA_LORA
    o_gate = o_gd + GATE_LORA
    assert w_in.shape[2] == o_gate + 3 * d
    w_t = jnp.transpose(w_in, (2, 0, 1))
    main = _repack_w_in(w_t, [(0, o_f), (o_rw, 3 * rww), (o_gate, 3 * d)], tn=256)

    def rows(lo, hi, padded):
        return jnp.pad(w_t[lo:hi], ((0, padded - (hi - lo)), (0, 0), (0, 0)))

    small = jnp.concatenate([rows(o_wd, o_ad, LANE), rows(o_ad, o_gd, LANE), rows(o_gd, o_gate, 2 * LANE),
                             rows(o_f, o_rw, LANE)], axis=0)
    return main, jnp.transpose(small, (1, 0, 2))


def _mixer_layer(x, layer, norm_pre, norm_post, w_main_t, w_small_t, b_forget, mu, w0, w_up, a0, a_up, g_up,
                 k_k, k_a, r_k, ln_w, ln_b, pa, pb, pc, w_out):
    bsz, s, d = x.shape
    t = bsz * s
    sbw = pa.shape[0]
    fxw = pb.shape[0]
    rww = pc.shape[0]
    sb_heads = sbw // ATTN_HEAD_DIM
    fx_heads = fxw // ATTN_HEAD_DIM

    gate_col = 3 * sbw + 3 * fxw + 3 * rww
    mu_rkv = mu[None, :3 * rww]
    mu_small = jnp.concatenate([
        _pad_cols(mu[None, 3 * rww:3 * rww + DECAY_LORA], LANE),
        _pad_cols(mu[None, 3 * rww + DECAY_LORA:3 * rww + DECAY_LORA + AAA_LORA], LANE),
        _pad_cols(mu[None, 3 * rww + DECAY_LORA + AAA_LORA:], 2 * LANE),
        jnp.zeros((1, LANE), F32)], axis=1)

    x2 = x.reshape(t, d)
    proj = _norm_proj(x2, norm_pre, w_main_t, layer, BF16, tm=min(1024, t), tn=512, sigmoid_from_col=gate_col)
    small = _norm_proj(x2, norm_pre, w_small_t, layer, F32, tm=min(1024, t), tn=SMALL_WIDTH)
    proj3 = proj.reshape(bsz, s, -1)
    small3 = small.reshape(bsz, s, SMALL_WIDTH)

    tq = min(512, s)
    qa = 0
    ya = _sb_attention(proj3, q_col=qa, k_col=qa + sb_heads, v_col=qa + 2 * sb_heads, heads=sb_heads, tq=tq)

    cum = _forget_cumsum(small3, _pad_cols(b_forget[None, :], LANE))
    c_hs = cum[:, :, :fx_heads].transpose(0, 2, 1)
    qb = 3 * sb_heads
    yb = _fox_attention(proj3, cum, c_hs.reshape(bsz, fx_heads, s // tq, tq),
                        q_col=qb, k_col=qb + fx_heads, v_col=qb + 2 * fx_heads, heads=fx_heads, tq=tq)

    rkv_col = (3 * sbw + 3 * fxw) // (3 * rww)
    assert rkv_col * 3 * rww == 3 * sbw + 3 * fxw
    row = lambda p: p.reshape(1, rww)
    r, lw, k, v, kn, bb, g = _rwkv_prep(
        proj3, small3, mu_rkv, mu_small, row(w0), _pad_rows(w_up, LANE).astype(BF16), row(a0),
        _pad_rows(a_up, LANE).astype(BF16), _pad_rows(g_up, 2 * LANE).astype(BF16), row(k_k), row(k_a),
        rkv_col=rkv_col, width=rww, tm=min(256, s))
    yc = _rwkv_scan(r, lw, k, v, kn, bb, g, row(r_k), row(ln_w), row(ln_b), tt=min(256, s))

    assert gate_col % (3 * d) == 0
    out = _mix_out(ya.reshape(t, sbw), yb.reshape(t, fxw), yc.reshape(t, rww), proj, x2,
                   pa.astype(BF16), pb.astype(BF16), pc.astype(BF16), w_out.astype(BF16), norm_post,
                   gate_col_block=gate_col // (3 * d), tm=min(256, t))
    return out.reshape(bsz, s, d)


def kernel(x, norm_mix_pre, norm_mix_post, norm_mlp_pre, norm_mlp_post, w_in, b_forget, rwkv_mu, rwkv_w0,
           rwkv_w_up, rwkv_a0, rwkv_a_up, rwkv_g_up, rwkv_k_k, rwkv_k_a, rwkv_r_k, rwkv_ln_w, rwkv_ln_b,
           w_branch_a, w_branch_b, w_branch_c, w_out, w_mlp_up, w_mlp_down):
    bsz, s, d = x.shape
    t = bsz * s
    w_main_t, w_small_t = _prepare_w_in(w_in, d, w_branch_a.shape[1], w_branch_b.shape[1], w_branch_c.shape[1])
    for l in range(w_in.shape[0]):
        x = _mixer_layer(x, l, norm_mix_pre[l], norm_mix_post[l], w_main_t, w_small_t, b_forget[l], rwkv_mu[l],
                         rwkv_w0[l], rwkv_w_up[l], rwkv_a0[l], rwkv_a_up[l], rwkv_g_up[l], rwkv_k_k[l],
                         rwkv_k_a[l], rwkv_r_k[l], rwkv_ln_w[l], rwkv_ln_b[l], w_branch_a[l], w_branch_b[l],
                         w_branch_c[l], w_out[l])
        x = _mlp(x.reshape(t, d), norm_mlp_pre[l], w_mlp_up[l].astype(BF16), w_mlp_down[l].astype(BF16),
                 norm_mlp_post[l], tm=min(512, t), tf=512).reshape(bsz, s, d)
    return x
```

```python
import functools

import jax
import jax.numpy as jnp
from jax import lax
from jax.experimental import pallas as pl
from jax.experimental.pallas import tpu as pltpu

F32 = jnp.float32
BF16 = jnp.bfloat16

RMS_EPS = 1e-6
RWKV_GN_EPS = 64e-5
ATTN_HEAD_DIM = 128
RWKV_HEAD_DIM = 64
RWKV_CHUNK = 64
RWKV_GROUP = 256
DECAY_LORA = 64
AAA_LORA = 64
GATE_LORA = 160
LANE = 128
MASK_VALUE = -1e30
LOG2E = 1.4426950408889634
SUFFIX_BLOCK = 256
PROJ_SUB_TILE = 512
VMEM_LIMIT = 56 * 1024 * 1024

SMALL_WD = 0
SMALL_AD = 128
SMALL_GD = 256
SMALL_F = 512
SMALL_WIDTH = 640


def _dot(a, b):
    return jnp.dot(a, b, preferred_element_type=F32)


def _dot_nt(a, b):
    return lax.dot_general(a, b, (((1,), (1,)), ((), ())), preferred_element_type=F32)


def _dot_tn(a, b):
    return lax.dot_general(a, b, (((0,), (0,)), ((), ())), preferred_element_type=F32)


def _split_dot_rhs(m_bf16, x, parts):
    acc = None
    rem = x
    for p in range(parts):
        h = rem.astype(BF16)
        t = _dot(m_bf16, h)
        acc = t if acc is None else acc + t
        if p + 1 < parts:
            rem = rem - h.astype(F32)
    return acc


def _split_dot_lhs(x, m_bf16, parts):
    acc = None
    rem = x
    for p in range(parts):
        h = rem.astype(BF16)
        t = _dot(h, m_bf16)
        acc = t if acc is None else acc + t
        if p + 1 < parts:
            rem = rem - h.astype(F32)
    return acc


def _log_sigmoid(z):
    return jnp.minimum(z, 0.0) - jnp.log(1.0 + jnp.exp(-jnp.abs(z)))


def _sigmoid(z):
    return 1.0 / (1.0 + jnp.exp(-z))


def _params(*sem):
    return pltpu.CompilerParams(dimension_semantics=sem, vmem_limit_bytes=VMEM_LIMIT)


def _repack_kernel(w_ref, o_ref):
    for l in range(o_ref.shape[0]):
        o_ref[l] = w_ref[:, l, :].astype(o_ref.dtype)


def _repack_w_in(w_t, segments, *, tn):
    n_in, layers, k = w_t.shape
    n_out = sum(width for _, width in segments)
    assert all(width % tn == 0 for _, width in segments)

    def src_row(j):
        row = j * tn
        packed = 0
        start = jnp.int32(0)
        for seg_start, width in segments:
            start = jnp.where(row >= packed, seg_start - packed, start)
            packed += width
        return row + start

    return pl.pallas_call(
        _repack_kernel,
        grid=(n_out // tn,),
        in_specs=[pl.BlockSpec((pl.Element(tn), pl.Element(layers), pl.Element(k)),
                               lambda j: (src_row(j), 0, 0))],
        out_specs=pl.BlockSpec((layers, tn, k), lambda j: (0, j, 0)),
        out_shape=jax.ShapeDtypeStruct((layers, n_out, k), BF16),
        compiler_params=_params("parallel"),
        name="repack_w_in",
    )(w_t)


def _norm_proj_kernel(x_ref, g_ref, w_ref, o_ref, u_ref, *, sigmoid_from, sub):
    n = pl.program_id(1)

    @pl.when(n == 0)
    def _():
        x = x_ref[...]
        ms = jnp.mean(x * x, axis=-1, keepdims=True)
        u_ref[...] = (x * lax.rsqrt(ms + RMS_EPS) * g_ref[...]).astype(BF16)

    def columns(apply_sigmoid):
        for c0 in range(0, o_ref.shape[1], sub):
            acc = _dot_nt(u_ref[...], w_ref[c0:c0 + sub, :].astype(BF16))
            if apply_sigmoid:
                acc = _sigmoid(acc)
            o_ref[:, c0:c0 + sub] = acc.astype(o_ref.dtype)

    if sigmoid_from is None:
        columns(False)
    else:
        pl.when(n < sigmoid_from)(lambda: columns(False))
        pl.when(n >= sigmoid_from)(lambda: columns(True))


def _norm_proj(x2, gain, w_t, layer, out_dtype, *, tm, tn, sigmoid_from_col=None):
    t, d = x2.shape
    n = w_t.shape[1]
    assert t % tm == 0 and n % tn == 0
    sig = None if sigmoid_from_col is None else sigmoid_from_col // tn
    if sigmoid_from_col is not None:
        assert sigmoid_from_col % tn == 0
    return pl.pallas_call(
        functools.partial(_norm_proj_kernel, sigmoid_from=sig,
                          sub=PROJ_SUB_TILE if tn % PROJ_SUB_TILE == 0 else tn),
        grid=(t // tm, n // tn),
        in_specs=[
            pl.BlockSpec((tm, d), lambda m, j: (m, 0)),
            pl.BlockSpec((1, d), lambda m, j: (0, 0)),
            pl.BlockSpec((None, tn, d), lambda m, j: (layer, j, 0)),
        ],
        out_specs=pl.BlockSpec((tm, tn), lambda m, j: (m, j)),
        out_shape=jax.ShapeDtypeStruct((t, n), out_dtype),
        scratch_shapes=[pltpu.VMEM((tm, d), BF16)],
        compiler_params=_params("parallel", "arbitrary"),
        name="norm_proj",
    )(x2, gain.reshape(1, d), w_t)


def _forget_cumsum_kernel(f_ref, b_ref, c_ref, *, cb):
    s = f_ref.shape[1]
    row = lax.broadcasted_iota(jnp.int32, (cb, cb), 0)
    col = lax.broadcasted_iota(jnp.int32, (cb, cb), 1)
    tri = jnp.where(col <= row, 1.0, 0.0).astype(BF16)
    carry = jnp.zeros((1, LANE), F32)
    for j in range(s // cb):
        lf = _log_sigmoid(f_ref[0, j * cb:(j + 1) * cb, :] + b_ref[...])
        c = _split_dot_rhs(tri, lf, 3) + carry
        c_ref[0, j * cb:(j + 1) * cb, :] = c
        carry = c[cb - 1:cb, :]


def _forget_cumsum(small3, b_forget_pad):
    b, s, _ = small3.shape
    cb = min(256, s)
    return pl.pallas_call(
        functools.partial(_forget_cumsum_kernel, cb=cb),
        grid=(b,),
        in_specs=[
            pl.BlockSpec((1, s, LANE), lambda i: (i, 0, SMALL_F // LANE)),
            pl.BlockSpec((1, LANE), lambda i: (0, 0)),
        ],
        out_specs=pl.BlockSpec((1, s, LANE), lambda i: (i, 0, 0)),
        out_shape=jax.ShapeDtypeStruct((b, s, LANE), F32),
        compiler_params=_params("parallel"),
        name="forget_cumsum",
    )(small3, b_forget_pad)


def _sb_kernel(q_ref, k_ref, v_ref, o_ref, *, tq, scale):
    i = pl.program_id(2)
    sub = min(SUFFIX_BLOCK, tq)
    nsub = tq // sub
    q = (q_ref[0].astype(F32) * (scale * LOG2E)).astype(BF16)
    srow = lax.broadcasted_iota(jnp.int32, (sub, sub), 0)
    scol = lax.broadcasted_iota(jnp.int32, (sub, sub), 1)
    upper = jnp.where(srow > scol, 1.0, 0.0).astype(BF16)

    def block(j, carry, acc, diagonal):
        start = pl.multiple_of(j * tq, tq)
        kb = k_ref[0, pl.ds(start, tq), :]
        vb = v_ref[0, pl.ds(start, tq), :]
        w = _dot_nt(q, kb)
        log_beta = jnp.minimum(w, 0.0) - jnp.log(1.0 + jnp.exp2(-jnp.abs(w))) * LOG2E
        log_rest = log_beta - w
        if diagonal:
            strict = (lax.broadcasted_iota(jnp.int32, (tq, tq), 1)
                      < lax.broadcasted_iota(jnp.int32, (tq, tq), 0))
            log_rest = jnp.where(strict, log_rest, 0.0)
        pieces = [None] * nsub
        for sb in reversed(range(nsub)):
            x = log_rest[:, sb * sub:(sb + 1) * sub]
            pieces[sb] = _split_dot_lhs(x, upper, 2) + carry
            carry = carry + jnp.sum(x, axis=1, keepdims=True)
        between = pieces[0] if nsub == 1 else jnp.concatenate(pieces, axis=1)
        a = jnp.exp2(log_beta + between)
        if diagonal:
            a = jnp.where(strict, a, 0.0)
        acc = acc + _dot(a.astype(BF16), vb)
        return carry, acc

    carry0 = jnp.zeros((tq, 1), F32)
    acc0 = jnp.zeros((tq, ATTN_HEAD_DIM), F32)
    carry, acc = block(i, carry0, acc0, True)
    odd = lax.rem(i, 2)
    carry, acc = lax.cond(odd == 1, lambda c, a: block(i - 1, c, a, False), lambda c, a: (c, a), carry, acc)

    def body(jj, state):
        j = i - 1 - odd - 2 * jj
        c1, a1 = block(j, state[0], state[1], False)
        return block(j - 1, c1, a1, False)

    carry, acc = lax.fori_loop(0, i // 2, body, (carry, acc))
    o_ref[0] = acc.astype(o_ref.dtype)


def _sb_attention(proj3, *, q_col, k_col, v_col, heads, tq):
    b, s, _ = proj3.shape
    dh = ATTN_HEAD_DIM
    return pl.pallas_call(
        functools.partial(_sb_kernel, tq=tq, scale=dh ** -0.5),
        grid=(b, heads, s // tq),
        in_specs=[
            pl.BlockSpec((1, tq, dh), lambda bi, h, i: (bi, i, q_col + h)),
            pl.BlockSpec((1, s, dh), lambda bi, h, i: (bi, 0, k_col + h)),
            pl.BlockSpec((1, s, dh), lambda bi, h, i: (bi, 0, v_col + h)),
        ],
        out_specs=pl.BlockSpec((1, tq, dh), lambda bi, h, i: (bi, i, h)),
        out_shape=jax.ShapeDtypeStruct((b, s, heads * dh), BF16),
        compiler_params=_params("parallel", "parallel", "arbitrary"),
        name="sb_attention",
    )(proj3, proj3, proj3)


def _fox_kernel(q_ref, k_ref, v_ref, cum_ref, ck_ref, o_ref, *, tq, scale):
    h = pl.program_id(1)
    i = pl.program_id(2)
    q = (q_ref[0].astype(F32) * (scale * LOG2E)).astype(BF16)
    lane = lax.broadcasted_iota(jnp.int32, (tq, LANE), 1)
    cq = jnp.sum(jnp.where(lane == h, cum_ref[0], 0.0), axis=1, keepdims=True) * LOG2E

    def scores(j):
        start = pl.multiple_of(j * tq, tq)
        kb = k_ref[0, pl.ds(start, tq), :]
        ck = ck_ref[0, 0, pl.ds(j, 1), :] * LOG2E
        return _dot_nt(q, kb) + (cq - ck), v_ref[0, pl.ds(start, tq), :]

    z, vb = scores(i)
    causal = (lax.broadcasted_iota(jnp.int32, (tq, tq), 1)
              <= lax.broadcasted_iota(jnp.int32, (tq, tq), 0))
    z = jnp.where(causal, z, MASK_VALUE)
    m = jnp.max(z, axis=1, keepdims=True)
    p = jnp.exp2(z - m)
    l = jnp.sum(p, axis=1, keepdims=True)
    acc = _dot(p.astype(BF16), vb)

    def step(j, m, l, acc):
        z, vb = scores(j)
        m_new = jnp.maximum(m, jnp.max(z, axis=1, keepdims=True))
        alpha = jnp.exp2(m - m_new)
        p = jnp.exp2(z - m_new)
        l = alpha * l + jnp.sum(p, axis=1, keepdims=True)
        acc = alpha * acc + _dot(p.astype(BF16), vb)
        return m_new, l, acc

    odd = lax.rem(i, 2)
    m, l, acc = lax.cond(odd == 1, lambda m, l, a: step(i - 1, m, l, a), lambda m, l, a: (m, l, a), m, l, acc)

    def body(jj, state):
        m, l, acc = state
        j = i - 1 - odd - 2 * jj
        z1, v1 = scores(j)
        z2, v2 = scores(j - 1)
        m_new = jnp.maximum(m, jnp.maximum(jnp.max(z1, axis=1, keepdims=True), jnp.max(z2, axis=1, keepdims=True)))
        alpha = jnp.exp2(m - m_new)
        p1 = jnp.exp2(z1 - m_new)
        p2 = jnp.exp2(z2 - m_new)
        l = alpha * l + (jnp.sum(p1, axis=1, keepdims=True) + jnp.sum(p2, axis=1, keepdims=True))
        acc = alpha * acc + (_dot(p1.astype(BF16), v1) + _dot(p2.astype(BF16), v2))
        return m_new, l, acc

    m, l, acc = lax.fori_loop(0, i // 2, body, (m, l, acc))
    o_ref[0] = (acc / l).astype(o_ref.dtype)


def _fox_attention(proj3, cum, ck, *, q_col, k_col, v_col, heads, tq):
    b, s, _ = proj3.shape
    dh = ATTN_HEAD_DIM
    nb = s // tq
    return pl.pallas_call(
        functools.partial(_fox_kernel, tq=tq, scale=dh ** -0.5),
        grid=(b, heads, nb),
        in_specs=[
            pl.BlockSpec((1, tq, dh), lambda bi, h, i: (bi, i, q_col + h)),
            pl.BlockSpec((1, s, dh), lambda bi, h, i: (bi, 0, k_col + h)),
            pl.BlockSpec((1, s, dh), lambda bi, h, i: (bi, 0, v_col + h)),
            pl.BlockSpec((1, tq, LANE), lambda bi, h, i: (bi, i, 0)),
            pl.BlockSpec((1, 1, nb, tq), lambda bi, h, i: (bi, h, 0, 0)),
        ],
        out_specs=pl.BlockSpec((1, tq, dh), lambda bi, h, i: (bi, i, h)),
        out_shape=jax.ShapeDtypeStruct((b, s, heads * dh), BF16),
        compiler_params=_params("parallel", "parallel", "arbitrary"),
        name="fox_attention",
    )(proj3, proj3, proj3, cum, ck)


def _same_head_mask(n):
    row = lax.broadcasted_iota(jnp.int32, (n, n), 0)
    col = lax.broadcasted_iota(jnp.int32, (n, n), 1)
    shift = RWKV_HEAD_DIM.bit_length() - 1
    return jnp.right_shift(row, shift) == jnp.right_shift(col, shift)


def _rwkv_prep_kernel(z_ref, zs_ref, mu_ref, mus_ref, w0_ref, wup_ref, a0_ref, aup_ref, gup_ref,
                      kk_ref, ka_ref,
                      r_out, lw_out, k_out, v_out, kn_out, b_out, g_out,
                      zc_ref, zsc_ref, *, tm, width):
    si = pl.program_id(1)

    @pl.when(si == 0)
    def _():
        zc_ref[...] = jnp.zeros_like(zc_ref)
        zsc_ref[...] = jnp.zeros_like(zsc_ref)

    z = z_ref[0].astype(F32)
    zs = zs_ref[0]
    first = lax.broadcasted_iota(jnp.int32, (tm, 1), 0) == 0

    def shifted(x, carry_ref):
        prev = jnp.where(first, carry_ref[...], pltpu.roll(x, 1, 0))
        carry_ref[...] = x[tm - 1:tm, :]
        return prev

    z = z + (shifted(z, zc_ref) - z) * mu_ref[...]
    zs = zs + (shifted(zs, zsc_ref) - zs) * mus_ref[...]

    r = z[:, :width]
    k = z[:, width:2 * width]
    v = z[:, 2 * width:]
    wd = zs[:, SMALL_WD:SMALL_WD + LANE]
    ad = zs[:, SMALL_AD:SMALL_AD + LANE]
    gd = zs[:, SMALL_GD:SMALL_GD + 2 * LANE]

    wl = w0_ref[...] + _dot(jnp.tanh(wd).astype(BF16), wup_ref[...])
    softplus = jnp.maximum(-wl, 0.0) + jnp.log(1.0 + jnp.exp(-jnp.abs(wl)))
    lw_out[0] = -jnp.exp(-softplus - 0.5)
    a = _sigmoid(a0_ref[...] + _dot(ad.astype(BF16), aup_ref[...]))
    g_out[0] = _dot(_sigmoid(gd).astype(BF16), gup_ref[...]).astype(g_out.dtype)

    r_out[0] = r.astype(r_out.dtype)
    v_out[0] = v.astype(v_out.dtype)
    k_out[0] = (k * (1.0 + (a - 1.0) * ka_ref[...])).astype(k_out.dtype)

    kk_raw = k * kk_ref[...]
    ones_bd = jnp.where(_same_head_mask(RWKV_GROUP), 1.0, 0.0).astype(BF16)
    for gi in range(width // RWKV_GROUP):
        sl = slice(gi * RWKV_GROUP, (gi + 1) * RWKV_GROUP)
        x = kk_raw[:, sl]
        ss = _split_dot_lhs(x * x, ones_bd, 2)
        kn = x / jnp.maximum(jnp.sqrt(ss), 1e-12)
        kn_out[0, :, sl] = kn.astype(kn_out.dtype)
        b_out[0, :, sl] = (kn * a[:, sl]).astype(b_out.dtype)


def _rwkv_prep(proj3, small3, mu_rkv, mu_small, w0, w_up_p, a0, a_up_p, g_up_p, k_k, k_a, *, rkv_col, width, tm):
    b, s, _ = proj3.shape
    full = lambda shape: pl.BlockSpec(shape, lambda bi, si: (0,) * len(shape))
    tile = pl.BlockSpec((1, tm, width), lambda bi, si: (bi, si, 0))
    shp = lambda dt: jax.ShapeDtypeStruct((b, s, width), dt)
    return pl.pallas_call(
        functools.partial(_rwkv_prep_kernel, tm=tm, width=width),
        grid=(b, s // tm),
        in_specs=[
            pl.BlockSpec((1, tm, 3 * width), lambda bi, si: (bi, si, rkv_col)),
            pl.BlockSpec((1, tm, SMALL_WIDTH), lambda bi, si: (bi, si, 0)),
            full((1, 3 * width)), full((1, SMALL_WIDTH)),
            full((1, width)), full((LANE, width)),
            full((1, width)), full((LANE, width)),
            full((2 * LANE, width)),
            full((1, width)), full((1, width)),
        ],
        out_specs=[tile] * 7,
        out_shape=[shp(BF16), shp(F32), shp(BF16), shp(BF16), shp(BF16), shp(BF16), shp(BF16)],
        scratch_shapes=[pltpu.VMEM((1, 3 * width), F32), pltpu.VMEM((1, SMALL_WIDTH), F32)],
        compiler_params=_params("parallel", "arbitrary"),
        name="rwkv_prep",
    )(proj3, small3, mu_rkv, mu_small, w0, w_up_p, a0, a_up_p, g_up_p, k_k, k_a)


def _rwkv_scan_kernel(r_ref, lw_ref, k_ref, v_ref, kn_ref, b_ref, g_ref, rk_ref, lnw_ref, lnb_ref,
                      o_ref, s_ref, *, tt):
    st = pl.program_id(2)

    @pl.when(st == 0)
    def _():
        s_ref[...] = jnp.zeros_like(s_ref)

    c = RWKV_CHUNK
    w = RWKV_GROUP
    heads = w // RWKV_HEAD_DIM
    assert heads * c == w
    same = _same_head_mask(w)
    row = lax.broadcasted_iota(jnp.int32, (w, w), 0)
    col = lax.broadcasted_iota(jnp.int32, (w, w), 1)
    strict = same & (col < row)
    incl = same & (col <= row)

    def tile(x):
        return jnp.concatenate([x] * heads, axis=0)

    def stack(x):
        return jnp.where(same, tile(x), 0.0).astype(BF16)

    def unstack(x):
        out = x[0:c]
        for h in range(1, heads):
            out = out + x[h * c:(h + 1) * c]
        return out

    nchunks = tt // c

    if tt == w:
        tri = jnp.where(incl, 1.0, 0.0).astype(BF16)
    else:
        trow = lax.broadcasted_iota(jnp.int32, (tt, tt), 0)
        tcol = lax.broadcasted_iota(jnp.int32, (tt, tt), 1)
        shift = c.bit_length() - 1
        tri = jnp.where((jnp.right_shift(trow, shift) == jnp.right_shift(tcol, shift)) & (tcol <= trow),
                        1.0, 0.0).astype(BF16)
    cum_all = _split_dot_rhs(tri, lw_ref[0], 3)

    chunks = []
    for ci in range(nchunks):
        sl = slice(ci * c, (ci + 1) * c)
        lw = lw_ref[0, sl, :]
        r = r_ref[0, sl, :].astype(F32)
        k = k_ref[0, sl, :].astype(F32)
        v = v_ref[0, sl, :].astype(F32)
        kn = kn_ref[0, sl, :].astype(F32)
        bb = b_ref[0, sl, :].astype(F32)
        cum = cum_all[sl]
        last = cum[c - 1:c, :]
        e_neg = jnp.exp(-cum)
        e_end = jnp.exp(last - cum)
        chunks.append(dict(
            r_s=stack(r * jnp.exp(cum)), a_s=stack(kn * jnp.exp(cum - lw)), v_s=stack(v),
            b_t=tile((bb * e_neg).astype(BF16)), k_t=tile((k * e_neg).astype(BF16)),
            b_end=stack(bb * e_end), k_end=stack(k * e_end), w_end=jnp.exp(last)))

    for ch in chunks:
        ch["pw"] = jnp.where(strict, -_dot_nt(ch["a_s"], ch["b_t"]), 0.0)
        ch["a_ak"] = jnp.where(strict, _dot_nt(ch["a_s"], ch["k_t"]), 0.0).astype(BF16)
        ch["a_rb"] = jnp.where(incl, _dot_nt(ch["r_s"], ch["b_t"]), 0.0).astype(BF16)
        ch["a_rk"] = jnp.where(incl, _dot_nt(ch["r_s"], ch["k_t"]), 0.0).astype(BF16)

    for ch in chunks:
        ch["nrm"] = ch["pw"]
    for _ in range(c.bit_length() - 2):
        for ch in chunks:
            pwb = ch["pw"].astype(BF16)
            ch["pw"] = _dot(pwb, pwb)
        for ch in chunks:
            ch["nrm"] = ch["nrm"] + ch["pw"] + _dot(ch["nrm"].astype(BF16), ch["pw"].astype(BF16))

    for ch in chunks:
        ch["nrm"] = ch["nrm"].astype(BF16)
        ch["a_hat"] = (ch["a_s"].astype(F32) + _dot(ch["nrm"], ch["a_s"])).astype(BF16)
        ch["akv"] = _dot(ch["a_ak"], ch["v_s"])
        ch["q_s"] = _dot(ch["a_rk"], ch["v_s"])
        ch["kv"] = _dot_tn(ch["v_s"], ch["k_end"])
    for ch in chunks:
        ch["v_hat"] = ch["akv"] + _dot(ch["nrm"], ch["akv"].astype(BF16))
        ch["x_mat"] = _dot_tn(ch["a_hat"], ch["b_end"]).astype(BF16)
    for ch in chunks:
        ch["g_mat"] = ch["kv"] - _dot_tn(ch["v_hat"].astype(BF16), ch["b_end"])

    state = s_ref[...]
    for ch in chunks:
        ch["s0"] = state.astype(BF16)
        state = ch["w_end"] * state - _dot(ch["s0"], ch["x_mat"]) + ch["g_mat"]
    s_ref[...] = state

    ys = []
    for ch in chunks:
        u_b = (-(_dot_nt(ch["a_hat"], ch["s0"]) + ch["v_hat"])).astype(BF16)
        ys.append(unstack(_dot_nt(ch["r_s"], ch["s0"]) + _dot(ch["a_rb"], u_b) + ch["q_s"]))
    y = jnp.concatenate(ys, axis=0)

    mean_bd = jnp.where(same, 1.0 / RWKV_HEAD_DIM, 0.0).astype(BF16)
    ones_bd = jnp.where(same, 1.0, 0.0).astype(BF16)
    mean = _split_dot_lhs(y, mean_bd, 2)
    d = y - mean
    var = _split_dot_lhs(d * d, mean_bd, 2)
    y = d * lax.rsqrt(var + RWKV_GN_EPS) * lnw_ref[...] + lnb_ref[...]
    r = r_ref[0].astype(F32)
    k = k_ref[0].astype(F32)
    v = v_ref[0].astype(F32)
    bonus = _split_dot_lhs(r * k * rk_ref[...], ones_bd, 2)
    y = y + bonus * v
    o_ref[0] = (y * g_ref[0].astype(F32)).astype(o_ref.dtype)


def _rwkv_scan(r, lw, k, v, kn, bb, g, r_k, ln_w, ln_b, *, tt):
    b, s, width = r.shape
    w = RWKV_GROUP
    tile = pl.BlockSpec((1, tt, w), lambda bi, gi, si: (bi, si, gi))
    vec = pl.BlockSpec((1, w), lambda bi, gi, si: (0, gi))
    return pl.pallas_call(
        functools.partial(_rwkv_scan_kernel, tt=tt),
        grid=(b, width // w, s // tt),
        in_specs=[tile] * 7 + [vec] * 3,
        out_specs=tile,
        out_shape=jax.ShapeDtypeStruct((b, s, width), BF16),
        scratch_shapes=[pltpu.VMEM((w, w), F32)],
        compiler_params=_params("parallel", "parallel", "arbitrary"),
        name="rwkv_scan",
    )(r, lw, k, v, kn, bb, g, r_k, ln_w, ln_b)


def _mix_out_kernel(ya_ref, yb_ref, yc_ref, gate_ref, x_ref, pa_ref, pb_ref, pc_ref, wo_ref, gain_ref, o_ref):
    d = x_ref.shape[1]
    m = gate_ref[:, 0:d].astype(F32) * _dot(ya_ref[...], pa_ref[...])
    m = m + gate_ref[:, d:2 * d].astype(F32) * _dot(yb_ref[...], pb_ref[...])
    m = m + gate_ref[:, 2 * d:3 * d].astype(F32) * _dot(yc_ref[...], pc_ref[...])
    o = _dot(m.astype(BF16), wo_ref[...])
    ms = jnp.mean(o * o, axis=-1, keepdims=True)
    o_ref[...] = x_ref[...] + o * lax.rsqrt(ms + RMS_EPS) * gain_ref[...]


def _mix_out(ya, yb, yc, proj2, x2, pa, pb, pc, wo, gain, *, gate_col_block, tm):
    t, d = x2.shape
    const = lambda shape: pl.BlockSpec(shape, lambda m: (0, 0), pipeline_mode=pl.Buffered(1))
    return pl.pallas_call(
        _mix_out_kernel,
        grid=(t // tm,),
        in_specs=[
            pl.BlockSpec((tm, ya.shape[1]), lambda m: (m, 0)),
            pl.BlockSpec((tm, yb.shape[1]), lambda m: (m, 0)),
            pl.BlockSpec((tm, yc.shape[1]), lambda m: (m, 0)),
            pl.BlockSpec((tm, 3 * d), lambda m: (m, gate_col_block)),
            pl.BlockSpec((tm, d), lambda m: (m, 0)),
            const(pa.shape), const(pb.shape), const(pc.shape), const(wo.shape),
            const((1, d)),
        ],
        out_specs=pl.BlockSpec((tm, d), lambda m: (m, 0)),
        out_shape=jax.ShapeDtypeStruct((t, d), F32),
        compiler_params=_params("parallel"),
        name="mix_out",
    )(ya, yb, yc, proj2, x2, pa, pb, pc, wo, gain.reshape(1, d))


def _mlp_kernel(x_ref, g1_ref, wu_ref, wd_ref, g2_ref, o_ref, u_ref, acc_ref):
    f = pl.program_id(1)

    @pl.when(f == 0)
    def _():
        x = x_ref[...]
        ms = jnp.mean(x * x, axis=-1, keepdims=True)
        u_ref[...] = (x * lax.rsqrt(ms + RMS_EPS) * g1_ref[...]).astype(BF16)
        acc_ref[...] = jnp.zeros_like(acc_ref)

    h = jnp.maximum(_dot(u_ref[...], wu_ref[...]), 0.0)
    acc_ref[...] += _dot((h * h).astype(BF16), wd_ref[...])

    @pl.when(f == pl.num_programs(1) - 1)
    def _():
        o = acc_ref[...]
        ms = jnp.mean(o * o, axis=-1, keepdims=True)
        o_ref[...] = x_ref[...] + o * lax.rsqrt(ms + RMS_EPS) * g2_ref[...]


def _mlp(x2, g1, wu, wd, g2, *, tm, tf):
    t, d = x2.shape
    ff = wu.shape[1]
    return pl.pallas_call(
        _mlp_kernel,
        grid=(t // tm, ff // tf),
        in_specs=[
            pl.BlockSpec((tm, d), lambda m, f: (m, 0)),
            pl.BlockSpec((1, d), lambda m, f: (0, 0)),
            pl.BlockSpec((d, tf), lambda m, f: (0, f)),
            pl.BlockSpec((tf, d), lambda m, f: (f, 0)),
            pl.BlockSpec((1, d), lambda m, f: (0, 0)),
        ],
        out_specs=pl.BlockSpec((tm, d), lambda m, f: (m, 0)),
        out_shape=jax.ShapeDtypeStruct((t, d), F32),
        scratch_shapes=[pltpu.VMEM((tm, d), BF16), pltpu.VMEM((tm, d), F32)],
        compiler_params=_params("parallel", "arbitrary"),
        name="mlp",
    )(x2, g1.reshape(1, d), wu, wd, g2.reshape(1, d))


def _pad_rows(w, rows):
    return jnp.pad(w, ((0, rows - w.shape[0]), (0, 0)))


def _pad_cols(w, cols):
    return jnp.pad(w, ((0, 0), (0, cols - w.shape[1])))


def _prepare_w_in(w_in, d, sbw, fxw, rww):
    fx_heads = fxw // ATTN_HEAD_DIM
    o_f = 3 * sbw + 3 * fxw
    o_rw = o_f + fx_heads
    o_wd = o_rw + 3 * rww
    o_ad = o_wd + DECAY_LORA
    o_gd = o_ad + AAA_LORA
    o_gate = o_gd + GATE_LORA
    assert w_in.shape[2] == o_gate + 3 * d
    w_t = jnp.transpose(w_in, (2, 0, 1))
    main = _repack_w_in(w_t, [(0, o_f), (o_rw, 3 * rww), (o_gate, 3 * d)], tn=256)

    def rows(lo, hi, padded):
        return jnp.pad(w_t[lo:hi], ((0, padded - (hi - lo)), (0, 0), (0, 0)))

    small = jnp.concatenate([rows(o_wd, o_ad, LANE), rows(o_ad, o_gd, LANE), rows(o_gd, o_gate, 2 * LANE),
                             rows(o_f, o_rw, LANE)], axis=0)
    return main, jnp.transpose(small, (1, 0, 2))


def _mixer_layer(x, layer, norm_pre, norm_post, w_main_t, w_small_t, b_forget, mu, w0, w_up, a0, a_up, g_up,
                 k_k, k_a, r_k, ln_w, ln_b, pa, pb, pc, w_out):
    bsz, s, d = x.shape
    t = bsz * s
    sbw = pa.shape[0]
    fxw = pb.shape[0]
    rww = pc.shape[0]
    sb_heads = sbw // ATTN_HEAD_DIM
    fx_heads = fxw // ATTN_HEAD_DIM

    gate_col = 3 * sbw + 3 * fxw + 3 * rww
    mu_rkv = mu[None, :3 * rww]
    mu_small = jnp.concatenate([
        _pad_cols(mu[None, 3 * rww:3 * rww + DECAY_LORA], LANE),
        _pad_cols(mu[None, 3 * rww + DECAY_LORA:3 * rww + DECAY_LORA + AAA_LORA], LANE),
        _pad_cols(mu[None, 3 * rww + DECAY_LORA + AAA_LORA:], 2 * LANE),
        jnp.zeros((1, LANE), F32)], axis=1)

    x2 = x.reshape(t, d)
    proj = _norm_proj(x2, norm_pre, w_main_t, layer, BF16, tm=min(1024, t), tn=1024, sigmoid_from_col=gate_col)
    small = _norm_proj(x2, norm_pre, w_small_t, layer, F32, tm=min(1024, t), tn=SMALL_WIDTH)
    proj3 = proj.reshape(bsz, s, -1)
    small3 = small.reshape(bsz, s, SMALL_WIDTH)

    tq = min(512, s)
    qa = 0
    ya = _sb_attention(proj3, q_col=qa, k_col=qa + sb_heads, v_col=qa + 2 * sb_heads, heads=sb_heads, tq=tq)

    cum = _forget_cumsum(small3, _pad_cols(b_forget[None, :], LANE))
    c_hs = cum[:, :, :fx_heads].transpose(0, 2, 1)
    qb = 3 * sb_heads
    yb = _fox_attention(proj3, cum, c_hs.reshape(bsz, fx_heads, s // tq, tq),
                        q_col=qb, k_col=qb + fx_heads, v_col=qb + 2 * fx_heads, heads=fx_heads, tq=tq)

    rkv_col = (3 * sbw + 3 * fxw) // (3 * rww)
    assert rkv_col * 3 * rww == 3 * sbw + 3 * fxw
    row = lambda p: p.reshape(1, rww)
    r, lw, k, v, kn, bb, g = _rwkv_prep(
        proj3, small3, mu_rkv, mu_small, row(w0), _pad_rows(w_up, LANE).astype(BF16), row(a0),
        _pad_rows(a_up, LANE).astype(BF16), _pad_rows(g_up, 2 * LANE).astype(BF16), row(k_k), row(k_a),
        rkv_col=rkv_col, width=rww, tm=min(256, s))
    yc = _rwkv_scan(r, lw, k, v, kn, bb, g, row(r_k), row(ln_w), row(ln_b), tt=min(512, s))

    assert gate_col % (3 * d) == 0
    out = _mix_out(ya.reshape(t, sbw), yb.reshape(t, fxw), yc.reshape(t, rww), proj, x2,
                   pa.astype(BF16), pb.astype(BF16), pc.astype(BF16), w_out.astype(BF16), norm_post,
                   gate_col_block=gate_col // (3 * d), tm=min(256, t))
    return out.reshape(bsz, s, d)


def kernel(x, norm_mix_pre, norm_mix_post, norm_mlp_pre, norm_mlp_post, w_in, b_forget, rwkv_mu, rwkv_w0,
           rwkv_w_up, rwkv_a0, rwkv_a_up, rwkv_g_up, rwkv_k_k, rwkv_k_a, rwkv_r_k, rwkv_ln_w, rwkv_ln_b,
           w_branch_a, w_branch_b, w_branch_c, w_out, w_mlp_up, w_mlp_down):
    bsz, s, d = x.shape
    t = bsz * s
    w_main_t, w_small_t = _prepare_w_in(w_in, d, w_branch_a.shape[1], w_branch_b.shape[1], w_branch_c.shape[1])
    for l in range(w_in.shape[0]):
        x = _mixer_layer(x, l, norm_mix_pre[l], norm_mix_post[l], w_main_t, w_small_t, b_forget[l], rwkv_mu[l],
                         rwkv_w0[l], rwkv_w_up[l], rwkv_a0[l], rwkv_a_up[l], rwkv_g_up[l], rwkv_k_k[l],
                         rwkv_k_a[l], rwkv_r_k[l], rwkv_ln_w[l], rwkv_ln_b[l], w_branch_a[l], w_branch_b[l],
                         w_branch_c[l], w_out[l])
        x = _mlp(x.reshape(t, d), norm_mlp_pre[l], w_mlp_up[l].astype(BF16), w_mlp_down[l].astype(BF16),
                 norm_mlp_post[l], tm=min(512, t), tf=512).reshape(bsz, s, d)
    return x
```

```python
import functools

import jax
import jax.numpy as jnp
from jax import lax
from jax.experimental import pallas as pl
from jax.experimental.pallas import tpu as pltpu

F32 = jnp.float32
BF16 = jnp.bfloat16

RMS_EPS = 1e-6
RWKV_GN_EPS = 64e-5
ATTN_HEAD_DIM = 128
RWKV_HEAD_DIM = 64
RWKV_CHUNK = 64
RWKV_GROUP = 256
DECAY_LORA = 64
AAA_LORA = 64
GATE_LORA = 160
LANE = 128
SUBLANE = 8
MASK_VALUE = -1e30
LOG2E = 1.4426950408889634
SUFFIX_BLOCK = 256
PROJ_SUB_TILE = 512
VMEM_LIMIT = 56 * 1024 * 1024

TILE = dict(
    repack_n=512,
    proj_m=1024, proj_n=1024,
    attn_q=512,
    rwkv_prep_m=256,
    rwkv_scan_t=512,
    mix_m=256,
    mlp_m=512, mlp_f=512,
)

SMALL_WD = 0
SMALL_AD = 128
SMALL_GD = 256
SMALL_F = 512
SMALL_WIDTH = 640


def _dot(a, b):
    return jnp.dot(a, b, preferred_element_type=F32)


def _dot_nt(a, b):
    return lax.dot_general(a, b, (((1,), (1,)), ((), ())), preferred_element_type=F32)


def _dot_tn(a, b):
    return lax.dot_general(a, b, (((0,), (0,)), ((), ())), preferred_element_type=F32)


def _bf16_terms(x, parts):
    terms = []
    rem = x
    for p in range(parts):
        h = rem.astype(BF16)
        terms.append(h)
        if p + 1 < parts:
            rem = rem - h.astype(F32)
    return terms


def _split_dot_rhs(m_bf16, x, parts):
    if parts == 1:
        return _dot(m_bf16, x.astype(BF16))
    return _dot(jnp.concatenate([m_bf16] * parts, axis=1), jnp.concatenate(_bf16_terms(x, parts), axis=0))


def _split_dot_lhs(x, m_bf16, parts):
    if parts == 1:
        return _dot(x.astype(BF16), m_bf16)
    return _dot(jnp.concatenate(_bf16_terms(x, parts), axis=1), jnp.concatenate([m_bf16] * parts, axis=0))


def _log_sigmoid(z):
    return jnp.minimum(z, 0.0) - jnp.log(1.0 + jnp.exp(-jnp.abs(z)))


def _sigmoid(z):
    return 1.0 / (1.0 + jnp.exp(-z))


def _params(*sem):
    return pltpu.CompilerParams(dimension_semantics=sem, vmem_limit_bytes=VMEM_LIMIT)


def _repack_kernel(w_ref, tail_ref, o_ref, *, tile_shifts, tail_tile, tn):
    j = pl.program_id(1)
    for shift in sorted(set(tile_shifts)):
        @pl.when(functools.reduce(jnp.logical_or, [j == t for t, s in enumerate(tile_shifts)
                                                   if s == shift and t != tail_tile]))
        def _(shift=shift):
            o_ref[0] = w_ref[0, shift:shift + tn, :].astype(o_ref.dtype)

    @pl.when(j == tail_tile)
    def _():
        shift = tile_shifts[tail_tile]
        o_ref[0] = tail_ref[0, shift:shift + tn, :].astype(o_ref.dtype)


def _repack_w_in(w_t, segments, *, tn):
    layers, n_in, k = w_t.shape
    n_out = sum(width for _, width in segments)
    assert all(width % tn == 0 for _, width in segments)
    win = tn + SUBLANE
    tile_starts = [start + off for start, width in segments for off in range(0, width, tn)]
    tile_shifts = [s % SUBLANE for s in tile_starts]
    tail_tile = len(tile_starts) - 1
    tail_from = tile_starts[tail_tile] - tile_shifts[tail_tile]
    assert all(s - s % SUBLANE + win <= n_in for s in tile_starts[:tail_tile])
    tail = jnp.pad(w_t[:, tail_from:], ((0, 0), (0, tail_from + win - n_in), (0, 0)))
    last_window = (n_in - win) // SUBLANE * SUBLANE

    def window_start(j):
        row = j * tn
        packed = 0
        start = jnp.int32(0)
        for seg_start, width in segments:
            start = jnp.where(row >= packed, seg_start - seg_start % SUBLANE - packed, start)
            packed += width
        return jnp.minimum(row + start, last_window)

    return pl.pallas_call(
        functools.partial(_repack_kernel, tile_shifts=tile_shifts, tail_tile=tail_tile, tn=tn),
        grid=(layers, n_out // tn),
        in_specs=[pl.BlockSpec((pl.Element(1), pl.Element(win), pl.Element(k)),
                               lambda l, j: (l, pl.multiple_of(window_start(j), SUBLANE), 0)),
                  pl.BlockSpec((1, win, k), lambda l, j: (l, 0, 0))],
        out_specs=pl.BlockSpec((1, tn, k), lambda l, j: (l, j, 0)),
        out_shape=jax.ShapeDtypeStruct((layers, n_out, k), BF16),
        compiler_params=_params("parallel", "parallel"),
        name="repack_w_in",
    )(w_t, tail)


def _norm_proj_kernel(x_ref, g_ref, w_ref, o_ref, u_ref, *, sigmoid_from, sub):
    n = pl.program_id(1)

    @pl.when(n == 0)
    def _():
        x = x_ref[...]
        ms = jnp.mean(x * x, axis=-1, keepdims=True)
        u_ref[...] = (x * lax.rsqrt(ms + RMS_EPS) * g_ref[...]).astype(BF16)

    def columns(apply_sigmoid):
        for c0 in range(0, o_ref.shape[1], sub):
            acc = _dot_nt(u_ref[...], w_ref[c0:c0 + sub, :].astype(BF16))
            if apply_sigmoid:
                acc = _sigmoid(acc)
            o_ref[:, c0:c0 + sub] = acc.astype(o_ref.dtype)

    if sigmoid_from is None:
        columns(False)
    else:
        pl.when(n < sigmoid_from)(lambda: columns(False))
        pl.when(n >= sigmoid_from)(lambda: columns(True))


def _norm_proj(x2, gain, w_t, layer, out_dtype, *, tm, tn, sigmoid_from_col=None):
    t, d = x2.shape
    n = w_t.shape[1]
    assert t % tm == 0 and n % tn == 0
    sig = None if sigmoid_from_col is None else sigmoid_from_col // tn
    if sigmoid_from_col is not None:
        assert sigmoid_from_col % tn == 0
    return pl.pallas_call(
        functools.partial(_norm_proj_kernel, sigmoid_from=sig,
                          sub=PROJ_SUB_TILE if tn % PROJ_SUB_TILE == 0 else tn),
        grid=(t // tm, n // tn),
        in_specs=[
            pl.BlockSpec((tm, d), lambda m, j: (m, 0)),
            pl.BlockSpec((1, d), lambda m, j: (0, 0)),
            pl.BlockSpec((None, tn, d), lambda m, j: (layer, j, 0)),
        ],
        out_specs=pl.BlockSpec((tm, tn), lambda m, j: (m, j)),
        out_shape=jax.ShapeDtypeStruct((t, n), out_dtype),
        scratch_shapes=[pltpu.VMEM((tm, d), BF16)],
        compiler_params=_params("parallel", "arbitrary"),
        name="norm_proj",
    )(x2, gain.reshape(1, d), w_t)


def _forget_cumsum_kernel(f_ref, b_ref, c_ref, *, cb):
    s = f_ref.shape[1]
    row = lax.broadcasted_iota(jnp.int32, (cb, cb), 0)
    col = lax.broadcasted_iota(jnp.int32, (cb, cb), 1)
    tri = jnp.where(col <= row, 1.0, 0.0).astype(BF16)
    carry = jnp.zeros((1, LANE), F32)
    for j in range(s // cb):
        lf = _log_sigmoid(f_ref[0, j * cb:(j + 1) * cb, :] + b_ref[...])
        c = _split_dot_rhs(tri, lf, 3) + carry
        c_ref[0, j * cb:(j + 1) * cb, :] = c
        carry = c[cb - 1:cb, :]


def _forget_cumsum(small3, b_forget_pad):
    b, s, _ = small3.shape
    cb = min(256, s)
    return pl.pallas_call(
        functools.partial(_forget_cumsum_kernel, cb=cb),
        grid=(b,),
        in_specs=[
            pl.BlockSpec((1, s, LANE), lambda i: (i, 0, SMALL_F // LANE)),
            pl.BlockSpec((1, LANE), lambda i: (0, 0)),
        ],
        out_specs=pl.BlockSpec((1, s, LANE), lambda i: (i, 0, 0)),
        out_shape=jax.ShapeDtypeStruct((b, s, LANE), F32),
        compiler_params=_params("parallel"),
        name="forget_cumsum",
    )(small3, b_forget_pad)


def _sb_kernel(q_ref, k_ref, v_ref, o_ref, *, tq, scale):
    i = pl.program_id(2)
    sub = min(SUFFIX_BLOCK, tq)
    nsub = tq // sub
    q = (q_ref[0].astype(F32) * (scale * LOG2E)).astype(BF16)
    srow = lax.broadcasted_iota(jnp.int32, (sub, sub), 0)
    scol = lax.broadcasted_iota(jnp.int32, (sub, sub), 1)
    upper = jnp.where(srow > scol, 1.0, 0.0).astype(BF16)

    def block(j, carry, acc, diagonal):
        start = pl.multiple_of(j * tq, tq)
        kb = k_ref[0, pl.ds(start, tq), :]
        vb = v_ref[0, pl.ds(start, tq), :]
        w = _dot_nt(q, kb)
        log_beta = jnp.minimum(w, 0.0) - jnp.log(1.0 + jnp.exp2(-jnp.abs(w))) * LOG2E
        log_rest = log_beta - w
        if diagonal:
            strict = (lax.broadcasted_iota(jnp.int32, (tq, tq), 1)
                      < lax.broadcasted_iota(jnp.int32, (tq, tq), 0))
            log_rest = jnp.where(strict, log_rest, 0.0)
        pieces = [None] * nsub
        for sb in reversed(range(nsub)):
            x = log_rest[:, sb * sub:(sb + 1) * sub]
            pieces[sb] = _split_dot_lhs(x, upper, 2) + carry
            carry = carry + jnp.sum(x, axis=1, keepdims=True)
        between = pieces[0] if nsub == 1 else jnp.concatenate(pieces, axis=1)
        a = jnp.exp2(log_beta + between)
        if diagonal:
            a = jnp.where(strict, a, 0.0)
        acc = acc + _dot(a.astype(BF16), vb)
        return carry, acc

    carry0 = jnp.zeros((tq, 1), F32)
    acc0 = jnp.zeros((tq, ATTN_HEAD_DIM), F32)
    carry, acc = block(i, carry0, acc0, True)
    odd = lax.rem(i, 2)
    carry, acc = lax.cond(odd == 1, lambda c, a: block(i - 1, c, a, False), lambda c, a: (c, a), carry, acc)

    def body(jj, state):
        j = i - 1 - odd - 2 * jj
        c1, a1 = block(j, state[0], state[1], False)
        return block(j - 1, c1, a1, False)

    carry, acc = lax.fori_loop(0, i // 2, body, (carry, acc))
    o_ref[0] = acc.astype(o_ref.dtype)


def _sb_attention(proj3, *, q_col, k_col, v_col, heads, tq):
    b, s, _ = proj3.shape
    dh = ATTN_HEAD_DIM
    return pl.pallas_call(
        functools.partial(_sb_kernel, tq=tq, scale=dh ** -0.5),
        grid=(b, heads, s // tq),
        in_specs=[
            pl.BlockSpec((1, tq, dh), lambda bi, h, i: (bi, i, q_col + h)),
            pl.BlockSpec((1, s, dh), lambda bi, h, i: (bi, 0, k_col + h)),
            pl.BlockSpec((1, s, dh), lambda bi, h, i: (bi, 0, v_col + h)),
        ],
        out_specs=pl.BlockSpec((1, tq, dh), lambda bi, h, i: (bi, i, h)),
        out_shape=jax.ShapeDtypeStruct((b, s, heads * dh), BF16),
        compiler_params=_params("parallel", "parallel", "arbitrary"),
        name="sb_attention",
    )(proj3, proj3, proj3)


def _fox_kernel(q_ref, k_ref, v_ref, cum_ref, ck_ref, o_ref, *, tq, scale):
    h = pl.program_id(1)
    i = pl.program_id(2)
    q = (q_ref[0].astype(F32) * (scale * LOG2E)).astype(BF16)
    lane = lax.broadcasted_iota(jnp.int32, (tq, LANE), 1)
    cq = jnp.sum(jnp.where(lane == h, cum_ref[0], 0.0), axis=1, keepdims=True) * LOG2E

    def scores(j):
        start = pl.multiple_of(j * tq, tq)
        kb = k_ref[0, pl.ds(start, tq), :]
        ck = ck_ref[0, 0, pl.ds(j, 1), :] * LOG2E
        return _dot_nt(q, kb) + (cq - ck), v_ref[0, pl.ds(start, tq), :]

    z, vb = scores(i)
    causal = (lax.broadcasted_iota(jnp.int32, (tq, tq), 1)
              <= lax.broadcasted_iota(jnp.int32, (tq, tq), 0))
    z = jnp.where(causal, z, MASK_VALUE)
    m = jnp.max(z, axis=1, keepdims=True)
    p = jnp.exp2(z - m)
    l = jnp.sum(p, axis=1, keepdims=True)
    acc = _dot(p.astype(BF16), vb)

    def step(j, m, l, acc):
        z, vb = scores(j)
        m_new = jnp.maximum(m, jnp.max(z, axis=1, keepdims=True))
        alpha = jnp.exp2(m - m_new)
        p = jnp.exp2(z - m_new)
        l = alpha * l + jnp.sum(p, axis=1, keepdims=True)
        acc = alpha * acc + _dot(p.astype(BF16), vb)
        return m_new, l, acc

    odd = lax.rem(i, 2)
    m, l, acc = lax.cond(odd == 1, lambda m, l, a: step(i - 1, m, l, a), lambda m, l, a: (m, l, a), m, l, acc)

    def body(jj, state):
        m, l, acc = state
        j = i - 1 - odd - 2 * jj
        z1, v1 = scores(j)
        z2, v2 = scores(j - 1)
        m_new = jnp.maximum(m, jnp.maximum(jnp.max(z1, axis=1, keepdims=True), jnp.max(z2, axis=1, keepdims=True)))
        alpha = jnp.exp2(m - m_new)
        p1 = jnp.exp2(z1 - m_new)
        p2 = jnp.exp2(z2 - m_new)
        l = alpha * l + (jnp.sum(p1, axis=1, keepdims=True) + jnp.sum(p2, axis=1, keepdims=True))
        acc = alpha * acc + (_dot(p1.astype(BF16), v1) + _dot(p2.astype(BF16), v2))
        return m_new, l, acc

    m, l, acc = lax.fori_loop(0, i // 2, body, (m, l, acc))
    o_ref[0] = (acc / l).astype(o_ref.dtype)


def _fox_attention(proj3, cum, ck, *, q_col, k_col, v_col, heads, tq):
    b, s, _ = proj3.shape
    dh = ATTN_HEAD_DIM
    nb = s // tq
    return pl.pallas_call(
        functools.partial(_fox_kernel, tq=tq, scale=dh ** -0.5),
        grid=(b, heads, nb),
        in_specs=[
            pl.BlockSpec((1, tq, dh), lambda bi, h, i: (bi, i, q_col + h)),
            pl.BlockSpec((1, s, dh), lambda bi, h, i: (bi, 0, k_col + h)),
            pl.BlockSpec((1, s, dh), lambda bi, h, i: (bi, 0, v_col + h)),
            pl.BlockSpec((1, tq, LANE), lambda bi, h, i: (bi, i, 0)),
            pl.BlockSpec((1, 1, nb, tq), lambda bi, h, i: (bi, h, 0, 0)),
        ],
        out_specs=pl.BlockSpec((1, tq, dh), lambda bi, h, i: (bi, i, h)),
        out_shape=jax.ShapeDtypeStruct((b, s, heads * dh), BF16),
        compiler_params=_params("parallel", "parallel", "arbitrary"),
        name="fox_attention",
    )(proj3, proj3, proj3, cum, ck)


def _same_head_mask(n):
    row = lax.broadcasted_iota(jnp.int32, (n, n), 0)
    col = lax.broadcasted_iota(jnp.int32, (n, n), 1)
    shift = RWKV_HEAD_DIM.bit_length() - 1
    return jnp.right_shift(row, shift) == jnp.right_shift(col, shift)


def _rwkv_prep_kernel(z_ref, zs_ref, mu_ref, mus_ref, w0_ref, wup_ref, a0_ref, aup_ref, gup_ref,
                      kk_ref, ka_ref,
                      r_out, lw_out, k_out, v_out, kn_out, b_out, g_out,
                      zc_ref, zsc_ref, *, tm, width):
    si = pl.program_id(1)

    @pl.when(si == 0)
    def _():
        zc_ref[...] = jnp.zeros_like(zc_ref)
        zsc_ref[...] = jnp.zeros_like(zsc_ref)

    z = z_ref[0].astype(F32)
    zs = zs_ref[0]
    first = lax.broadcasted_iota(jnp.int32, (tm, 1), 0) == 0

    def shifted(x, carry_ref):
        prev = jnp.where(first, carry_ref[...], pltpu.roll(x, 1, 0))
        carry_ref[...] = x[tm - 1:tm, :]
        return prev

    z = z + (shifted(z, zc_ref) - z) * mu_ref[...]
    zs = zs + (shifted(zs, zsc_ref) - zs) * mus_ref[...]

    r = z[:, :width]
    k = z[:, width:2 * width]
    v = z[:, 2 * width:]
    wd = zs[:, SMALL_WD:SMALL_WD + LANE]
    ad = zs[:, SMALL_AD:SMALL_AD + LANE]
    gd = zs[:, SMALL_GD:SMALL_GD + 2 * LANE]

    wl = w0_ref[...] + _dot(jnp.tanh(wd).astype(BF16), wup_ref[...])
    softplus = jnp.maximum(-wl, 0.0) + jnp.log(1.0 + jnp.exp(-jnp.abs(wl)))
    lw_out[0] = -jnp.exp(-softplus - 0.5)
    a = _sigmoid(a0_ref[...] + _dot(ad.astype(BF16), aup_ref[...]))
    g_out[0] = _dot(_sigmoid(gd).astype(BF16), gup_ref[...]).astype(g_out.dtype)

    r_out[0] = r.astype(r_out.dtype)
    v_out[0] = v.astype(v_out.dtype)
    k_out[0] = (k * (1.0 + (a - 1.0) * ka_ref[...])).astype(k_out.dtype)

    kk_raw = k * kk_ref[...]
    ones_bd = jnp.where(_same_head_mask(RWKV_GROUP), 1.0, 0.0).astype(BF16)
    for gi in range(width // RWKV_GROUP):
        sl = slice(gi * RWKV_GROUP, (gi + 1) * RWKV_GROUP)
        x = kk_raw[:, sl]
        ss = _split_dot_lhs(x * x, ones_bd, 2)
        kn = x / jnp.maximum(jnp.sqrt(ss), 1e-12)
        kn_out[0, :, sl] = kn.astype(kn_out.dtype)
        b_out[0, :, sl] = (kn * a[:, sl]).astype(b_out.dtype)


def _rwkv_prep(proj3, small3, mu_rkv, mu_small, w0, w_up_p, a0, a_up_p, g_up_p, k_k, k_a, *, rkv_col, width, tm):
    b, s, _ = proj3.shape
    full = lambda shape: pl.BlockSpec(shape, lambda bi, si: (0,) * len(shape))
    tile = pl.BlockSpec((1, tm, width), lambda bi, si: (bi, si, 0))
    shp = lambda dt: jax.ShapeDtypeStruct((b, s, width), dt)
    return pl.pallas_call(
        functools.partial(_rwkv_prep_kernel, tm=tm, width=width),
        grid=(b, s // tm),
        in_specs=[
            pl.BlockSpec((1, tm, 3 * width), lambda bi, si: (bi, si, rkv_col)),
            pl.BlockSpec((1, tm, SMALL_WIDTH), lambda bi, si: (bi, si, 0)),
            full((1, 3 * width)), full((1, SMALL_WIDTH)),
            full((1, width)), full((LANE, width)),
            full((1, width)), full((LANE, width)),
            full((2 * LANE, width)),
            full((1, width)), full((1, width)),
        ],
        out_specs=[tile] * 7,
        out_shape=[shp(BF16), shp(F32), shp(BF16), shp(BF16), shp(BF16), shp(BF16), shp(BF16)],
        scratch_shapes=[pltpu.VMEM((1, 3 * width), F32), pltpu.VMEM((1, SMALL_WIDTH), F32)],
        compiler_params=_params("parallel", "arbitrary"),
        name="rwkv_prep",
    )(proj3, small3, mu_rkv, mu_small, w0, w_up_p, a0, a_up_p, g_up_p, k_k, k_a)


def _rwkv_scan_kernel(r_ref, lw_ref, k_ref, v_ref, kn_ref, b_ref, g_ref, rk_ref, lnw_ref, lnb_ref,
                      o_ref, s_ref, *, tt):
    st = pl.program_id(2)

    @pl.when(st == 0)
    def _():
        s_ref[...] = jnp.zeros_like(s_ref)

    c = RWKV_CHUNK
    w = RWKV_GROUP
    heads = w // RWKV_HEAD_DIM
    assert heads * c == w
    same = _same_head_mask(w)
    row = lax.broadcasted_iota(jnp.int32, (w, w), 0)
    col = lax.broadcasted_iota(jnp.int32, (w, w), 1)
    strict = same & (col < row)
    incl = same & (col <= row)

    def tile(x):
        return jnp.concatenate([x] * heads, axis=0)

    def stack(x):
        return jnp.where(same, tile(x), 0.0).astype(BF16)

    def unstack(x):
        out = x[0:c]
        for h in range(1, heads):
            out = out + x[h * c:(h + 1) * c]
        return out

    nchunks = tt // c

    assert tt % w == 0
    tri = jnp.where(incl, 1.0, 0.0).astype(BF16)
    cum_all = jnp.concatenate([_split_dot_rhs(tri, lw_ref[0, r0:r0 + w, :], 3) for r0 in range(0, tt, w)],
                              axis=0)

    def operands(ci):
        sl = slice(ci * c, (ci + 1) * c)
        lw = lw_ref[0, sl, :]
        r = r_ref[0, sl, :].astype(F32)
        k = k_ref[0, sl, :].astype(F32)
        v = v_ref[0, sl, :].astype(F32)
        kn = kn_ref[0, sl, :].astype(F32)
        bb = b_ref[0, sl, :].astype(F32)
        cum = cum_all[sl]
        last = cum[c - 1:c, :]
        e_neg = jnp.exp(-cum)
        e_end = jnp.exp(last - cum)
        return dict(
            r_s=stack(r * jnp.exp(cum)), a_s=stack(kn * jnp.exp(cum - lw)), v_s=stack(v),
            bk=jnp.concatenate([(bb * e_neg).astype(BF16), (k * e_neg).astype(BF16)], axis=0),
            b_end=stack(bb * e_end), k_end=stack(k * e_end), w_end=jnp.exp(last))

    first_half = lax.broadcasted_iota(jnp.int32, (w, 2 * c), 1) < c

    def block_diag(x, swapped, use_first, mask):
        half = jnp.where(first_half, x, swapped) if use_first else jnp.where(first_half, swapped, x)
        return jnp.where(mask, jnp.concatenate([half] * (w // (2 * c)), axis=1), 0.0)

    def interactions(ch):
        prod = _dot_nt(jnp.concatenate([ch["a_s"], ch["r_s"]], axis=0), ch["bk"])
        pa, pr = prod[:w], prod[w:]
        sa, sr = pltpu.roll(pa, c, 1), pltpu.roll(pr, c, 1)
        ch["pw"] = -block_diag(pa, sa, True, strict)
        ch["nrm"] = ch["pw"]
        ch["a_ak"] = block_diag(pa, sa, False, strict).astype(BF16)
        ch["a_rb"] = block_diag(pr, sr, True, incl).astype(BF16)
        ch["a_rk"] = block_diag(pr, sr, False, incl).astype(BF16)
        return ch

    def neumann_level(group):
        for ch in group:
            pwb = ch["pw"].astype(BF16)
            ch["pw"] = _dot(pwb, pwb)
        for ch in group:
            ch["nrm"] = ch["nrm"] + ch["pw"] + _dot(ch["nrm"].astype(BF16), ch["pw"].astype(BF16))

    chunks = [operands(ci) for ci in range(nchunks)]
    for ch in chunks:
        interactions(ch)
    for _ in range(c.bit_length() - 2):
        neumann_level(chunks)

    def stage_a(ch):
        ch["nrm"] = ch["nrm"].astype(BF16)
        ch["a_hat"] = (ch["a_s"].astype(F32) + _dot(ch["nrm"], ch["a_s"])).astype(BF16)
        ch["akv"] = _dot(ch["a_ak"], ch["v_s"])
        ch["q_s"] = _dot(ch["a_rk"], ch["v_s"])
        ch["kv"] = _dot_tn(ch["v_s"], ch["k_end"])

    def stage_b(ch):
        ch["v_hat"] = ch["akv"] + _dot(ch["nrm"], ch["akv"].astype(BF16))
        ch["x_mat"] = _dot_tn(ch["a_hat"], ch["b_end"]).astype(BF16)

    def stage_c(ch, state):
        g_mat = ch["kv"] - _dot_tn(ch["v_hat"].astype(BF16), ch["b_end"])
        ch["s0"] = state.astype(BF16)
        return ch["w_end"] * state - _dot(ch["s0"], ch["x_mat"]) + g_mat

    def stage_out(ch):
        u_b = (-(_dot_nt(ch["a_hat"], ch["s0"]) + ch["v_hat"])).astype(BF16)
        return unstack(_dot_nt(ch["r_s"], ch["s0"]) + _dot(ch["a_rb"], u_b) + ch["q_s"])

    state = s_ref[...]
    ys = []
    for step in range(nchunks + 3):
        if step < nchunks:
            stage_a(chunks[step])
        if 0 <= step - 1 < nchunks:
            stage_b(chunks[step - 1])
        if 0 <= step - 2 < nchunks:
            state = stage_c(chunks[step - 2], state)
        if 0 <= step - 3 < nchunks:
            ys.append(stage_out(chunks[step - 3]))
    s_ref[...] = state
    y = jnp.concatenate(ys, axis=0)

    mean_bd = jnp.where(same, 1.0 / RWKV_HEAD_DIM, 0.0).astype(BF16)
    ones_bd = jnp.where(same, 1.0, 0.0).astype(BF16)
    mean = _split_dot_lhs(y, mean_bd, 2)
    d = y - mean
    var = _split_dot_lhs(d * d, mean_bd, 1)
    y = d * lax.rsqrt(var + RWKV_GN_EPS) * lnw_ref[...] + lnb_ref[...]
    r = r_ref[0].astype(F32)
    k = k_ref[0].astype(F32)
    v = v_ref[0].astype(F32)
    bonus = _split_dot_lhs(r * k * rk_ref[...], ones_bd, 1)
    y = y + bonus * v
    o_ref[0] = (y * g_ref[0].astype(F32)).astype(o_ref.dtype)


def _rwkv_scan(r, lw, k, v, kn, bb, g, r_k, ln_w, ln_b, *, tt):
    b, s, width = r.shape
    w = RWKV_GROUP
    tile = pl.BlockSpec((1, tt, w), lambda bi, gi, si: (bi, si, gi))
    vec = pl.BlockSpec((1, w), lambda bi, gi, si: (0, gi))
    return pl.pallas_call(
        functools.partial(_rwkv_scan_kernel, tt=tt),
        grid=(b, width // w, s // tt),
        in_specs=[tile] * 7 + [vec] * 3,
        out_specs=tile,
        out_shape=jax.ShapeDtypeStruct((b, s, width), BF16),
        scratch_shapes=[pltpu.VMEM((w, w), F32)],
        compiler_params=_params("parallel", "parallel", "arbitrary"),
        name="rwkv_scan",
    )(r, lw, k, v, kn, bb, g, r_k, ln_w, ln_b)


def _mix_out_kernel(ya_ref, yb_ref, yc_ref, gate_ref, x_ref, pa_ref, pb_ref, pc_ref, wo_ref, gain_ref, o_ref):
    d = x_ref.shape[1]
    m = gate_ref[:, 0:d].astype(F32) * _dot(ya_ref[...], pa_ref[...])
    m = m + gate_ref[:, d:2 * d].astype(F32) * _dot(yb_ref[...], pb_ref[...])
    m = m + gate_ref[:, 2 * d:3 * d].astype(F32) * _dot(yc_ref[...], pc_ref[...])
    o = _dot(m.astype(BF16), wo_ref[...])
    ms = jnp.mean(o * o, axis=-1, keepdims=True)
    o_ref[...] = x_ref[...] + o * lax.rsqrt(ms + RMS_EPS) * gain_ref[...]


def _mix_out(ya, yb, yc, proj2, x2, pa, pb, pc, wo, gain, *, gate_col_block, tm):
    t, d = x2.shape
    const = lambda shape: pl.BlockSpec(shape, lambda m: (0, 0), pipeline_mode=pl.Buffered(1))
    return pl.pallas_call(
        _mix_out_kernel,
        grid=(t // tm,),
        in_specs=[
            pl.BlockSpec((tm, ya.shape[1]), lambda m: (m, 0)),
            pl.BlockSpec((tm, yb.shape[1]), lambda m: (m, 0)),
            pl.BlockSpec((tm, yc.shape[1]), lambda m: (m, 0)),
            pl.BlockSpec((tm, 3 * d), lambda m: (m, gate_col_block)),
            pl.BlockSpec((tm, d), lambda m: (m, 0)),
            const(pa.shape), const(pb.shape), const(pc.shape), const(wo.shape),
            const((1, d)),
        ],
        out_specs=pl.BlockSpec((tm, d), lambda m: (m, 0)),
        out_shape=jax.ShapeDtypeStruct((t, d), F32),
        compiler_params=_params("parallel"),
        name="mix_out",
    )(ya, yb, yc, proj2, x2, pa, pb, pc, wo, gain.reshape(1, d))


def _mlp_kernel(x_ref, g1_ref, wu_ref, wd_ref, g2_ref, o_ref, u_ref, acc_ref):
    f = pl.program_id(1)

    @pl.when(f == 0)
    def _():
        x = x_ref[...]
        ms = jnp.mean(x * x, axis=-1, keepdims=True)
        u_ref[...] = (x * lax.rsqrt(ms + RMS_EPS) * g1_ref[...]).astype(BF16)
        acc_ref[...] = jnp.zeros_like(acc_ref)

    h = jnp.maximum(_dot(u_ref[...], wu_ref[...]), 0.0)
    acc_ref[...] += _dot((h * h).astype(BF16), wd_ref[...])

    @pl.when(f == pl.num_programs(1) - 1)
    def _():
        o = acc_ref[...]
        ms = jnp.mean(o * o, axis=-1, keepdims=True)
        o_ref[...] = x_ref[...] + o * lax.rsqrt(ms + RMS_EPS) * g2_ref[...]


def _mlp(x2, g1, wu, wd, g2, *, tm, tf):
    t, d = x2.shape
    ff = wu.shape[1]
    return pl.pallas_call(
        _mlp_kernel,
        grid=(t // tm, ff // tf),
        in_specs=[
            pl.BlockSpec((tm, d), lambda m, f: (m, 0)),
            pl.BlockSpec((1, d), lambda m, f: (0, 0)),
            pl.BlockSpec((d, tf), lambda m, f: (0, f)),
            pl.BlockSpec((tf, d), lambda m, f: (f, 0)),
            pl.BlockSpec((1, d), lambda m, f: (0, 0)),
        ],
        out_specs=pl.BlockSpec((tm, d), lambda m, f: (m, 0)),
        out_shape=jax.ShapeDtypeStruct((t, d), F32),
        scratch_shapes=[pltpu.VMEM((tm, d), BF16), pltpu.VMEM((tm, d), F32)],
        compiler_params=_params("parallel", "arbitrary"),
        name="mlp",
    )(x2, g1.reshape(1, d), wu, wd, g2.reshape(1, d))


def _pad_rows(w, rows):
    return jnp.pad(w, ((0, rows - w.shape[0]), (0, 0)))


def _pad_cols(w, cols):
    return jnp.pad(w, ((0, 0), (0, cols - w.shape[1])))


def _prepare_w_in(w_in, d, sbw, fxw, rww):
    fx_heads = fxw // ATTN_HEAD_DIM
    o_f = 3 * sbw + 3 * fxw
    o_rw = o_f + fx_heads
    o_wd = o_rw + 3 * rww
    o_ad = o_wd + DECAY_LORA
    o_gd = o_ad + AAA_LORA
    o_gate = o_gd + GATE_LORA
    assert w_in.shape[2] == o_gate + 3 * d
    w_t = jnp.transpose(w_in, (0, 2, 1))
    main = _repack_w_in(w_t, [(0, o_f), (o_rw, 3 * rww), (o_gate, 3 * d)], tn=TILE["repack_n"])

    def rows(lo, hi, padded):
        return jnp.pad(w_t[:, lo:hi], ((0, 0), (0, padded - (hi - lo)), (0, 0)))

    small = jnp.concatenate([rows(o_wd, o_ad, LANE), rows(o_ad, o_gd, LANE), rows(o_gd, o_gate, 2 * LANE),
                             rows(o_f, o_rw, LANE)], axis=1)
    return main, small


def _mixer_layer(x, layer, norm_pre, norm_post, w_main_t, w_small_t, b_forget, mu, w0, w_up, a0, a_up, g_up,
                 k_k, k_a, r_k, ln_w, ln_b, pa, pb, pc, w_out):
    bsz, s, d = x.shape
    t = bsz * s
    sbw = pa.shape[0]
    fxw = pb.shape[0]
    rww = pc.shape[0]
    sb_heads = sbw // ATTN_HEAD_DIM
    fx_heads = fxw // ATTN_HEAD_DIM

    gate_col = 3 * sbw + 3 * fxw + 3 * rww
    mu_rkv = mu[None, :3 * rww]
    mu_small = jnp.concatenate([
        _pad_cols(mu[None, 3 * rww:3 * rww + DECAY_LORA], LANE),
        _pad_cols(mu[None, 3 * rww + DECAY_LORA:3 * rww + DECAY_LORA + AAA_LORA], LANE),
        _pad_cols(mu[None, 3 * rww + DECAY_LORA + AAA_LORA:], 2 * LANE),
        jnp.zeros((1, LANE), F32)], axis=1)

    x2 = x.reshape(t, d)
    proj_m = min(TILE["proj_m"], t)
    proj = _norm_proj(x2, norm_pre, w_main_t, layer, BF16, tm=proj_m, tn=TILE["proj_n"],
                      sigmoid_from_col=gate_col)
    small = _norm_proj(x2, norm_pre, w_small_t, layer, F32, tm=proj_m, tn=SMALL_WIDTH)
    proj3 = proj.reshape(bsz, s, -1)
    small3 = small.reshape(bsz, s, SMALL_WIDTH)

    tq = min(TILE["attn_q"], s)
    qa = 0
    ya = _sb_attention(proj3, q_col=qa, k_col=qa + sb_heads, v_col=qa + 2 * sb_heads, heads=sb_heads, tq=tq)

    cum = _forget_cumsum(small3, _pad_cols(b_forget[None, :], LANE))
    c_hs = cum[:, :, :fx_heads].transpose(0, 2, 1)
    qb = 3 * sb_heads
    yb = _fox_attention(proj3, cum, c_hs.reshape(bsz, fx_heads, s // tq, tq),
                        q_col=qb, k_col=qb + fx_heads, v_col=qb + 2 * fx_heads, heads=fx_heads, tq=tq)

    rkv_col = (3 * sbw + 3 * fxw) // (3 * rww)
    assert rkv_col * 3 * rww == 3 * sbw + 3 * fxw
    row = lambda p: p.reshape(1, rww)
    r, lw, k, v, kn, bb, g = _rwkv_prep(
        proj3, small3, mu_rkv, mu_small, row(w0), _pad_rows(w_up, LANE).astype(BF16), row(a0),
        _pad_rows(a_up, LANE).astype(BF16), _pad_rows(g_up, 2 * LANE).astype(BF16), row(k_k), row(k_a),
        rkv_col=rkv_col, width=rww, tm=min(TILE["rwkv_prep_m"], s))
    yc = _rwkv_scan(r, lw, k, v, kn, bb, g, row(r_k), row(ln_w), row(ln_b), tt=min(TILE["rwkv_scan_t"], s))

    assert gate_col % (3 * d) == 0
    out = _mix_out(ya.reshape(t, sbw), yb.reshape(t, fxw), yc.reshape(t, rww), proj, x2,
                   pa.astype(BF16), pb.astype(BF16), pc.astype(BF16), w_out.astype(BF16), norm_post,
                   gate_col_block=gate_col // (3 * d), tm=min(TILE["mix_m"], t))
    return out.reshape(bsz, s, d)


def kernel(x, norm_mix_pre, norm_mix_post, norm_mlp_pre, norm_mlp_post, w_in, b_forget, rwkv_mu, rwkv_w0,
           rwkv_w_up, rwkv_a0, rwkv_a_up, rwkv_g_up, rwkv_k_k, rwkv_k_a, rwkv_r_k, rwkv_ln_w, rwkv_ln_b,
           w_branch_a, w_branch_b, w_branch_c, w_out, w_mlp_up, w_mlp_down):
    bsz, s, d = x.shape
    t = bsz * s
    w_main_t, w_small_t = _prepare_w_in(w_in, d, w_branch_a.shape[1], w_branch_b.shape[1], w_branch_c.shape[1])
    for l in range(w_in.shape[0]):
        x = _mixer_layer(x, l, norm_mix_pre[l], norm_mix_post[l], w_main_t, w_small_t, b_forget[l], rwkv_mu[l],
                         rwkv_w0[l], rwkv_w_up[l], rwkv_a0[l], rwkv_a_up[l], rwkv_g_up[l], rwkv_k_k[l],
                         rwkv_k_a[l], rwkv_r_k[l], rwkv_ln_w[l], rwkv_ln_b[l], w_branch_a[l], w_branch_b[l],
                         w_branch_c[l], w_out[l])
        x = _mlp(x.reshape(t, d), norm_mlp_pre[l], w_mlp_up[l].astype(BF16), w_mlp_down[l].astype(BF16),
                 norm_mlp_post[l], tm=min(TILE["mlp_m"], t), tf=TILE["mlp_f"]).reshape(bsz, s, d)
    return x
```

```python
import functools

import jax
import jax.numpy as jnp
from jax import lax
from jax.experimental import pallas as pl
from jax.experimental.pallas import tpu as pltpu

F32 = jnp.float32
BF16 = jnp.bfloat16

RMS_EPS = 1e-6
RWKV_GN_EPS = 64e-5
DECAY_SCALE = 0.6065306597126334
KEY_NORM_FLOOR = 1e-12
ATTN_HEAD_DIM = 128
RWKV_HEAD_DIM = 64
RWKV_CHUNK = 64
RWKV_GROUP = 256
DECAY_LORA = 64
AAA_LORA = 64
GATE_LORA = 160
LANE = 128
SUBLANE = 8
MASK_VALUE = -1e30
LOG2E = 1.4426950408889634
SUFFIX_BLOCK = 256
PROJ_SUB_TILE = 512
VMEM_LIMIT = 56 * 1024 * 1024

TILE = dict(
    repack_n=512,
    proj_m=1024, proj_n=2048,
    attn_q=512,
    rwkv_prep_m=256,
    rwkv_scan_t=512,
    mix_m=256,
    mlp_m=512, mlp_f=1024,
)

SMALL_WD = 0
SMALL_AD = 128
SMALL_GD = 256
SMALL_F = 512
SMALL_WIDTH = 640


def _dot(a, b):
    return jnp.dot(a, b, preferred_element_type=F32)


def _dot_nt(a, b):
    return lax.dot_general(a, b, (((1,), (1,)), ((), ())), preferred_element_type=F32)


def _dot_tn(a, b):
    return lax.dot_general(a, b, (((0,), (0,)), ((), ())), preferred_element_type=F32)


def _bf16_terms(x, parts):
    terms = []
    rem = x
    for p in range(parts):
        h = rem.astype(BF16)
        terms.append(h)
        if p + 1 < parts:
            rem = rem - h.astype(F32)
    return terms


def _split_dot_rhs(m_bf16, x, parts):
    if parts == 1:
        return _dot(m_bf16, x.astype(BF16))
    return _dot(jnp.concatenate([m_bf16] * parts, axis=1), jnp.concatenate(_bf16_terms(x, parts), axis=0))


def _split_dot_lhs(x, m_bf16, parts):
    if parts == 1:
        return _dot(x.astype(BF16), m_bf16)
    return _dot(jnp.concatenate(_bf16_terms(x, parts), axis=1), jnp.concatenate([m_bf16] * parts, axis=0))


def _log_sigmoid(z):
    return jnp.minimum(z, 0.0) - jnp.log(1.0 + jnp.exp(-jnp.abs(z)))


def _sigmoid(z):
    return 1.0 / (1.0 + jnp.exp(-z))


def _params(*sem):
    return pltpu.CompilerParams(dimension_semantics=sem, vmem_limit_bytes=VMEM_LIMIT)


def _repack_kernel(w_ref, tail_ref, o_ref, *, tile_shifts, tail_tile, tn):
    j = pl.program_id(1)
    for shift in sorted(set(tile_shifts)):
        @pl.when(functools.reduce(jnp.logical_or, [j == t for t, s in enumerate(tile_shifts)
                                                   if s == shift and t != tail_tile]))
        def _(shift=shift):
            o_ref[0] = w_ref[0, shift:shift + tn, :].astype(o_ref.dtype)

    @pl.when(j == tail_tile)
    def _():
        shift = tile_shifts[tail_tile]
        o_ref[0] = tail_ref[0, shift:shift + tn, :].astype(o_ref.dtype)


def _repack_w_in(w_t, segments, *, tn):
    layers, n_in, k = w_t.shape
    n_out = sum(width for _, width in segments)
    assert all(width % tn == 0 for _, width in segments)
    win = tn + SUBLANE
    tile_starts = [start + off for start, width in segments for off in range(0, width, tn)]
    tile_shifts = [s % SUBLANE for s in tile_starts]
    tail_tile = len(tile_starts) - 1
    tail_from = tile_starts[tail_tile] - tile_shifts[tail_tile]
    assert all(s - s % SUBLANE + win <= n_in for s in tile_starts[:tail_tile])
    tail = jnp.pad(w_t[:, tail_from:], ((0, 0), (0, tail_from + win - n_in), (0, 0)))
    last_window = (n_in - win) // SUBLANE * SUBLANE

    def window_start(j):
        row = j * tn
        packed = 0
        start = jnp.int32(0)
        for seg_start, width in segments:
            start = jnp.where(row >= packed, seg_start - seg_start % SUBLANE - packed, start)
            packed += width
        return jnp.minimum(row + start, last_window)

    return pl.pallas_call(
        functools.partial(_repack_kernel, tile_shifts=tile_shifts, tail_tile=tail_tile, tn=tn),
        grid=(layers, n_out // tn),
        in_specs=[pl.BlockSpec((pl.Element(1), pl.Element(win), pl.Element(k)),
                               lambda l, j: (l, pl.multiple_of(window_start(j), SUBLANE), 0)),
                  pl.BlockSpec((1, win, k), lambda l, j: (l, 0, 0))],
        out_specs=pl.BlockSpec((1, tn, k), lambda l, j: (l, j, 0)),
        out_shape=jax.ShapeDtypeStruct((layers, n_out, k), BF16),
        compiler_params=_params("parallel", "parallel"),
        name="repack_w_in",
    )(w_t, tail)


def _norm_proj_kernel(x_ref, g_ref, w_ref, o_ref, u_ref, *, sigmoid_from, sub):
    n = pl.program_id(1)

    @pl.when(n == 0)
    def _():
        x = x_ref[...]
        ms = jnp.mean(x * x, axis=-1, keepdims=True)
        u_ref[...] = (x * lax.rsqrt(ms + RMS_EPS) * g_ref[...]).astype(BF16)

    def columns(apply_sigmoid):
        for c0 in range(0, o_ref.shape[1], sub):
            acc = _dot_nt(u_ref[...], w_ref[c0:c0 + sub, :].astype(BF16))
            if apply_sigmoid:
                acc = _sigmoid(acc)
            o_ref[:, c0:c0 + sub] = acc.astype(o_ref.dtype)

    if sigmoid_from is None:
        columns(False)
    else:
        pl.when(n < sigmoid_from)(lambda: columns(False))
        pl.when(n >= sigmoid_from)(lambda: columns(True))


def _norm_proj(x2, gain, w_t, layer, out_dtype, *, tm, tn, sigmoid_from_col=None):
    t, d = x2.shape
    n = w_t.shape[1]
    assert t % tm == 0 and n % tn == 0
    sig = None if sigmoid_from_col is None else sigmoid_from_col // tn
    if sigmoid_from_col is not None:
        assert sigmoid_from_col % tn == 0
    return pl.pallas_call(
        functools.partial(_norm_proj_kernel, sigmoid_from=sig,
                          sub=PROJ_SUB_TILE if tn % PROJ_SUB_TILE == 0 else tn),
        grid=(t // tm, n // tn),
        in_specs=[
            pl.BlockSpec((tm, d), lambda m, j: (m, 0)),
            pl.BlockSpec((1, d), lambda m, j: (0, 0)),
            pl.BlockSpec((None, tn, d), lambda m, j: (layer, j, 0)),
        ],
        out_specs=pl.BlockSpec((tm, tn), lambda m, j: (m, j)),
        out_shape=jax.ShapeDtypeStruct((t, n), out_dtype),
        scratch_shapes=[pltpu.VMEM((tm, d), BF16)],
        compiler_params=_params("parallel", "arbitrary"),
        name="norm_proj",
    )(x2, gain.reshape(1, d), w_t)


def _forget_cumsum_kernel(f_ref, b_ref, c_ref, *, cb):
    s = f_ref.shape[1]
    row = lax.broadcasted_iota(jnp.int32, (cb, cb), 0)
    col = lax.broadcasted_iota(jnp.int32, (cb, cb), 1)
    tri = jnp.where(col <= row, 1.0, 0.0).astype(BF16)
    carry = jnp.zeros((1, LANE), F32)
    for j in range(s // cb):
        lf = _log_sigmoid(f_ref[0, j * cb:(j + 1) * cb, :] + b_ref[...])
        c = _split_dot_rhs(tri, lf, 3) + carry
        c_ref[0, j * cb:(j + 1) * cb, :] = c
        carry = c[cb - 1:cb, :]


def _forget_cumsum(small3, b_forget_pad):
    b, s, _ = small3.shape
    cb = min(256, s)
    return pl.pallas_call(
        functools.partial(_forget_cumsum_kernel, cb=cb),
        grid=(b,),
        in_specs=[
            pl.BlockSpec((1, s, LANE), lambda i: (i, 0, SMALL_F // LANE)),
            pl.BlockSpec((1, LANE), lambda i: (0, 0)),
        ],
        out_specs=pl.BlockSpec((1, s, LANE), lambda i: (i, 0, 0)),
        out_shape=jax.ShapeDtypeStruct((b, s, LANE), F32),
        compiler_params=_params("parallel"),
        name="forget_cumsum",
    )(small3, b_forget_pad)


def _sb_kernel(q_ref, k_ref, v_ref, o_ref, *, tq, scale):
    i = pl.program_id(2)
    sub = min(SUFFIX_BLOCK, tq)
    nsub = tq // sub
    q = (q_ref[0].astype(F32) * (scale * LOG2E)).astype(BF16)
    srow = lax.broadcasted_iota(jnp.int32, (sub, sub), 0)
    scol = lax.broadcasted_iota(jnp.int32, (sub, sub), 1)
    upper = jnp.where(srow > scol, 1.0, 0.0).astype(BF16)

    def block(j, carry, acc, diagonal):
        start = pl.multiple_of(j * tq, tq)
        kb = k_ref[0, pl.ds(start, tq), :]
        vb = v_ref[0, pl.ds(start, tq), :]
        w = _dot_nt(q, kb)
        log_beta = jnp.minimum(w, 0.0) - jnp.log(1.0 + jnp.exp2(-jnp.abs(w))) * LOG2E
        log_rest = log_beta - w
        if diagonal:
            strict = (lax.broadcasted_iota(jnp.int32, (tq, tq), 1)
                      < lax.broadcasted_iota(jnp.int32, (tq, tq), 0))
            log_rest = jnp.where(strict, log_rest, 0.0)
        pieces = [None] * nsub
        for sb in reversed(range(nsub)):
            x = log_rest[:, sb * sub:(sb + 1) * sub]
            pieces[sb] = _split_dot_lhs(x, upper, 2) + carry
            carry = carry + jnp.sum(x, axis=1, keepdims=True)
        between = pieces[0] if nsub == 1 else jnp.concatenate(pieces, axis=1)
        a = jnp.exp2(log_beta + between)
        if diagonal:
            a = jnp.where(strict, a, 0.0)
        acc = acc + _dot(a.astype(BF16), vb)
        return carry, acc

    carry0 = jnp.zeros((tq, 1), F32)
    acc0 = jnp.zeros((tq, ATTN_HEAD_DIM), F32)
    carry, acc = block(i, carry0, acc0, True)
    odd = lax.rem(i, 2)
    carry, acc = lax.cond(odd == 1, lambda c, a: block(i - 1, c, a, False), lambda c, a: (c, a), carry, acc)

    def body(jj, state):
        j = i - 1 - odd - 2 * jj
        c1, a1 = block(j, state[0], state[1], False)
        return block(j - 1, c1, a1, False)

    carry, acc = lax.fori_loop(0, i // 2, body, (carry, acc))
    o_ref[0] = acc.astype(o_ref.dtype)


def _sb_attention(proj3, *, q_col, k_col, v_col, heads, tq):
    b, s, _ = proj3.shape
    dh = ATTN_HEAD_DIM
    return pl.pallas_call(
        functools.partial(_sb_kernel, tq=tq, scale=dh ** -0.5),
        grid=(b, heads, s // tq),
        in_specs=[
            pl.BlockSpec((1, tq, dh), lambda bi, h, i: (bi, i, q_col + h)),
            pl.BlockSpec((1, s, dh), lambda bi, h, i: (bi, 0, k_col + h)),
            pl.BlockSpec((1, s, dh), lambda bi, h, i: (bi, 0, v_col + h)),
        ],
        out_specs=pl.BlockSpec((1, tq, dh), lambda bi, h, i: (bi, i, h)),
        out_shape=jax.ShapeDtypeStruct((b, s, heads * dh), BF16),
        compiler_params=_params("parallel", "parallel", "arbitrary"),
        name="sb_attention",
    )(proj3, proj3, proj3)


def _fox_kernel(q_ref, k_ref, v_ref, cum_ref, ck_ref, o_ref, *, tq, scale):
    h = pl.program_id(1)
    i = pl.program_id(2)
    q = (q_ref[0].astype(F32) * (scale * LOG2E)).astype(BF16)
    lane = lax.broadcasted_iota(jnp.int32, (tq, LANE), 1)
    cq = jnp.sum(jnp.where(lane == h, cum_ref[0], 0.0), axis=1, keepdims=True) * LOG2E

    def scores(j):
        start = pl.multiple_of(j * tq, tq)
        kb = k_ref[0, pl.ds(start, tq), :]
        ck = ck_ref[0, 0, pl.ds(j, 1), :] * LOG2E
        return _dot_nt(q, kb) + (cq - ck), v_ref[0, pl.ds(start, tq), :]

    z, vb = scores(i)
    causal = (lax.broadcasted_iota(jnp.int32, (tq, tq), 1)
              <= lax.broadcasted_iota(jnp.int32, (tq, tq), 0))
    z = jnp.where(causal, z, MASK_VALUE)
    m = jnp.max(z, axis=1, keepdims=True)
    p = jnp.exp2(z - m)
    l = jnp.sum(p, axis=1, keepdims=True)
    acc = _dot(p.astype(BF16), vb)

    def step(j, m, l, acc):
        z, vb = scores(j)
        m_new = jnp.maximum(m, jnp.max(z, axis=1, keepdims=True))
        alpha = jnp.exp2(m - m_new)
        p = jnp.exp2(z - m_new)
        l = alpha * l + jnp.sum(p, axis=1, keepdims=True)
        acc = alpha * acc + _dot(p.astype(BF16), vb)
        return m_new, l, acc

    odd = lax.rem(i, 2)
    m, l, acc = lax.cond(odd == 1, lambda m, l, a: step(i - 1, m, l, a), lambda m, l, a: (m, l, a), m, l, acc)

    def body(jj, state):
        m, l, acc = state
        j = i - 1 - odd - 2 * jj
        z1, v1 = scores(j)
        z2, v2 = scores(j - 1)
        m_new = jnp.maximum(m, jnp.maximum(jnp.max(z1, axis=1, keepdims=True), jnp.max(z2, axis=1, keepdims=True)))
        alpha = jnp.exp2(m - m_new)
        p1 = jnp.exp2(z1 - m_new)
        p2 = jnp.exp2(z2 - m_new)
        l = alpha * l + (jnp.sum(p1, axis=1, keepdims=True) + jnp.sum(p2, axis=1, keepdims=True))
        acc = alpha * acc + (_dot(p1.astype(BF16), v1) + _dot(p2.astype(BF16), v2))
        return m_new, l, acc

    m, l, acc = lax.fori_loop(0, i // 2, body, (m, l, acc))
    o_ref[0] = (acc / l).astype(o_ref.dtype)


def _fox_attention(proj3, cum, ck, *, q_col, k_col, v_col, heads, tq):
    b, s, _ = proj3.shape
    dh = ATTN_HEAD_DIM
    nb = s // tq
    return pl.pallas_call(
        functools.partial(_fox_kernel, tq=tq, scale=dh ** -0.5),
        grid=(b, heads, nb),
        in_specs=[
            pl.BlockSpec((1, tq, dh), lambda bi, h, i: (bi, i, q_col + h)),
            pl.BlockSpec((1, s, dh), lambda bi, h, i: (bi, 0, k_col + h)),
            pl.BlockSpec((1, s, dh), lambda bi, h, i: (bi, 0, v_col + h)),
            pl.BlockSpec((1, tq, LANE), lambda bi, h, i: (bi, i, 0)),
            pl.BlockSpec((1, 1, nb, tq), lambda bi, h, i: (bi, h, 0, 0)),
        ],
        out_specs=pl.BlockSpec((1, tq, dh), lambda bi, h, i: (bi, i, h)),
        out_shape=jax.ShapeDtypeStruct((b, s, heads * dh), BF16),
        compiler_params=_params("parallel", "parallel", "arbitrary"),
        name="fox_attention",
    )(proj3, proj3, proj3, cum, ck)


def _same_head_mask(n):
    row = lax.broadcasted_iota(jnp.int32, (n, n), 0)
    col = lax.broadcasted_iota(jnp.int32, (n, n), 1)
    shift = RWKV_HEAD_DIM.bit_length() - 1
    return jnp.right_shift(row, shift) == jnp.right_shift(col, shift)


def _rwkv_prep_kernel(z_ref, zs_ref, mu_ref, mus_ref, w0_ref, wup_ref, a0_ref, aup_ref, gup_ref,
                      kk_ref, ka_ref,
                      r_out, lw_out, k_out, v_out, kn_out, b_out, g_out,
                      zc_ref, zsc_ref, *, tm, width):
    si = pl.program_id(1)

    @pl.when(si == 0)
    def _():
        zc_ref[...] = jnp.zeros_like(zc_ref)
        zsc_ref[...] = jnp.zeros_like(zsc_ref)

    z = z_ref[0].astype(F32)
    zs = zs_ref[0]
    first = lax.broadcasted_iota(jnp.int32, (tm, 1), 0) == 0

    def shifted(x, carry_ref):
        prev = jnp.where(first, carry_ref[...], pltpu.roll(x, 1, 0))
        carry_ref[...] = x[tm - 1:tm, :]
        return prev

    z = z + (shifted(z, zc_ref) - z) * mu_ref[...]
    zs = zs + (shifted(zs, zsc_ref) - zs) * mus_ref[...]

    r = z[:, :width]
    k = z[:, width:2 * width]
    v = z[:, 2 * width:]
    wd = zs[:, SMALL_WD:SMALL_WD + LANE]
    ad = zs[:, SMALL_AD:SMALL_AD + LANE]
    gd = zs[:, SMALL_GD:SMALL_GD + 2 * LANE]

    wl = w0_ref[...] + _dot(jnp.tanh(wd).astype(BF16), wup_ref[...])
    lw_out[0] = _sigmoid(wl) * (-DECAY_SCALE)
    a = _sigmoid(a0_ref[...] + _dot(ad.astype(BF16), aup_ref[...]))
    g_out[0] = _dot(_sigmoid(gd).astype(BF16), gup_ref[...]).astype(g_out.dtype)

    r_out[0] = r.astype(r_out.dtype)
    v_out[0] = v.astype(v_out.dtype)
    k_out[0] = (k * (1.0 + (a - 1.0) * ka_ref[...])).astype(k_out.dtype)

    kk_raw = k * kk_ref[...]
    ones_bd = jnp.where(_same_head_mask(RWKV_GROUP), 1.0, 0.0).astype(BF16)
    for gi in range(width // RWKV_GROUP):
        sl = slice(gi * RWKV_GROUP, (gi + 1) * RWKV_GROUP)
        x = kk_raw[:, sl]
        ss = _split_dot_lhs(x * x, ones_bd, 1)
        kn = x * lax.rsqrt(jnp.maximum(ss, KEY_NORM_FLOOR ** 2))
        kn_out[0, :, sl] = kn.astype(kn_out.dtype)
        b_out[0, :, sl] = (kn * a[:, sl]).astype(b_out.dtype)


def _rwkv_prep(proj3, small3, mu_rkv, mu_small, w0, w_up_p, a0, a_up_p, g_up_p, k_k, k_a, *, rkv_col, width, tm):
    b, s, _ = proj3.shape
    full = lambda shape: pl.BlockSpec(shape, lambda bi, si: (0,) * len(shape))
    tile = pl.BlockSpec((1, tm, width), lambda bi, si: (bi, si, 0))
    shp = lambda dt: jax.ShapeDtypeStruct((b, s, width), dt)
    return pl.pallas_call(
        functools.partial(_rwkv_prep_kernel, tm=tm, width=width),
        grid=(b, s // tm),
        in_specs=[
            pl.BlockSpec((1, tm, 3 * width), lambda bi, si: (bi, si, rkv_col)),
            pl.BlockSpec((1, tm, SMALL_WIDTH), lambda bi, si: (bi, si, 0)),
            full((1, 3 * width)), full((1, SMALL_WIDTH)),
            full((1, width)), full((LANE, width)),
            full((1, width)), full((LANE, width)),
            full((2 * LANE, width)),
            full((1, width)), full((1, width)),
        ],
        out_specs=[tile] * 7,
        out_shape=[shp(BF16), shp(F32), shp(BF16), shp(BF16), shp(BF16), shp(BF16), shp(BF16)],
        scratch_shapes=[pltpu.VMEM((1, 3 * width), F32), pltpu.VMEM((1, SMALL_WIDTH), F32)],
        compiler_params=_params("parallel", "arbitrary"),
        name="rwkv_prep",
    )(proj3, small3, mu_rkv, mu_small, w0, w_up_p, a0, a_up_p, g_up_p, k_k, k_a)


def _rwkv_scan_kernel(r_ref, lw_ref, k_ref, v_ref, kn_ref, b_ref, g_ref, rk_ref, lnw_ref, lnb_ref,
                      o_ref, s_ref, *, tt):
    st = pl.program_id(2)

    @pl.when(st == 0)
    def _():
        s_ref[...] = jnp.zeros_like(s_ref)

    c = RWKV_CHUNK
    w = RWKV_GROUP
    heads = w // RWKV_HEAD_DIM
    assert heads * c == w
    same = _same_head_mask(w)
    row = lax.broadcasted_iota(jnp.int32, (w, w), 0)
    col = lax.broadcasted_iota(jnp.int32, (w, w), 1)
    strict = same & (col < row)
    incl = same & (col <= row)

    def tile(x):
        return jnp.concatenate([x] * heads, axis=0)

    def stack(x):
        return jnp.where(same, tile(x), 0.0).astype(BF16)

    def unstack(x):
        out = x[0:c]
        for h in range(1, heads):
            out = out + x[h * c:(h + 1) * c]
        return out

    nchunks = tt // c

    assert tt % w == 0
    tri = jnp.where(incl, 1.0, 0.0).astype(BF16)
    cum_all = jnp.concatenate([_split_dot_rhs(tri, lw_ref[0, r0:r0 + w, :], 2) for r0 in range(0, tt, w)],
                              axis=0)

    def operands(ci):
        sl = slice(ci * c, (ci + 1) * c)
        lw = lw_ref[0, sl, :]
        r = r_ref[0, sl, :].astype(F32)
        k = k_ref[0, sl, :].astype(F32)
        v = v_ref[0, sl, :].astype(F32)
        kn = kn_ref[0, sl, :].astype(F32)
        bb = b_ref[0, sl, :].astype(F32)
        cum = cum_all[sl]
        last = cum[c - 1:c, :]
        e_neg = jnp.exp(-cum)
        e_end = jnp.exp(last - cum)
        return dict(
            r_s=stack(r * jnp.exp(cum)), a_s=stack(kn * jnp.exp(cum - lw)), v_s=stack(v),
            bk=jnp.concatenate([(bb * e_neg).astype(BF16), (k * e_neg).astype(BF16)], axis=0),
            b_end=stack(bb * e_end), k_end=stack(k * e_end), w_end=jnp.exp(last))

    first_half = lax.broadcasted_iota(jnp.int32, (w, 2 * c), 1) < c

    def block_diag(x, swapped, use_first, mask):
        half = jnp.where(first_half, x, swapped) if use_first else jnp.where(first_half, swapped, x)
        return jnp.where(mask, jnp.concatenate([half] * (w // (2 * c)), axis=1), 0.0)

    def interactions(ch):
        prod = _dot_nt(jnp.concatenate([ch["a_s"], ch["r_s"]], axis=0), ch["bk"])
        pa, pr = prod[:w], prod[w:]
        sa, sr = pltpu.roll(pa, c, 1), pltpu.roll(pr, c, 1)
        ch["pw"] = -block_diag(pa, sa, True, strict)
        ch["nrm"] = ch["pw"]
        ch["a_ak"] = block_diag(pa, sa, False, strict).astype(BF16)
        ch["a_rb"] = block_diag(pr, sr, True, incl).astype(BF16)
        ch["a_rk"] = block_diag(pr, sr, False, incl).astype(BF16)
        return ch

    def neumann_level(group):
        for ch in group:
            pwb = ch["pw"].astype(BF16)
            ch["pw"] = _dot(pwb, pwb)
        for ch in group:
            ch["nrm"] = ch["nrm"] + ch["pw"] + _dot(ch["nrm"].astype(BF16), ch["pw"].astype(BF16))

    chunks = [operands(ci) for ci in range(nchunks)]
    for ch in chunks:
        interactions(ch)
    for _ in range(c.bit_length() - 2):
        neumann_level(chunks)

    def stage_a(ch):
        ch["nrm"] = ch["nrm"].astype(BF16)
        ch["a_hat"] = (ch["a_s"].astype(F32) + _dot(ch["nrm"], ch["a_s"])).astype(BF16)
        ch["akv"] = _dot(ch["a_ak"], ch["v_s"])
        ch["q_s"] = _dot(ch["a_rk"], ch["v_s"])
        ch["kv"] = _dot_tn(ch["v_s"], ch["k_end"])

    def stage_b(ch):
        ch["v_hat"] = ch["akv"] + _dot(ch["nrm"], ch["akv"].astype(BF16))
        ch["x_mat"] = _dot_tn(ch["a_hat"], ch["b_end"]).astype(BF16)

    def stage_c(ch, state):
        g_mat = ch["kv"] - _dot_tn(ch["v_hat"].astype(BF16), ch["b_end"])
        ch["s0"] = state.astype(BF16)
        return ch["w_end"] * state - _dot(ch["s0"], ch["x_mat"]) + g_mat

    def stage_out(ch):
        u_b = (-(_dot_nt(ch["a_hat"], ch["s0"]) + ch["v_hat"])).astype(BF16)
        return unstack(_dot_nt(ch["r_s"], ch["s0"]) + _dot(ch["a_rb"], u_b) + ch["q_s"])

    state = s_ref[...]
    ys = []
    for step in range(nchunks + 3):
        if step < nchunks:
            stage_a(chunks[step])
        if 0 <= step - 1 < nchunks:
            stage_b(chunks[step - 1])
        if 0 <= step - 2 < nchunks:
            state = stage_c(chunks[step - 2], state)
        if 0 <= step - 3 < nchunks:
            ys.append(stage_out(chunks[step - 3]))
    s_ref[...] = state
    y = jnp.concatenate(ys, axis=0)

    mean_bd = jnp.where(same, 1.0 / RWKV_HEAD_DIM, 0.0).astype(BF16)
    ones_bd = jnp.where(same, 1.0, 0.0).astype(BF16)
    mean = _split_dot_lhs(y, mean_bd, 2)
    d = y - mean
    var = _split_dot_lhs(d * d, mean_bd, 1)
    y = d * lax.rsqrt(var + RWKV_GN_EPS) * lnw_ref[...] + lnb_ref[...]
    r = r_ref[0].astype(F32)
    k = k_ref[0].astype(F32)
    v = v_ref[0].astype(F32)
    bonus = _split_dot_lhs(r * k * rk_ref[...], ones_bd, 1)
    y = y + bonus * v
    o_ref[0] = (y * g_ref[0].astype(F32)).astype(o_ref.dtype)


def _rwkv_scan(r, lw, k, v, kn, bb, g, r_k, ln_w, ln_b, *, tt):
    b, s, width = r.shape
    w = RWKV_GROUP
    tile = pl.BlockSpec((1, tt, w), lambda bi, gi, si: (bi, si, gi))
    vec = pl.BlockSpec((1, w), lambda bi, gi, si: (0, gi))
    return pl.pallas_call(
        functools.partial(_rwkv_scan_kernel, tt=tt),
        grid=(b, width // w, s // tt),
        in_specs=[tile] * 7 + [vec] * 3,
        out_specs=tile,
        out_shape=jax.ShapeDtypeStruct((b, s, width), BF16),
        scratch_shapes=[pltpu.VMEM((w, w), F32)],
        compiler_params=_params("parallel", "parallel", "arbitrary"),
        name="rwkv_scan",
    )(r, lw, k, v, kn, bb, g, r_k, ln_w, ln_b)


def _mix_out_kernel(ya_ref, yb_ref, yc_ref, gate_ref, x_ref, pa_ref, pb_ref, pc_ref, wo_ref, gain_ref, o_ref):
    d = x_ref.shape[1]
    m = gate_ref[:, 0:d].astype(F32) * _dot(ya_ref[...], pa_ref[...])
    m = m + gate_ref[:, d:2 * d].astype(F32) * _dot(yb_ref[...], pb_ref[...])
    m = m + gate_ref[:, 2 * d:3 * d].astype(F32) * _dot(yc_ref[...], pc_ref[...])
    o = _dot(m.astype(BF16), wo_ref[...])
    ms = jnp.mean(o * o, axis=-1, keepdims=True)
    o_ref[...] = x_ref[...] + o * lax.rsqrt(ms + RMS_EPS) * gain_ref[...]


def _mix_out(ya, yb, yc, proj2, x2, pa, pb, pc, wo, gain, *, gate_col_block, tm):
    t, d = x2.shape
    const = lambda shape: pl.BlockSpec(shape, lambda m: (0, 0), pipeline_mode=pl.Buffered(1))
    return pl.pallas_call(
        _mix_out_kernel,
        grid=(t // tm,),
        in_specs=[
            pl.BlockSpec((tm, ya.shape[1]), lambda m: (m, 0)),
            pl.BlockSpec((tm, yb.shape[1]), lambda m: (m, 0)),
            pl.BlockSpec((tm, yc.shape[1]), lambda m: (m, 0)),
            pl.BlockSpec((tm, 3 * d), lambda m: (m, gate_col_block)),
            pl.BlockSpec((tm, d), lambda m: (m, 0)),
            const(pa.shape), const(pb.shape), const(pc.shape), const(wo.shape),
            const((1, d)),
        ],
        out_specs=pl.BlockSpec((tm, d), lambda m: (m, 0)),
        out_shape=jax.ShapeDtypeStruct((t, d), F32),
        compiler_params=_params("parallel"),
        name="mix_out",
    )(ya, yb, yc, proj2, x2, pa, pb, pc, wo, gain.reshape(1, d))


def _mlp_kernel(x_ref, g1_ref, wu_ref, wd_ref, g2_ref, o_ref, u_ref, acc_ref):
    f = pl.program_id(1)

    @pl.when(f == 0)
    def _():
        x = x_ref[...]
        ms = jnp.mean(x * x, axis=-1, keepdims=True)
        u_ref[...] = (x * lax.rsqrt(ms + RMS_EPS) * g1_ref[...]).astype(BF16)
        acc_ref[...] = jnp.zeros_like(acc_ref)

    h = jnp.maximum(_dot(u_ref[...], wu_ref[...]), 0.0)
    acc_ref[...] += _dot((h * h).astype(BF16), wd_ref[...])

    @pl.when(f == pl.num_programs(1) - 1)
    def _():
        o = acc_ref[...]
        ms = jnp.mean(o * o, axis=-1, keepdims=True)
        o_ref[...] = x_ref[...] + o * lax.rsqrt(ms + RMS_EPS) * g2_ref[...]


def _mlp(x2, g1, wu, wd, g2, *, tm, tf):
    t, d = x2.shape
    ff = wu.shape[1]
    return pl.pallas_call(
        _mlp_kernel,
        grid=(t // tm, ff // tf),
        in_specs=[
            pl.BlockSpec((tm, d), lambda m, f: (m, 0)),
            pl.BlockSpec((1, d), lambda m, f: (0, 0)),
            pl.BlockSpec((d, tf), lambda m, f: (0, f)),
            pl.BlockSpec((tf, d), lambda m, f: (f, 0)),
            pl.BlockSpec((1, d), lambda m, f: (0, 0)),
        ],
        out_specs=pl.BlockSpec((tm, d), lambda m, f: (m, 0)),
        out_shape=jax.ShapeDtypeStruct((t, d), F32),
        scratch_shapes=[pltpu.VMEM((tm, d), BF16), pltpu.VMEM((tm, d), F32)],
        compiler_params=_params("parallel", "arbitrary"),
        name="mlp",
    )(x2, g1.reshape(1, d), wu, wd, g2.reshape(1, d))


def _pad_rows(w, rows):
    return jnp.pad(w, ((0, rows - w.shape[0]), (0, 0)))


def _pad_cols(w, cols):
    return jnp.pad(w, ((0, 0), (0, cols - w.shape[1])))


def _prepare_w_in(w_in, d, sbw, fxw, rww):
    fx_heads = fxw // ATTN_HEAD_DIM
    o_f = 3 * sbw + 3 * fxw
    o_rw = o_f + fx_heads
    o_wd = o_rw + 3 * rww
    o_ad = o_wd + DECAY_LORA
    o_gd = o_ad + AAA_LORA
    o_gate = o_gd + GATE_LORA
    assert w_in.shape[2] == o_gate + 3 * d
    w_t = jnp.transpose(w_in, (0, 2, 1))
    main = _repack_w_in(w_t, [(0, o_f), (o_rw, 3 * rww), (o_gate, 3 * d)], tn=TILE["repack_n"])

    def rows(lo, hi, padded):
        return jnp.pad(w_t[:, lo:hi], ((0, 0), (0, padded - (hi - lo)), (0, 0)))

    small = jnp.concatenate([rows(o_wd, o_ad, LANE), rows(o_ad, o_gd, LANE), rows(o_gd, o_gate, 2 * LANE),
                             rows(o_f, o_rw, LANE)], axis=1)
    return main, small


def _mixer_layer(x, layer, norm_pre, norm_post, w_main_t, w_small_t, b_forget, mu, w0, w_up, a0, a_up, g_up,
                 k_k, k_a, r_k, ln_w, ln_b, pa, pb, pc, w_out):
    bsz, s, d = x.shape
    t = bsz * s
    sbw = pa.shape[0]
    fxw = pb.shape[0]
    rww = pc.shape[0]
    sb_heads = sbw // ATTN_HEAD_DIM
    fx_heads = fxw // ATTN_HEAD_DIM

    gate_col = 3 * sbw + 3 * fxw + 3 * rww
    mu_rkv = mu[None, :3 * rww]
    mu_small = jnp.concatenate([
        _pad_cols(mu[None, 3 * rww:3 * rww + DECAY_LORA], LANE),
        _pad_cols(mu[None, 3 * rww + DECAY_LORA:3 * rww + DECAY_LORA + AAA_LORA], LANE),
        _pad_cols(mu[None, 3 * rww + DECAY_LORA + AAA_LORA:], 2 * LANE),
        jnp.zeros((1, LANE), F32)], axis=1)

    x2 = x.reshape(t, d)
    proj_m = min(TILE["proj_m"], t)
    proj = _norm_proj(x2, norm_pre, w_main_t, layer, BF16, tm=proj_m, tn=TILE["proj_n"],
                      sigmoid_from_col=gate_col)
    small = _norm_proj(x2, norm_pre, w_small_t, layer, F32, tm=proj_m, tn=SMALL_WIDTH)
    proj3 = proj.reshape(bsz, s, -1)
    small3 = small.reshape(bsz, s, SMALL_WIDTH)

    tq = min(TILE["attn_q"], s)
    qa = 0
    ya = _sb_attention(proj3, q_col=qa, k_col=qa + sb_heads, v_col=qa + 2 * sb_heads, heads=sb_heads, tq=tq)

    cum = _forget_cumsum(small3, _pad_cols(b_forget[None, :], LANE))
    c_hs = cum[:, :, :fx_heads].transpose(0, 2, 1)
    qb = 3 * sb_heads
    yb = _fox_attention(proj3, cum, c_hs.reshape(bsz, fx_heads, s // tq, tq),
                        q_col=qb, k_col=qb + fx_heads, v_col=qb + 2 * fx_heads, heads=fx_heads, tq=tq)

    rkv_col = (3 * sbw + 3 * fxw) // (3 * rww)
    assert rkv_col * 3 * rww == 3 * sbw + 3 * fxw
    row = lambda p: p.reshape(1, rww)
    r, lw, k, v, kn, bb, g = _rwkv_prep(
        proj3, small3, mu_rkv, mu_small, row(w0), _pad_rows(w_up, LANE).astype(BF16), row(a0),
        _pad_rows(a_up, LANE).astype(BF16), _pad_rows(g_up, 2 * LANE).astype(BF16), row(k_k), row(k_a),
        rkv_col=rkv_col, width=rww, tm=min(TILE["rwkv_prep_m"], s))
    yc = _rwkv_scan(r, lw, k, v, kn, bb, g, row(r_k), row(ln_w), row(ln_b), tt=min(TILE["rwkv_scan_t"], s))

    assert gate_col % (3 * d) == 0
    out = _mix_out(ya.reshape(t, sbw), yb.reshape(t, fxw), yc.reshape(t, rww), proj, x2,
                   pa.astype(BF16), pb.astype(BF16), pc.astype(BF16), w_out.astype(BF16), norm_post,
                   gate_col_block=gate_col // (3 * d), tm=min(TILE["mix_m"], t))
    return out.reshape(bsz, s, d)


def kernel(x, norm_mix_pre, norm_mix_post, norm_mlp_pre, norm_mlp_post, w_in, b_forget, rwkv_mu, rwkv_w0,
           rwkv_w_up, rwkv_a0, rwkv_a_up, rwkv_g_up, rwkv_k_k, rwkv_k_a, rwkv_r_k, rwkv_ln_w, rwkv_ln_b,
           w_branch_a, w_branch_b, w_branch_c, w_out, w_mlp_up, w_mlp_down):
    bsz, s, d = x.shape
    t = bsz * s
    w_main_t, w_small_t = _prepare_w_in(w_in, d, w_branch_a.shape[1], w_branch_b.shape[1], w_branch_c.shape[1])
    for l in range(w_in.shape[0]):
        x = _mixer_layer(x, l, norm_mix_pre[l], norm_mix_post[l], w_main_t, w_small_t, b_forget[l], rwkv_mu[l],
                         rwkv_w0[l], rwkv_w_up[l], rwkv_a0[l], rwkv_a_up[l], rwkv_g_up[l], rwkv_k_k[l],
                         rwkv_k_a[l], rwkv_r_k[l], rwkv_ln_w[l], rwkv_ln_b[l], w_branch_a[l], w_branch_b[l],
                         w_branch_c[l], w_out[l])
        x = _mlp(x.reshape(t, d), norm_mlp_pre[l], w_mlp_up[l].astype(BF16), w_mlp_down[l].astype(BF16),
                 norm_mlp_post[l], tm=min(TILE["mlp_m"], t), tf=TILE["mlp_f"]).reshape(bsz, s, d)
    return x
```

```python
import functools

import jax
import jax.numpy as jnp
from jax import lax
from jax.experimental import pallas as pl
from jax.experimental.pallas import tpu as pltpu

F32 = jnp.float32
BF16 = jnp.bfloat16

RMS_EPS = 1e-6
RWKV_GN_EPS = 64e-5
DECAY_SCALE = 0.6065306597126334
KEY_NORM_FLOOR = 1e-12
ATTN_HEAD_DIM = 128
SB_HEADS_PER_STEP = 2
FOX_HEADS_PER_STEP = 1
RWKV_HEAD_DIM = 64
RWKV_CHUNK = 64
RWKV_GROUP = 256
DECAY_LORA = 64
AAA_LORA = 64
GATE_LORA = 160
LANE = 128
SUBLANE = 8
MASK_VALUE = -1e30
LOG2E = 1.4426950408889634
SUFFIX_BLOCK = 256
PROJ_SUB_TILE = 512
VMEM_LIMIT = 56 * 1024 * 1024

TILE = dict(
    repack_n=512,
    proj_m=1024, proj_n=2048,
    attn_q=512,
    rwkv_prep_m=256,
    rwkv_scan_t=512,
    mix_m=256,
    mlp_m=512, mlp_f=1024,
)

SMALL_WD = 0
SMALL_AD = 128
SMALL_GD = 256
SMALL_F = 512
SMALL_WIDTH = 640


def _dot(a, b):
    return jnp.dot(a, b, preferred_element_type=F32)


def _dot_nt(a, b):
    return lax.dot_general(a, b, (((1,), (1,)), ((), ())), preferred_element_type=F32)


def _dot_tn(a, b):
    return lax.dot_general(a, b, (((0,), (0,)), ((), ())), preferred_element_type=F32)


def _bf16_terms(x, parts):
    terms = []
    rem = x
    for p in range(parts):
        h = rem.astype(BF16)
        terms.append(h)
        if p + 1 < parts:
            rem = rem - h.astype(F32)
    return terms


def _split_dot_rhs(m_bf16, x, parts):
    if parts == 1:
        return _dot(m_bf16, x.astype(BF16))
    return _dot(jnp.concatenate([m_bf16] * parts, axis=1), jnp.concatenate(_bf16_terms(x, parts), axis=0))


def _split_dot_lhs(x, m_bf16, parts):
    if parts == 1:
        return _dot(x.astype(BF16), m_bf16)
    return _dot(jnp.concatenate(_bf16_terms(x, parts), axis=1), jnp.concatenate([m_bf16] * parts, axis=0))


def _log_sigmoid(z):
    return jnp.minimum(z, 0.0) - jnp.log(1.0 + jnp.exp(-jnp.abs(z)))


def _sigmoid(z):
    return 1.0 / (1.0 + jnp.exp(-z))


def _params(*sem):
    return pltpu.CompilerParams(dimension_semantics=sem, vmem_limit_bytes=VMEM_LIMIT)


def _repack_kernel(w_ref, tail_ref, o_ref, *, tile_shifts, tail_tile, tn):
    j = pl.program_id(1)
    for shift in sorted(set(tile_shifts)):
        @pl.when(functools.reduce(jnp.logical_or, [j == t for t, s in enumerate(tile_shifts)
                                                   if s == shift and t != tail_tile]))
        def _(shift=shift):
            o_ref[0] = w_ref[0, shift:shift + tn, :].astype(o_ref.dtype)

    @pl.when(j == tail_tile)
    def _():
        shift = tile_shifts[tail_tile]
        o_ref[0] = tail_ref[0, shift:shift + tn, :].astype(o_ref.dtype)


def _repack_w_in(w_t, segments, *, tn):
    layers, n_in, k = w_t.shape
    n_out = sum(width for _, width in segments)
    assert all(width % tn == 0 for _, width in segments)
    win = tn + SUBLANE
    tile_starts = [start + off for start, width in segments for off in range(0, width, tn)]
    tile_shifts = [s % SUBLANE for s in tile_starts]
    tail_tile = len(tile_starts) - 1
    tail_from = tile_starts[tail_tile] - tile_shifts[tail_tile]
    assert all(s - s % SUBLANE + win <= n_in for s in tile_starts[:tail_tile])
    tail = jnp.pad(w_t[:, tail_from:], ((0, 0), (0, tail_from + win - n_in), (0, 0)))
    last_window = (n_in - win) // SUBLANE * SUBLANE

    def window_start(j):
        row = j * tn
        packed = 0
        start = jnp.int32(0)
        for seg_start, width in segments:
            start = jnp.where(row >= packed, seg_start - seg_start % SUBLANE - packed, start)
            packed += width
        return jnp.minimum(row + start, last_window)

    return pl.pallas_call(
        functools.partial(_repack_kernel, tile_shifts=tile_shifts, tail_tile=tail_tile, tn=tn),
        grid=(layers, n_out // tn),
        in_specs=[pl.BlockSpec((pl.Element(1), pl.Element(win), pl.Element(k)),
                               lambda l, j: (l, pl.multiple_of(window_start(j), SUBLANE), 0)),
                  pl.BlockSpec((1, win, k), lambda l, j: (l, 0, 0))],
        out_specs=pl.BlockSpec((1, tn, k), lambda l, j: (l, j, 0)),
        out_shape=jax.ShapeDtypeStruct((layers, n_out, k), BF16),
        compiler_params=_params("parallel", "parallel"),
        name="repack_w_in",
    )(w_t, tail)


def _norm_proj_kernel(x_ref, g_ref, w_ref, o_ref, u_ref, *, sigmoid_from, sub):
    n = pl.program_id(1)

    @pl.when(n == 0)
    def _():
        x = x_ref[...]
        ms = jnp.mean(x * x, axis=-1, keepdims=True)
        u_ref[...] = (x * lax.rsqrt(ms + RMS_EPS) * g_ref[...]).astype(BF16)

    def columns(apply_sigmoid):
        for c0 in range(0, o_ref.shape[1], sub):
            acc = _dot_nt(u_ref[...], w_ref[c0:c0 + sub, :].astype(BF16))
            if apply_sigmoid:
                acc = _sigmoid(acc)
            o_ref[:, c0:c0 + sub] = acc.astype(o_ref.dtype)

    if sigmoid_from is None:
        columns(False)
    else:
        pl.when(n < sigmoid_from)(lambda: columns(False))
        pl.when(n >= sigmoid_from)(lambda: columns(True))


def _norm_proj(x2, gain, w_t, layer, out_dtype, *, tm, tn, sigmoid_from_col=None):
    t, d = x2.shape
    n = w_t.shape[1]
    assert t % tm == 0 and n % tn == 0
    sig = None if sigmoid_from_col is None else sigmoid_from_col // tn
    if sigmoid_from_col is not None:
        assert sigmoid_from_col % tn == 0
    return pl.pallas_call(
        functools.partial(_norm_proj_kernel, sigmoid_from=sig,
                          sub=PROJ_SUB_TILE if tn % PROJ_SUB_TILE == 0 else tn),
        grid=(t // tm, n // tn),
        in_specs=[
            pl.BlockSpec((tm, d), lambda m, j: (m, 0)),
            pl.BlockSpec((1, d), lambda m, j: (0, 0)),
            pl.BlockSpec((None, tn, d), lambda m, j: (layer, j, 0)),
        ],
        out_specs=pl.BlockSpec((tm, tn), lambda m, j: (m, j)),
        out_shape=jax.ShapeDtypeStruct((t, n), out_dtype),
        scratch_shapes=[pltpu.VMEM((tm, d), BF16)],
        compiler_params=_params("parallel", "arbitrary"),
        name="norm_proj",
    )(x2, gain.reshape(1, d), w_t)


def _forget_cumsum_kernel(f_ref, b_ref, c_ref, *, cb):
    s = f_ref.shape[1]
    row = lax.broadcasted_iota(jnp.int32, (cb, cb), 0)
    col = lax.broadcasted_iota(jnp.int32, (cb, cb), 1)
    tri = jnp.where(col <= row, 1.0, 0.0).astype(BF16)
    carry = jnp.zeros((1, LANE), F32)
    for j in range(s // cb):
        lf = _log_sigmoid(f_ref[0, j * cb:(j + 1) * cb, :] + b_ref[...])
        c = _split_dot_rhs(tri, lf, 3) + carry
        c_ref[0, j * cb:(j + 1) * cb, :] = c
        carry = c[cb - 1:cb, :]


def _forget_cumsum(small3, b_forget_pad):
    b, s, _ = small3.shape
    cb = min(256, s)
    return pl.pallas_call(
        functools.partial(_forget_cumsum_kernel, cb=cb),
        grid=(b,),
        in_specs=[
            pl.BlockSpec((1, s, LANE), lambda i: (i, 0, SMALL_F // LANE)),
            pl.BlockSpec((1, LANE), lambda i: (0, 0)),
        ],
        out_specs=pl.BlockSpec((1, s, LANE), lambda i: (i, 0, 0)),
        out_shape=jax.ShapeDtypeStruct((b, s, LANE), F32),
        compiler_params=_params("parallel"),
        name="forget_cumsum",
    )(small3, b_forget_pad)


def _sb_kernel(q_ref, k_ref, v_ref, o_ref, *, tq, scale, hp):
    i = pl.program_id(2)
    dh = ATTN_HEAD_DIM
    sub = min(SUFFIX_BLOCK, tq)
    nsub = tq // sub
    qs = [(q_ref[0, :, h * dh:(h + 1) * dh].astype(F32) * (scale * LOG2E)).astype(BF16) for h in range(hp)]
    srow = lax.broadcasted_iota(jnp.int32, (sub, sub), 0)
    scol = lax.broadcasted_iota(jnp.int32, (sub, sub), 1)
    upper = jnp.where(srow > scol, 1.0, 0.0).astype(BF16)

    def weights(h, j, carry, diagonal):
        start = pl.multiple_of(j * tq, tq)
        kb = k_ref[0, pl.ds(start, tq), h * dh:(h + 1) * dh]
        vb = v_ref[0, pl.ds(start, tq), h * dh:(h + 1) * dh]
        w = _dot_nt(qs[h], kb)
        log_beta = jnp.minimum(w, 0.0) - jnp.log(1.0 + jnp.exp2(-jnp.abs(w))) * LOG2E
        log_rest = log_beta - w
        if diagonal:
            strict = (lax.broadcasted_iota(jnp.int32, (tq, tq), 1)
                      < lax.broadcasted_iota(jnp.int32, (tq, tq), 0))
            log_rest = jnp.where(strict, log_rest, 0.0)
        pieces = [None] * nsub
        for sb in reversed(range(nsub)):
            x = log_rest[:, sb * sub:(sb + 1) * sub]
            pieces[sb] = _split_dot_lhs(x, upper, 2) + carry
            carry = carry + jnp.sum(x, axis=1, keepdims=True)
        between = pieces[0] if nsub == 1 else jnp.concatenate(pieces, axis=1)
        a = jnp.exp2(log_beta + between)
        if diagonal:
            a = jnp.where(strict, a, 0.0)
        return carry, a.astype(BF16), vb

    def blocks(js, state, diagonal):
        state = list(state)
        for j in js:
            for h in range(hp):
                carry, acc = state[h]
                carry, a, vb = weights(h, j, carry, diagonal)
                state[h] = (carry, acc + _dot(a, vb))
        return tuple(state)

    state = tuple((jnp.zeros((tq, 1), F32), jnp.zeros((tq, dh), F32)) for _ in range(hp))
    state = blocks([i], state, True)
    odd = lax.rem(i, 2)
    state = lax.cond(odd == 1, lambda st: blocks([i - 1], st, False), lambda st: st, state)

    def body(jj, st):
        j = i - 1 - odd - 2 * jj
        return blocks([j, j - 1], st, False)

    state = lax.fori_loop(0, i // 2, body, state)
    o_ref[0] = jnp.concatenate([acc.astype(o_ref.dtype) for _, acc in state], axis=1)


def _sb_attention(proj3, *, q_col, k_col, v_col, heads, tq):
    b, s, _ = proj3.shape
    hp = SB_HEADS_PER_STEP
    dh = hp * ATTN_HEAD_DIM
    assert heads % hp == 0 and q_col % hp == 0 and k_col % hp == 0 and v_col % hp == 0
    return pl.pallas_call(
        functools.partial(_sb_kernel, tq=tq, scale=ATTN_HEAD_DIM ** -0.5, hp=hp),
        grid=(b, heads // hp, s // tq),
        in_specs=[
            pl.BlockSpec((1, tq, dh), lambda bi, h, i: (bi, i, q_col // hp + h)),
            pl.BlockSpec((1, s, dh), lambda bi, h, i: (bi, 0, k_col // hp + h)),
            pl.BlockSpec((1, s, dh), lambda bi, h, i: (bi, 0, v_col // hp + h)),
        ],
        out_specs=pl.BlockSpec((1, tq, dh), lambda bi, h, i: (bi, i, h)),
        out_shape=jax.ShapeDtypeStruct((b, s, heads * ATTN_HEAD_DIM), BF16),
        compiler_params=_params("parallel", "parallel", "arbitrary"),
        name="sb_attention",
    )(proj3, proj3, proj3)


def _fox_kernel(q_ref, k_ref, v_ref, cum_ref, ck_ref, o_ref, *, tq, scale, hp):
    hg = pl.program_id(1)
    i = pl.program_id(2)
    dh = ATTN_HEAD_DIM
    lane = lax.broadcasted_iota(jnp.int32, (tq, LANE), 1)
    qs, cqs = [], []
    for h in range(hp):
        qs.append((q_ref[0, :, h * dh:(h + 1) * dh].astype(F32) * (scale * LOG2E)).astype(BF16))
        cqs.append(jnp.sum(jnp.where(lane == hg * hp + h, cum_ref[0], 0.0), axis=1, keepdims=True) * LOG2E)

    def scores(h, j):
        start = pl.multiple_of(j * tq, tq)
        kb = k_ref[0, pl.ds(start, tq), h * dh:(h + 1) * dh]
        ck = ck_ref[0, h, pl.ds(j, 1), :] * LOG2E
        return _dot_nt(qs[h], kb) + (cqs[h] - ck), v_ref[0, pl.ds(start, tq), h * dh:(h + 1) * dh]

    causal = (lax.broadcasted_iota(jnp.int32, (tq, tq), 1)
              <= lax.broadcasted_iota(jnp.int32, (tq, tq), 0))

    def first(h):
        z, vb = scores(h, i)
        z = jnp.where(causal, z, MASK_VALUE)
        m = jnp.max(z, axis=1, keepdims=True)
        p = jnp.exp2(z - m)
        return m, jnp.sum(p, axis=1, keepdims=True), _dot(p.astype(BF16), vb)

    def step(h, js, m, l, acc):
        zv = [scores(h, j) for j in js]
        block_max = functools.reduce(jnp.maximum, [jnp.max(z, axis=1, keepdims=True) for z, _ in zv])
        m_new = jnp.maximum(m, block_max)
        alpha = jnp.exp2(m - m_new)
        ps = [jnp.exp2(z - m_new) for z, _ in zv]
        l = alpha * l + functools.reduce(jnp.add, [jnp.sum(p, axis=1, keepdims=True) for p in ps])
        acc = alpha * acc + functools.reduce(jnp.add, [_dot(p.astype(BF16), vb) for p, (_, vb) in zip(ps, zv)])
        return m_new, l, acc

    def steps(js, state):
        return tuple(step(h, js, *state[h]) for h in range(hp))

    state = tuple(first(h) for h in range(hp))
    odd = lax.rem(i, 2)
    state = lax.cond(odd == 1, lambda st: steps([i - 1], st), lambda st: st, state)

    def body(jj, st):
        j = i - 1 - odd - 2 * jj
        return steps([j, j - 1], st)

    state = lax.fori_loop(0, i // 2, body, state)
    o_ref[0] = jnp.concatenate([(acc / l).astype(o_ref.dtype) for _, l, acc in state], axis=1)


def _fox_attention(proj3, cum, ck, *, q_col, k_col, v_col, heads, tq):
    b, s, _ = proj3.shape
    hp = FOX_HEADS_PER_STEP
    dh = hp * ATTN_HEAD_DIM
    nb = s // tq
    assert heads % hp == 0 and q_col % hp == 0 and k_col % hp == 0 and v_col % hp == 0
    return pl.pallas_call(
        functools.partial(_fox_kernel, tq=tq, scale=ATTN_HEAD_DIM ** -0.5, hp=hp),
        grid=(b, heads // hp, nb),
        in_specs=[
            pl.BlockSpec((1, tq, dh), lambda bi, h, i: (bi, i, q_col // hp + h)),
            pl.BlockSpec((1, s, dh), lambda bi, h, i: (bi, 0, k_col // hp + h)),
            pl.BlockSpec((1, s, dh), lambda bi, h, i: (bi, 0, v_col // hp + h)),
            pl.BlockSpec((1, tq, LANE), lambda bi, h, i: (bi, i, 0)),
            pl.BlockSpec((1, hp, nb, tq), lambda bi, h, i: (bi, h, 0, 0)),
        ],
        out_specs=pl.BlockSpec((1, tq, dh), lambda bi, h, i: (bi, i, h)),
        out_shape=jax.ShapeDtypeStruct((b, s, heads * ATTN_HEAD_DIM), BF16),
        compiler_params=_params("parallel", "parallel", "arbitrary"),
        name="fox_attention",
    )(proj3, proj3, proj3, cum, ck)


def _same_head_mask(n):
    row = lax.broadcasted_iota(jnp.int32, (n, n), 0)
    col = lax.broadcasted_iota(jnp.int32, (n, n), 1)
    shift = RWKV_HEAD_DIM.bit_length() - 1
    return jnp.right_shift(row, shift) == jnp.right_shift(col, shift)


def _rwkv_prep_kernel(z_ref, zs_ref, mu_ref, mus_ref, w0_ref, wup_ref, a0_ref, aup_ref, gup_ref,
                      kk_ref, ka_ref,
                      r_out, lw_out, k_out, v_out, kn_out, b_out, g_out,
                      zc_ref, zsc_ref, *, tm, width):
    si = pl.program_id(1)

    @pl.when(si == 0)
    def _():
        zc_ref[...] = jnp.zeros_like(zc_ref)
        zsc_ref[...] = jnp.zeros_like(zsc_ref)

    z = z_ref[0].astype(F32)
    zs = zs_ref[0]
    first = lax.broadcasted_iota(jnp.int32, (tm, 1), 0) == 0

    def shifted(x, carry_ref):
        prev = jnp.where(first, carry_ref[...], pltpu.roll(x, 1, 0))
        carry_ref[...] = x[tm - 1:tm, :]
        return prev

    z = z + (shifted(z, zc_ref) - z) * mu_ref[...]
    zs = zs + (shifted(zs, zsc_ref) - zs) * mus_ref[...]

    r = z[:, :width]
    k = z[:, width:2 * width]
    v = z[:, 2 * width:]
    wd = zs[:, SMALL_WD:SMALL_WD + LANE]
    ad = zs[:, SMALL_AD:SMALL_AD + LANE]
    gd = zs[:, SMALL_GD:SMALL_GD + 2 * LANE]

    wl = w0_ref[...] + _dot(jnp.tanh(wd).astype(BF16), wup_ref[...])
    lw_out[0] = _sigmoid(wl) * (-DECAY_SCALE)
    a = _sigmoid(a0_ref[...] + _dot(ad.astype(BF16), aup_ref[...]))
    g_out[0] = _dot(_sigmoid(gd).astype(BF16), gup_ref[...]).astype(g_out.dtype)

    r_out[0] = r.astype(r_out.dtype)
    v_out[0] = v.astype(v_out.dtype)
    k_out[0] = (k * (1.0 + (a - 1.0) * ka_ref[...])).astype(k_out.dtype)

    kk_raw = k * kk_ref[...]
    ones_bd = jnp.where(_same_head_mask(RWKV_GROUP), 1.0, 0.0).astype(BF16)
    for gi in range(width // RWKV_GROUP):
        sl = slice(gi * RWKV_GROUP, (gi + 1) * RWKV_GROUP)
        x = kk_raw[:, sl]
        ss = _split_dot_lhs(x * x, ones_bd, 1)
        kn = x * lax.rsqrt(jnp.maximum(ss, KEY_NORM_FLOOR ** 2))
        kn_out[0, :, sl] = kn.astype(kn_out.dtype)
        b_out[0, :, sl] = (kn * a[:, sl]).astype(b_out.dtype)


def _rwkv_prep(proj3, small3, mu_rkv, mu_small, w0, w_up_p, a0, a_up_p, g_up_p, k_k, k_a, *, rkv_col, width, tm):
    b, s, _ = proj3.shape
    full = lambda shape: pl.BlockSpec(shape, lambda bi, si: (0,) * len(shape))
    tile = pl.BlockSpec((1, tm, width), lambda bi, si: (bi, si, 0))
    shp = lambda dt: jax.ShapeDtypeStruct((b, s, width), dt)
    return pl.pallas_call(
        functools.partial(_rwkv_prep_kernel, tm=tm, width=width),
        grid=(b, s // tm),
        in_specs=[
            pl.BlockSpec((1, tm, 3 * width), lambda bi, si: (bi, si, rkv_col)),
            pl.BlockSpec((1, tm, SMALL_WIDTH), lambda bi, si: (bi, si, 0)),
            full((1, 3 * width)), full((1, SMALL_WIDTH)),
            full((1, width)), full((LANE, width)),
            full((1, width)), full((LANE, width)),
            full((2 * LANE, width)),
            full((1, width)), full((1, width)),
        ],
        out_specs=[tile] * 7,
        out_shape=[shp(BF16), shp(F32), shp(BF16), shp(BF16), shp(BF16), shp(BF16), shp(BF16)],
        scratch_shapes=[pltpu.VMEM((1, 3 * width), F32), pltpu.VMEM((1, SMALL_WIDTH), F32)],
        compiler_params=_params("parallel", "arbitrary"),
        name="rwkv_prep",
    )(proj3, small3, mu_rkv, mu_small, w0, w_up_p, a0, a_up_p, g_up_p, k_k, k_a)


def _rwkv_scan_kernel(r_ref, lw_ref, k_ref, v_ref, kn_ref, b_ref, g_ref, rk_ref, lnw_ref, lnb_ref,
                      o_ref, s_ref, *, tt):
    st = pl.program_id(2)

    @pl.when(st == 0)
    def _():
        s_ref[...] = jnp.zeros_like(s_ref)

    c = RWKV_CHUNK
    w = RWKV_GROUP
    heads = w // RWKV_HEAD_DIM
    assert heads * c == w
    same = _same_head_mask(w)
    row = lax.broadcasted_iota(jnp.int32, (w, w), 0)
    col = lax.broadcasted_iota(jnp.int32, (w, w), 1)
    strict = same & (col < row)
    incl = same & (col <= row)

    def tile(x):
        return jnp.concatenate([x] * heads, axis=0)

    def stack(x):
        return jnp.where(same, tile(x), 0.0).astype(BF16)

    def unstack(x):
        out = x[0:c]
        for h in range(1, heads):
            out = out + x[h * c:(h + 1) * c]
        return out

    nchunks = tt // c

    assert tt % w == 0
    tri = jnp.where(incl, 1.0, 0.0).astype(BF16)
    cum_all = jnp.concatenate([_split_dot_rhs(tri, lw_ref[0, r0:r0 + w, :], 2) for r0 in range(0, tt, w)],
                              axis=0)

    def operands(ci):
        sl = slice(ci * c, (ci + 1) * c)
        lw = lw_ref[0, sl, :]
        r = r_ref[0, sl, :].astype(F32)
        k = k_ref[0, sl, :].astype(F32)
        v = v_ref[0, sl, :].astype(F32)
        kn = kn_ref[0, sl, :].astype(F32)
        bb = b_ref[0, sl, :].astype(F32)
        cum = cum_all[sl]
        last = cum[c - 1:c, :]
        e_neg = jnp.exp(-cum)
        e_end = jnp.exp(last - cum)
        return dict(
            r_s=stack(r * jnp.exp(cum)), a_s=stack(kn * jnp.exp(cum - lw)), v_s=stack(v),
            bk=jnp.concatenate([(bb * e_neg).astype(BF16), (k * e_neg).astype(BF16)], axis=0),
            b_end=stack(bb * e_end), k_end=stack(k * e_end), w_end=jnp.exp(last))

    first_half = lax.broadcasted_iota(jnp.int32, (w, 2 * c), 1) < c

    def block_diag(x, swapped, use_first, mask):
        half = jnp.where(first_half, x, swapped) if use_first else jnp.where(first_half, swapped, x)
        return jnp.where(mask, jnp.concatenate([half] * (w // (2 * c)), axis=1), 0.0)

    def interactions(ch):
        prod = _dot_nt(jnp.concatenate([ch["a_s"], ch["r_s"]], axis=0), ch["bk"])
        pa, pr = prod[:w], prod[w:]
        sa, sr = pltpu.roll(pa, c, 1), pltpu.roll(pr, c, 1)
        ch["pw"] = -block_diag(pa, sa, True, strict)
        ch["nrm"] = ch["pw"]
        ch["a_ak"] = block_diag(pa, sa, False, strict).astype(BF16)
        ch["a_rb"] = block_diag(pr, sr, True, incl).astype(BF16)
        ch["a_rk"] = block_diag(pr, sr, False, incl).astype(BF16)
        return ch

    def neumann_level(group):
        for ch in group:
            pwb = ch["pw"].astype(BF16)
            ch["pw"] = _dot(pwb, pwb)
        for ch in group:
            ch["nrm"] = ch["nrm"] + ch["pw"] + _dot(ch["nrm"].astype(BF16), ch["pw"].astype(BF16))

    chunks = [operands(ci) for ci in range(nchunks)]
    for ch in chunks:
        interactions(ch)
    for _ in range(c.bit_length() - 2):
        neumann_level(chunks)

    def stage_a(ch):
        ch["nrm"] = ch["nrm"].astype(BF16)
        ch["a_hat"] = (ch["a_s"].astype(F32) + _dot(ch["nrm"], ch["a_s"])).astype(BF16)
        ch["akv"] = _dot(ch["a_ak"], ch["v_s"])
        ch["q_s"] = _dot(ch["a_rk"], ch["v_s"])
        ch["kv"] = _dot_tn(ch["v_s"], ch["k_end"])

    def stage_b(ch):
        ch["v_hat"] = ch["akv"] + _dot(ch["nrm"], ch["akv"].astype(BF16))
        ch["x_mat"] = _dot_tn(ch["a_hat"], ch["b_end"]).astype(BF16)

    def stage_c(ch, state):
        g_mat = ch["kv"] - _dot_tn(ch["v_hat"].astype(BF16), ch["b_end"])
        ch["s0"] = state.astype(BF16)
        return ch["w_end"] * state - _dot(ch["s0"], ch["x_mat"]) + g_mat

    def stage_out(ch):
        u_b = (-(_dot_nt(ch["a_hat"], ch["s0"]) + ch["v_hat"])).astype(BF16)
        return unstack(_dot_nt(ch["r_s"], ch["s0"]) + _dot(ch["a_rb"], u_b) + ch["q_s"])

    state = s_ref[...]
    ys = []
    for step in range(nchunks + 3):
        if step < nchunks:
            stage_a(chunks[step])
        if 0 <= step - 1 < nchunks:
            stage_b(chunks[step - 1])
        if 0 <= step - 2 < nchunks:
            state = stage_c(chunks[step - 2], state)
        if 0 <= step - 3 < nchunks:
            ys.append(stage_out(chunks[step - 3]))
    s_ref[...] = state
    y = jnp.concatenate(ys, axis=0)

    mean_bd = jnp.where(same, 1.0 / RWKV_HEAD_DIM, 0.0).astype(BF16)
    ones_bd = jnp.where(same, 1.0, 0.0).astype(BF16)
    mean = _split_dot_lhs(y, mean_bd, 2)
    d = y - mean
    var = _split_dot_lhs(d * d, mean_bd, 1)
    y = d * lax.rsqrt(var + RWKV_GN_EPS) * lnw_ref[...] + lnb_ref[...]
    r = r_ref[0].astype(F32)
    k = k_ref[0].astype(F32)
    v = v_ref[0].astype(F32)
    bonus = _split_dot_lhs(r * k * rk_ref[...], ones_bd, 1)
    y = y + bonus * v
    o_ref[0] = (y * g_ref[0].astype(F32)).astype(o_ref.dtype)


def _rwkv_scan(r, lw, k, v, kn, bb, g, r_k, ln_w, ln_b, *, tt):
    b, s, width = r.shape
    w = RWKV_GROUP
    tile = pl.BlockSpec((1, tt, w), lambda bi, gi, si: (bi, si, gi))
    vec = pl.BlockSpec((1, w), lambda bi, gi, si: (0, gi))
    return pl.pallas_call(
        functools.partial(_rwkv_scan_kernel, tt=tt),
        grid=(b, width // w, s // tt),
        in_specs=[tile] * 7 + [vec] * 3,
        out_specs=tile,
        out_shape=jax.ShapeDtypeStruct((b, s, width), BF16),
        scratch_shapes=[pltpu.VMEM((w, w), F32)],
        compiler_params=_params("parallel", "parallel", "arbitrary"),
        name="rwkv_scan",
    )(r, lw, k, v, kn, bb, g, r_k, ln_w, ln_b)


def _mix_out_kernel(ya_ref, yb_ref, yc_ref, gate_ref, x_ref, pa_ref, pb_ref, pc_ref, wo_ref, gain_ref, o_ref):
    d = x_ref.shape[1]
    m = gate_ref[:, 0:d].astype(F32) * _dot(ya_ref[...], pa_ref[...])
    m = m + gate_ref[:, d:2 * d].astype(F32) * _dot(yb_ref[...], pb_ref[...])
    m = m + gate_ref[:, 2 * d:3 * d].astype(F32) * _dot(yc_ref[...], pc_ref[...])
    o = _dot(m.astype(BF16), wo_ref[...])
    ms = jnp.mean(o * o, axis=-1, keepdims=True)
    o_ref[...] = x_ref[...] + o * lax.rsqrt(ms + RMS_EPS) * gain_ref[...]


def _mix_out(ya, yb, yc, proj2, x2, pa, pb, pc, wo, gain, *, gate_col_block, tm):
    t, d = x2.shape
    const = lambda shape: pl.BlockSpec(shape, lambda m: (0, 0), pipeline_mode=pl.Buffered(1))
    return pl.pallas_call(
        _mix_out_kernel,
        grid=(t // tm,),
        in_specs=[
            pl.BlockSpec((tm, ya.shape[1]), lambda m: (m, 0)),
            pl.BlockSpec((tm, yb.shape[1]), lambda m: (m, 0)),
            pl.BlockSpec((tm, yc.shape[1]), lambda m: (m, 0)),
            pl.BlockSpec((tm, 3 * d), lambda m: (m, gate_col_block)),
            pl.BlockSpec((tm, d), lambda m: (m, 0)),
            const(pa.shape), const(pb.shape), const(pc.shape), const(wo.shape),
            const((1, d)),
        ],
        out_specs=pl.BlockSpec((tm, d), lambda m: (m, 0)),
        out_shape=jax.ShapeDtypeStruct((t, d), F32),
        compiler_params=_params("parallel"),
        name="mix_out",
    )(ya, yb, yc, proj2, x2, pa, pb, pc, wo, gain.reshape(1, d))


def _mlp_kernel(x_ref, g1_ref, wu_ref, wd_ref, g2_ref, o_ref, u_ref, acc_ref):
    f = pl.program_id(1)

    @pl.when(f == 0)
    def _():
        x = x_ref[...]
        ms = jnp.mean(x * x, axis=-1, keepdims=True)
        u_ref[...] = (x * lax.rsqrt(ms + RMS_EPS) * g1_ref[...]).astype(BF16)
        acc_ref[...] = jnp.zeros_like(acc_ref)

    h = jnp.maximum(_dot(u_ref[...], wu_ref[...]), 0.0)
    acc_ref[...] += _dot((h * h).astype(BF16), wd_ref[...])

    @pl.when(f == pl.num_programs(1) - 1)
    def _():
        o = acc_ref[...]
        ms = jnp.mean(o * o, axis=-1, keepdims=True)
        o_ref[...] = x_ref[...] + o * lax.rsqrt(ms + RMS_EPS) * g2_ref[...]


def _mlp(x2, g1, wu, wd, g2, *, tm, tf):
    t, d = x2.shape
    ff = wu.shape[1]
    return pl.pallas_call(
        _mlp_kernel,
        grid=(t // tm, ff // tf),
        in_specs=[
            pl.BlockSpec((tm, d), lambda m, f: (m, 0)),
            pl.BlockSpec((1, d), lambda m, f: (0, 0)),
            pl.BlockSpec((d, tf), lambda m, f: (0, f)),
            pl.BlockSpec((tf, d), lambda m, f: (f, 0)),
            pl.BlockSpec((1, d), lambda m, f: (0, 0)),
        ],
        out_specs=pl.BlockSpec((tm, d), lambda m, f: (m, 0)),
        out_shape=jax.ShapeDtypeStruct((t, d), F32),
        scratch_shapes=[pltpu.VMEM((tm, d), BF16), pltpu.VMEM((tm, d), F32)],
        compiler_params=_params("parallel", "arbitrary"),
        name="mlp",
    )(x2, g1.reshape(1, d), wu, wd, g2.reshape(1, d))


def _pad_rows(w, rows):
    return jnp.pad(w, ((0, rows - w.shape[0]), (0, 0)))


def _pad_cols(w, cols):
    return jnp.pad(w, ((0, 0), (0, cols - w.shape[1])))


def _prepare_w_in(w_in, d, sbw, fxw, rww):
    fx_heads = fxw // ATTN_HEAD_DIM
    o_f = 3 * sbw + 3 * fxw
    o_rw = o_f + fx_heads
    o_wd = o_rw + 3 * rww
    o_ad = o_wd + DECAY_LORA
    o_gd = o_ad + AAA_LORA
    o_gate = o_gd + GATE_LORA
    assert w_in.shape[2] == o_gate + 3 * d
    w_t = jnp.transpose(w_in, (0, 2, 1))
    main = _repack_w_in(w_t, [(0, o_f), (o_rw, 3 * rww), (o_gate, 3 * d)], tn=TILE["repack_n"])

    def rows(lo, hi, padded):
        return jnp.pad(w_t[:, lo:hi], ((0, 0), (0, padded - (hi - lo)), (0, 0)))

    small = jnp.concatenate([rows(o_wd, o_ad, LANE), rows(o_ad, o_gd, LANE), rows(o_gd, o_gate, 2 * LANE),
                             rows(o_f, o_rw, LANE)], axis=1)
    return main, small


def _mixer_layer(x, layer, norm_pre, norm_post, w_main_t, w_small_t, b_forget, mu, w0, w_up, a0, a_up, g_up,
                 k_k, k_a, r_k, ln_w, ln_b, pa, pb, pc, w_out):
    bsz, s, d = x.shape
    t = bsz * s
    sbw = pa.shape[0]
    fxw = pb.shape[0]
    rww = pc.shape[0]
    sb_heads = sbw // ATTN_HEAD_DIM
    fx_heads = fxw // ATTN_HEAD_DIM

    gate_col = 3 * sbw + 3 * fxw + 3 * rww
    mu_rkv = mu[None, :3 * rww]
    mu_small = jnp.concatenate([
        _pad_cols(mu[None, 3 * rww:3 * rww + DECAY_LORA], LANE),
        _pad_cols(mu[None, 3 * rww + DECAY_LORA:3 * rww + DECAY_LORA + AAA_LORA], LANE),
        _pad_cols(mu[None, 3 * rww + DECAY_LORA + AAA_LORA:], 2 * LANE),
        jnp.zeros((1, LANE), F32)], axis=1)

    x2 = x.reshape(t, d)
    proj_m = min(TILE["proj_m"], t)
    proj = _norm_proj(x2, norm_pre, w_main_t, layer, BF16, tm=proj_m, tn=TILE["proj_n"],
                      sigmoid_from_col=gate_col)
    small = _norm_proj(x2, norm_pre, w_small_t, layer, F32, tm=proj_m, tn=SMALL_WIDTH)
    proj3 = proj.reshape(bsz, s, -1)
    small3 = small.reshape(bsz, s, SMALL_WIDTH)

    tq = min(TILE["attn_q"], s)
    qa = 0
    ya = _sb_attention(proj3, q_col=qa, k_col=qa + sb_heads, v_col=qa + 2 * sb_heads, heads=sb_heads, tq=tq)

    cum = _forget_cumsum(small3, _pad_cols(b_forget[None, :], LANE))
    c_hs = cum[:, :, :fx_heads].transpose(0, 2, 1)
    qb = 3 * sb_heads
    yb = _fox_attention(proj3, cum, c_hs.reshape(bsz, fx_heads, s // tq, tq),
                        q_col=qb, k_col=qb + fx_heads, v_col=qb + 2 * fx_heads, heads=fx_heads, tq=tq)

    rkv_col = (3 * sbw + 3 * fxw) // (3 * rww)
    assert rkv_col * 3 * rww == 3 * sbw + 3 * fxw
    row = lambda p: p.reshape(1, rww)
    r, lw, k, v, kn, bb, g = _rwkv_prep(
        proj3, small3, mu_rkv, mu_small, row(w0), _pad_rows(w_up, LANE).astype(BF16), row(a0),
        _pad_rows(a_up, LANE).astype(BF16), _pad_rows(g_up, 2 * LANE).astype(BF16), row(k_k), row(k_a),
        rkv_col=rkv_col, width=rww, tm=min(TILE["rwkv_prep_m"], s))
    yc = _rwkv_scan(r, lw, k, v, kn, bb, g, row(r_k), row(ln_w), row(ln_b), tt=min(TILE["rwkv_scan_t"], s))

    assert gate_col % (3 * d) == 0
    out = _mix_out(ya.reshape(t, sbw), yb.reshape(t, fxw), yc.reshape(t, rww), proj, x2,
                   pa.astype(BF16), pb.astype(BF16), pc.astype(BF16), w_out.astype(BF16), norm_post,
                   gate_col_block=gate_col // (3 * d), tm=min(TILE["mix_m"], t))
    return out.reshape(bsz, s, d)


def kernel(x, norm_mix_pre, norm_mix_post, norm_mlp_pre, norm_mlp_post, w_in, b_forget, rwkv_mu, rwkv_w0,
           rwkv_w_up, rwkv_a0, rwkv_a_up, rwkv_g_up, rwkv_k_k, rwkv_k_a, rwkv_r_k, rwkv_ln_w, rwkv_ln_b,
           w_branch_a, w_branch_b, w_branch_c, w_out, w_mlp_up, w_mlp_down):
    bsz, s, d = x.shape
    t = bsz * s
    w_main_t, w_small_t = _prepare_w_in(w_in, d, w_branch_a.shape[1], w_branch_b.shape[1], w_branch_c.shape[1])
    for l in range(w_in.shape[0]):
        x = _mixer_layer(x, l, norm_mix_pre[l], norm_mix_post[l], w_main_t, w_small_t, b_forget[l], rwkv_mu[l],
                         rwkv_w0[l], rwkv_w_up[l], rwkv_a0[l], rwkv_a_up[l], rwkv_g_up[l], rwkv_k_k[l],
                         rwkv_k_a[l], rwkv_r_k[l], rwkv_ln_w[l], rwkv_ln_b[l], w_branch_a[l], w_branch_b[l],
                         w_branch_c[l], w_out[l])
        x = _mlp(x.reshape(t, d), norm_mlp_pre[l], w_mlp_up[l].astype(BF16), w_mlp_down[l].astype(BF16),
                 norm_mlp_post[l], tm=min(TILE["mlp_m"], t), tf=TILE["mlp_f"]).reshape(bsz, s, d)
    return x
```

```python
import functools

import jax
import jax.numpy as jnp
from jax import lax
from jax.experimental import pallas as pl
from jax.experimental.pallas import tpu as pltpu

F32 = jnp.float32
BF16 = jnp.bfloat16

RMS_EPS = 1e-6
RWKV_GN_EPS = 64e-5
DECAY_SCALE = 0.6065306597126334
KEY_NORM_FLOOR = 1e-12
ATTN_HEAD_DIM = 128
SB_HEADS_PER_STEP = 2
FOX_HEADS_PER_STEP = 1
RWKV_HEAD_DIM = 64
RWKV_CHUNK = 64
RWKV_GROUP = 256
RWKV_GROUPS_PER_STEP = 2
DECAY_LORA = 64
AAA_LORA = 64
GATE_LORA = 160
LANE = 128
SUBLANE = 8
MASK_VALUE = -1e30
LOG2E = 1.4426950408889634
SUFFIX_BLOCK = 256
PROJ_SUB_TILE = 512
VMEM_LIMIT = 56 * 1024 * 1024

TILE = dict(
    repack_n=512,
    proj_m=1024, proj_n=2048,
    attn_q=512,
    rwkv_prep_m=256,
    rwkv_scan_t=512,
    mix_m=256,
    mlp_m=512, mlp_f=1024,
)

SMALL_WD = 0
SMALL_AD = 128
SMALL_GD = 256
SMALL_F = 512
SMALL_WIDTH = 640


def _dot(a, b):
    return jnp.dot(a, b, preferred_element_type=F32)


def _dot_nt(a, b):
    return lax.dot_general(a, b, (((1,), (1,)), ((), ())), preferred_element_type=F32)


def _dot_tn(a, b):
    return lax.dot_general(a, b, (((0,), (0,)), ((), ())), preferred_element_type=F32)


def _bf16_terms(x, parts):
    terms = []
    rem = x
    for p in range(parts):
        h = rem.astype(BF16)
        terms.append(h)
        if p + 1 < parts:
            rem = rem - h.astype(F32)
    return terms


def _split_dot_rhs(m_bf16, x, parts):
    if parts == 1:
        return _dot(m_bf16, x.astype(BF16))
    return _dot(jnp.concatenate([m_bf16] * parts, axis=1), jnp.concatenate(_bf16_terms(x, parts), axis=0))


def _split_dot_lhs(x, m_bf16, parts):
    if parts == 1:
        return _dot(x.astype(BF16), m_bf16)
    return _dot(jnp.concatenate(_bf16_terms(x, parts), axis=1), jnp.concatenate([m_bf16] * parts, axis=0))


def _log_sigmoid(z):
    return jnp.minimum(z, 0.0) - jnp.log(1.0 + jnp.exp(-jnp.abs(z)))


def _sigmoid(z):
    return 1.0 / (1.0 + jnp.exp(-z))


def _params(*sem):
    return pltpu.CompilerParams(dimension_semantics=sem, vmem_limit_bytes=VMEM_LIMIT)


def _repack_kernel(w_ref, tail_ref, o_ref, *, tile_shifts, tail_tile, tn):
    j = pl.program_id(1)
    for shift in sorted(set(tile_shifts)):
        @pl.when(functools.reduce(jnp.logical_or, [j == t for t, s in enumerate(tile_shifts)
                                                   if s == shift and t != tail_tile]))
        def _(shift=shift):
            o_ref[0] = w_ref[0, shift:shift + tn, :].astype(o_ref.dtype)

    @pl.when(j == tail_tile)
    def _():
        shift = tile_shifts[tail_tile]
        o_ref[0] = tail_ref[0, shift:shift + tn, :].astype(o_ref.dtype)


def _repack_w_in(w_t, segments, *, tn):
    layers, n_in, k = w_t.shape
    n_out = sum(width for _, width in segments)
    assert all(width % tn == 0 for _, width in segments)
    win = tn + SUBLANE
    tile_starts = [start + off for start, width in segments for off in range(0, width, tn)]
    tile_shifts = [s % SUBLANE for s in tile_starts]
    tail_tile = len(tile_starts) - 1
    tail_from = tile_starts[tail_tile] - tile_shifts[tail_tile]
    assert all(s - s % SUBLANE + win <= n_in for s in tile_starts[:tail_tile])
    tail = jnp.pad(w_t[:, tail_from:], ((0, 0), (0, tail_from + win - n_in), (0, 0)))
    last_window = (n_in - win) // SUBLANE * SUBLANE

    def window_start(j):
        row = j * tn
        packed = 0
        start = jnp.int32(0)
        for seg_start, width in segments:
            start = jnp.where(row >= packed, seg_start - seg_start % SUBLANE - packed, start)
            packed += width
        return jnp.minimum(row + start, last_window)

    return pl.pallas_call(
        functools.partial(_repack_kernel, tile_shifts=tile_shifts, tail_tile=tail_tile, tn=tn),
        grid=(layers, n_out // tn),
        in_specs=[pl.BlockSpec((pl.Element(1), pl.Element(win), pl.Element(k)),
                               lambda l, j: (l, pl.multiple_of(window_start(j), SUBLANE), 0)),
                  pl.BlockSpec((1, win, k), lambda l, j: (l, 0, 0))],
        out_specs=pl.BlockSpec((1, tn, k), lambda l, j: (l, j, 0)),
        out_shape=jax.ShapeDtypeStruct((layers, n_out, k), BF16),
        compiler_params=_params("parallel", "parallel"),
        name="repack_w_in",
    )(w_t, tail)


def _norm_proj_kernel(x_ref, g_ref, w_ref, o_ref, u_ref, *, sigmoid_from, sub):
    n = pl.program_id(1)

    @pl.when(n == 0)
    def _():
        x = x_ref[...]
        ms = jnp.mean(x * x, axis=-1, keepdims=True)
        u_ref[...] = (x * lax.rsqrt(ms + RMS_EPS) * g_ref[...]).astype(BF16)

    def columns(apply_sigmoid):
        for c0 in range(0, o_ref.shape[1], sub):
            acc = _dot_nt(u_ref[...], w_ref[c0:c0 + sub, :].astype(BF16))
            if apply_sigmoid:
                acc = _sigmoid(acc)
            o_ref[:, c0:c0 + sub] = acc.astype(o_ref.dtype)

    if sigmoid_from is None:
        columns(False)
    else:
        pl.when(n < sigmoid_from)(lambda: columns(False))
        pl.when(n >= sigmoid_from)(lambda: columns(True))


def _norm_proj(x2, gain, w_t, layer, out_dtype, *, tm, tn, sigmoid_from_col=None):
    t, d = x2.shape
    n = w_t.shape[1]
    assert t % tm == 0 and n % tn == 0
    sig = None if sigmoid_from_col is None else sigmoid_from_col // tn
    if sigmoid_from_col is not None:
        assert sigmoid_from_col % tn == 0
    return pl.pallas_call(
        functools.partial(_norm_proj_kernel, sigmoid_from=sig,
                          sub=PROJ_SUB_TILE if tn % PROJ_SUB_TILE == 0 else tn),
        grid=(t // tm, n // tn),
        in_specs=[
            pl.BlockSpec((tm, d), lambda m, j: (m, 0)),
            pl.BlockSpec((1, d), lambda m, j: (0, 0)),
            pl.BlockSpec((None, tn, d), lambda m, j: (layer, j, 0)),
        ],
        out_specs=pl.BlockSpec((tm, tn), lambda m, j: (m, j)),
        out_shape=jax.ShapeDtypeStruct((t, n), out_dtype),
        scratch_shapes=[pltpu.VMEM((tm, d), BF16)],
        compiler_params=_params("parallel", "arbitrary"),
        name="norm_proj",
    )(x2, gain.reshape(1, d), w_t)


def _forget_cumsum_kernel(f_ref, b_ref, c_ref, *, cb):
    s = f_ref.shape[1]
    row = lax.broadcasted_iota(jnp.int32, (cb, cb), 0)
    col = lax.broadcasted_iota(jnp.int32, (cb, cb), 1)
    tri = jnp.where(col <= row, 1.0, 0.0).astype(BF16)
    carry = jnp.zeros((1, LANE), F32)
    for j in range(s // cb):
        lf = _log_sigmoid(f_ref[0, j * cb:(j + 1) * cb, :] + b_ref[...])
        c = _split_dot_rhs(tri, lf, 3) + carry
        c_ref[0, j * cb:(j + 1) * cb, :] = c
        carry = c[cb - 1:cb, :]


def _forget_cumsum(small3, b_forget_pad):
    b, s, _ = small3.shape
    cb = min(256, s)
    return pl.pallas_call(
        functools.partial(_forget_cumsum_kernel, cb=cb),
        grid=(b,),
        in_specs=[
            pl.BlockSpec((1, s, LANE), lambda i: (i, 0, SMALL_F // LANE)),
            pl.BlockSpec((1, LANE), lambda i: (0, 0)),
        ],
        out_specs=pl.BlockSpec((1, s, LANE), lambda i: (i, 0, 0)),
        out_shape=jax.ShapeDtypeStruct((b, s, LANE), F32),
        compiler_params=_params("parallel"),
        name="forget_cumsum",
    )(small3, b_forget_pad)


def _sb_kernel(q_ref, k_ref, v_ref, o_ref, *, tq, scale, hp):
    i = pl.program_id(2)
    dh = ATTN_HEAD_DIM
    sub = min(SUFFIX_BLOCK, tq)
    nsub = tq // sub
    qs = [(q_ref[0, :, h * dh:(h + 1) * dh].astype(F32) * (scale * LOG2E)).astype(BF16) for h in range(hp)]
    srow = lax.broadcasted_iota(jnp.int32, (sub, sub), 0)
    scol = lax.broadcasted_iota(jnp.int32, (sub, sub), 1)
    upper = jnp.where(srow > scol, 1.0, 0.0).astype(BF16)

    def weights(h, j, carry, diagonal):
        start = pl.multiple_of(j * tq, tq)
        kb = k_ref[0, pl.ds(start, tq), h * dh:(h + 1) * dh]
        vb = v_ref[0, pl.ds(start, tq), h * dh:(h + 1) * dh]
        w = _dot_nt(qs[h], kb)
        log_beta = jnp.minimum(w, 0.0) - jnp.log(1.0 + jnp.exp2(-jnp.abs(w))) * LOG2E
        log_rest = log_beta - w
        if diagonal:
            strict = (lax.broadcasted_iota(jnp.int32, (tq, tq), 1)
                      < lax.broadcasted_iota(jnp.int32, (tq, tq), 0))
            log_rest = jnp.where(strict, log_rest, 0.0)
        pieces = [None] * nsub
        for sb in reversed(range(nsub)):
            x = log_rest[:, sb * sub:(sb + 1) * sub]
            pieces[sb] = _split_dot_lhs(x, upper, 2) + carry
            carry = carry + jnp.sum(x, axis=1, keepdims=True)
        between = pieces[0] if nsub == 1 else jnp.concatenate(pieces, axis=1)
        a = jnp.exp2(log_beta + between)
        if diagonal:
            a = jnp.where(strict, a, 0.0)
        return carry, a.astype(BF16), vb

    def blocks(js, state, diagonal):
        state = list(state)
        for j in js:
            for h in range(hp):
                carry, acc = state[h]
                carry, a, vb = weights(h, j, carry, diagonal)
                state[h] = (carry, acc + _dot(a, vb))
        return tuple(state)

    state = tuple((jnp.zeros((tq, 1), F32), jnp.zeros((tq, dh), F32)) for _ in range(hp))
    state = blocks([i], state, True)
    odd = lax.rem(i, 2)
    state = lax.cond(odd == 1, lambda st: blocks([i - 1], st, False), lambda st: st, state)

    def body(jj, st):
        j = i - 1 - odd - 2 * jj
        return blocks([j, j - 1], st, False)

    state = lax.fori_loop(0, i // 2, body, state)
    o_ref[0] = jnp.concatenate([acc.astype(o_ref.dtype) for _, acc in state], axis=1)


def _sb_attention(proj3, *, q_col, k_col, v_col, heads, tq):
    b, s, _ = proj3.shape
    hp = SB_HEADS_PER_STEP
    dh = hp * ATTN_HEAD_DIM
    assert heads % hp == 0 and q_col % hp == 0 and k_col % hp == 0 and v_col % hp == 0
    return pl.pallas_call(
        functools.partial(_sb_kernel, tq=tq, scale=ATTN_HEAD_DIM ** -0.5, hp=hp),
        grid=(b, heads // hp, s // tq),
        in_specs=[
            pl.BlockSpec((1, tq, dh), lambda bi, h, i: (bi, i, q_col // hp + h)),
            pl.BlockSpec((1, s, dh), lambda bi, h, i: (bi, 0, k_col // hp + h)),
            pl.BlockSpec((1, s, dh), lambda bi, h, i: (bi, 0, v_col // hp + h)),
        ],
        out_specs=pl.BlockSpec((1, tq, dh), lambda bi, h, i: (bi, i, h)),
        out_shape=jax.ShapeDtypeStruct((b, s, heads * ATTN_HEAD_DIM), BF16),
        compiler_params=_params("parallel", "parallel", "arbitrary"),
        name="sb_attention",
    )(proj3, proj3, proj3)


def _fox_kernel(q_ref, k_ref, v_ref, cum_ref, ck_ref, o_ref, *, tq, scale, hp):
    hg = pl.program_id(1)
    i = pl.program_id(2)
    dh = ATTN_HEAD_DIM
    lane = lax.broadcasted_iota(jnp.int32, (tq, LANE), 1)
    qs, cqs = [], []
    for h in range(hp):
        qs.append((q_ref[0, :, h * dh:(h + 1) * dh].astype(F32) * (scale * LOG2E)).astype(BF16))
        cqs.append(jnp.sum(jnp.where(lane == hg * hp + h, cum_ref[0], 0.0), axis=1, keepdims=True) * LOG2E)

    def scores(h, j):
        start = pl.multiple_of(j * tq, tq)
        kb = k_ref[0, pl.ds(start, tq), h * dh:(h + 1) * dh]
        ck = ck_ref[0, h, pl.ds(j, 1), :] * LOG2E
        return _dot_nt(qs[h], kb) + (cqs[h] - ck), v_ref[0, pl.ds(start, tq), h * dh:(h + 1) * dh]

    causal = (lax.broadcasted_iota(jnp.int32, (tq, tq), 1)
              <= lax.broadcasted_iota(jnp.int32, (tq, tq), 0))

    def first(h):
        z, vb = scores(h, i)
        z = jnp.where(causal, z, MASK_VALUE)
        m = jnp.max(z, axis=1, keepdims=True)
        p = jnp.exp2(z - m)
        return m, jnp.sum(p, axis=1, keepdims=True), _dot(p.astype(BF16), vb)

    def step(h, js, m, l, acc):
        zv = [scores(h, j) for j in js]
        block_max = functools.reduce(jnp.maximum, [jnp.max(z, axis=1, keepdims=True) for z, _ in zv])
        m_new = jnp.maximum(m, block_max)
        alpha = jnp.exp2(m - m_new)
        ps = [jnp.exp2(z - m_new) for z, _ in zv]
        l = alpha * l + functools.reduce(jnp.add, [jnp.sum(p, axis=1, keepdims=True) for p in ps])
        acc = alpha * acc + functools.reduce(jnp.add, [_dot(p.astype(BF16), vb) for p, (_, vb) in zip(ps, zv)])
        return m_new, l, acc

    def steps(js, state):
        return tuple(step(h, js, *state[h]) for h in range(hp))

    state = tuple(first(h) for h in range(hp))
    odd = lax.rem(i, 2)
    state = lax.cond(odd == 1, lambda st: steps([i - 1], st), lambda st: st, state)

    def body(jj, st):
        j = i - 1 - odd - 2 * jj
        return steps([j, j - 1], st)

    state = lax.fori_loop(0, i // 2, body, state)
    o_ref[0] = jnp.concatenate([(acc / l).astype(o_ref.dtype) for _, l, acc in state], axis=1)


def _fox_attention(proj3, cum, ck, *, q_col, k_col, v_col, heads, tq):
    b, s, _ = proj3.shape
    hp = FOX_HEADS_PER_STEP
    dh = hp * ATTN_HEAD_DIM
    nb = s // tq
    assert heads % hp == 0 and q_col % hp == 0 and k_col % hp == 0 and v_col % hp == 0
    return pl.pallas_call(
        functools.partial(_fox_kernel, tq=tq, scale=ATTN_HEAD_DIM ** -0.5, hp=hp),
        grid=(b, heads // hp, nb),
        in_specs=[
            pl.BlockSpec((1, tq, dh), lambda bi, h, i: (bi, i, q_col // hp + h)),
            pl.BlockSpec((1, s, dh), lambda bi, h, i: (bi, 0, k_col // hp + h)),
            pl.BlockSpec((1, s, dh), lambda bi, h, i: (bi, 0, v_col // hp + h)),
            pl.BlockSpec((1, tq, LANE), lambda bi, h, i: (bi, i, 0)),
            pl.BlockSpec((1, hp, nb, tq), lambda bi, h, i: (bi, h, 0, 0)),
        ],
        out_specs=pl.BlockSpec((1, tq, dh), lambda bi, h, i: (bi, i, h)),
        out_shape=jax.ShapeDtypeStruct((b, s, heads * ATTN_HEAD_DIM), BF16),
        compiler_params=_params("parallel", "parallel", "arbitrary"),
        name="fox_attention",
    )(proj3, proj3, proj3, cum, ck)


def _same_head_mask(n):
    row = lax.broadcasted_iota(jnp.int32, (n, n), 0)
    col = lax.broadcasted_iota(jnp.int32, (n, n), 1)
    shift = RWKV_HEAD_DIM.bit_length() - 1
    return jnp.right_shift(row, shift) == jnp.right_shift(col, shift)


def _rwkv_prep_kernel(z_ref, zs_ref, mu_ref, mus_ref, w0_ref, wup_ref, a0_ref, aup_ref, gup_ref,
                      kk_ref, ka_ref,
                      r_out, lw_out, k_out, v_out, kn_out, b_out, g_out,
                      zc_ref, zsc_ref, *, tm, width):
    si = pl.program_id(1)

    @pl.when(si == 0)
    def _():
        zc_ref[...] = jnp.zeros_like(zc_ref)
        zsc_ref[...] = jnp.zeros_like(zsc_ref)

    z = z_ref[0].astype(F32)
    zs = zs_ref[0]
    first = lax.broadcasted_iota(jnp.int32, (tm, 1), 0) == 0

    def shifted(x, carry_ref):
        prev = jnp.where(first, carry_ref[...], pltpu.roll(x, 1, 0))
        carry_ref[...] = x[tm - 1:tm, :]
        return prev

    z = z + (shifted(z, zc_ref) - z) * mu_ref[...]
    zs = zs + (shifted(zs, zsc_ref) - zs) * mus_ref[...]

    r = z[:, :width]
    k = z[:, width:2 * width]
    v = z[:, 2 * width:]
    wd = zs[:, SMALL_WD:SMALL_WD + LANE]
    ad = zs[:, SMALL_AD:SMALL_AD + LANE]
    gd = zs[:, SMALL_GD:SMALL_GD + 2 * LANE]

    wl = w0_ref[...] + _dot(jnp.tanh(wd).astype(BF16), wup_ref[...])
    lw_out[0] = _sigmoid(wl) * (-DECAY_SCALE)
    a = _sigmoid(a0_ref[...] + _dot(ad.astype(BF16), aup_ref[...]))
    g_out[0] = _dot(_sigmoid(gd).astype(BF16), gup_ref[...]).astype(g_out.dtype)

    r_out[0] = r.astype(r_out.dtype)
    v_out[0] = v.astype(v_out.dtype)
    k_out[0] = (k * (1.0 + (a - 1.0) * ka_ref[...])).astype(k_out.dtype)

    kk_raw = k * kk_ref[...]
    ones_bd = jnp.where(_same_head_mask(RWKV_GROUP), 1.0, 0.0).astype(BF16)
    for gi in range(width // RWKV_GROUP):
        sl = slice(gi * RWKV_GROUP, (gi + 1) * RWKV_GROUP)
        x = kk_raw[:, sl]
        ss = _split_dot_lhs(x * x, ones_bd, 1)
        kn = x * lax.rsqrt(jnp.maximum(ss, KEY_NORM_FLOOR ** 2))
        kn_out[0, :, sl] = kn.astype(kn_out.dtype)
        b_out[0, :, sl] = (kn * a[:, sl]).astype(b_out.dtype)


def _rwkv_prep(proj3, small3, mu_rkv, mu_small, w0, w_up_p, a0, a_up_p, g_up_p, k_k, k_a, *, rkv_col, width, tm):
    b, s, _ = proj3.shape
    full = lambda shape: pl.BlockSpec(shape, lambda bi, si: (0,) * len(shape))
    tile = pl.BlockSpec((1, tm, width), lambda bi, si: (bi, si, 0))
    shp = lambda dt: jax.ShapeDtypeStruct((b, s, width), dt)
    return pl.pallas_call(
        functools.partial(_rwkv_prep_kernel, tm=tm, width=width),
        grid=(b, s // tm),
        in_specs=[
            pl.BlockSpec((1, tm, 3 * width), lambda bi, si: (bi, si, rkv_col)),
            pl.BlockSpec((1, tm, SMALL_WIDTH), lambda bi, si: (bi, si, 0)),
            full((1, 3 * width)), full((1, SMALL_WIDTH)),
            full((1, width)), full((LANE, width)),
            full((1, width)), full((LANE, width)),
            full((2 * LANE, width)),
            full((1, width)), full((1, width)),
        ],
        out_specs=[tile] * 7,
        out_shape=[shp(BF16), shp(F32), shp(BF16), shp(BF16), shp(BF16), shp(BF16), shp(BF16)],
        scratch_shapes=[pltpu.VMEM((1, 3 * width), F32), pltpu.VMEM((1, SMALL_WIDTH), F32)],
        compiler_params=_params("parallel", "arbitrary"),
        name="rwkv_prep",
    )(proj3, small3, mu_rkv, mu_small, w0, w_up_p, a0, a_up_p, g_up_p, k_k, k_a)


def _rwkv_scan_kernel(r_ref, lw_ref, k_ref, v_ref, kn_ref, b_ref, g_ref, rk_ref, lnw_ref, lnb_ref,
                      o_ref, s_ref, *, tt, gp):
    st = pl.program_id(2)

    @pl.when(st == 0)
    def _():
        s_ref[...] = jnp.zeros_like(s_ref)

    c = RWKV_CHUNK
    w = RWKV_GROUP
    heads = w // RWKV_HEAD_DIM
    assert heads * c == w
    same = _same_head_mask(w)
    row = lax.broadcasted_iota(jnp.int32, (w, w), 0)
    col = lax.broadcasted_iota(jnp.int32, (w, w), 1)
    strict = same & (col < row)
    incl = same & (col <= row)

    def tile(x):
        return jnp.concatenate([x] * heads, axis=0)

    def stack(x):
        return jnp.where(same, tile(x), 0.0).astype(BF16)

    def unstack(x):
        out = x[0:c]
        for h in range(1, heads):
            out = out + x[h * c:(h + 1) * c]
        return out

    nchunks = tt // c

    assert tt % w == 0
    tri = jnp.where(incl, 1.0, 0.0).astype(BF16)
    lanes = [slice(gi * w, (gi + 1) * w) for gi in range(gp)]
    cum_all = [jnp.concatenate([_split_dot_rhs(tri, lw_ref[0, r0:r0 + w, ln], 2) for r0 in range(0, tt, w)],
                               axis=0) for ln in lanes]

    def operands(ci, gi):
        sl = slice(ci * c, (ci + 1) * c)
        ln = lanes[gi]
        lw = lw_ref[0, sl, ln]
        r = r_ref[0, sl, ln].astype(F32)
        k = k_ref[0, sl, ln].astype(F32)
        v = v_ref[0, sl, ln].astype(F32)
        kn = kn_ref[0, sl, ln].astype(F32)
        bb = b_ref[0, sl, ln].astype(F32)
        cum = cum_all[gi][sl]
        last = cum[c - 1:c, :]
        e_neg = jnp.exp(-cum)
        e_end = jnp.exp(last - cum)
        return dict(
            gi=gi,
            r_s=stack(r * jnp.exp(cum)), a_s=stack(kn * jnp.exp(cum - lw)), v_s=stack(v),
            bk=jnp.concatenate([(bb * e_neg).astype(BF16), (k * e_neg).astype(BF16)], axis=0),
            b_end=stack(bb * e_end), k_end=stack(k * e_end), w_end=jnp.exp(last))

    first_half = lax.broadcasted_iota(jnp.int32, (w, 2 * c), 1) < c

    def block_diag(x, swapped, use_first, mask):
        half = jnp.where(first_half, x, swapped) if use_first else jnp.where(first_half, swapped, x)
        return jnp.where(mask, jnp.concatenate([half] * (w // (2 * c)), axis=1), 0.0)

    def interactions(ch):
        prod = _dot_nt(jnp.concatenate([ch["a_s"], ch["r_s"]], axis=0), ch["bk"])
        pa, pr = prod[:w], prod[w:]
        sa, sr = pltpu.roll(pa, c, 1), pltpu.roll(pr, c, 1)
        ch["pw"] = -block_diag(pa, sa, True, strict)
        ch["nrm"] = ch["pw"]
        ch["a_ak"] = block_diag(pa, sa, False, strict).astype(BF16)
        ch["a_rb"] = block_diag(pr, sr, True, incl).astype(BF16)
        ch["a_rk"] = block_diag(pr, sr, False, incl).astype(BF16)
        return ch

    def neumann_level(group):
        for ch in group:
            pwb = ch["pw"].astype(BF16)
            ch["pw"] = _dot(pwb, pwb)
        for ch in group:
            ch["nrm"] = ch["nrm"] + ch["pw"] + _dot(ch["nrm"].astype(BF16), ch["pw"].astype(BF16))

    chunks = [operands(ci, gi) for ci in range(nchunks) for gi in range(gp)]
    for ch in chunks:
        interactions(ch)
    for _ in range(c.bit_length() - 2):
        neumann_level(chunks)

    def stage_a(ch):
        ch["nrm"] = ch["nrm"].astype(BF16)
        ch["a_hat"] = (ch["a_s"].astype(F32) + _dot(ch["nrm"], ch["a_s"])).astype(BF16)
        ch["akv"] = _dot(ch["a_ak"], ch["v_s"])
        ch["q_s"] = _dot(ch["a_rk"], ch["v_s"])
        ch["kv"] = _dot_tn(ch["v_s"], ch["k_end"])

    def stage_b(ch):
        ch["v_hat"] = ch["akv"] + _dot(ch["nrm"], ch["akv"].astype(BF16))

    def stage_c(ch, state):
        ch["s0"] = state.astype(BF16)
        ch["u"] = (-(_dot_nt(ch["a_hat"], ch["s0"]) + ch["v_hat"])).astype(BF16)
        return ch["w_end"] * state + _dot_tn(ch["u"], ch["b_end"]) + ch["kv"]

    def stage_out(ch):
        return unstack(_dot_nt(ch["r_s"], ch["s0"]) + _dot(ch["a_rb"], ch["u"]) + ch["q_s"])

    states = [s_ref[gi] for gi in range(gp)]
    ys = [[] for _ in range(gp)]
    for step in range(len(chunks) + 3):
        if step < len(chunks):
            stage_a(chunks[step])
        if 0 <= step - 1 < len(chunks):
            stage_b(chunks[step - 1])
        if 0 <= step - 2 < len(chunks):
            ch = chunks[step - 2]
            states[ch["gi"]] = stage_c(ch, states[ch["gi"]])
        if 0 <= step - 3 < len(chunks):
            ch = chunks[step - 3]
            ys[ch["gi"]].append(stage_out(ch))

    mean_bd = jnp.where(same, 1.0 / RWKV_HEAD_DIM, 0.0).astype(BF16)
    ones_bd = jnp.where(same, 1.0, 0.0).astype(BF16)
    for gi, ln in enumerate(lanes):
        s_ref[gi] = states[gi]
        y = jnp.concatenate(ys[gi], axis=0)
        mean = _split_dot_lhs(y, mean_bd, 2)
        d = y - mean
        var = _split_dot_lhs(d * d, mean_bd, 1)
        y = d * lax.rsqrt(var + RWKV_GN_EPS) * lnw_ref[:, ln] + lnb_ref[:, ln]
        r = r_ref[0, :, ln].astype(F32)
        k = k_ref[0, :, ln].astype(F32)
        v = v_ref[0, :, ln].astype(F32)
        bonus = _split_dot_lhs(r * k * rk_ref[:, ln], ones_bd, 1)
        y = y + bonus * v
        o_ref[0, :, ln] = (y * g_ref[0, :, ln].astype(F32)).astype(o_ref.dtype)


def _rwkv_scan(r, lw, k, v, kn, bb, g, r_k, ln_w, ln_b, *, tt):
    b, s, width = r.shape
    gp = RWKV_GROUPS_PER_STEP
    w = gp * RWKV_GROUP
    assert width % w == 0
    tile = pl.BlockSpec((1, tt, w), lambda bi, gi, si: (bi, si, gi))
    vec = pl.BlockSpec((1, w), lambda bi, gi, si: (0, gi))
    return pl.pallas_call(
        functools.partial(_rwkv_scan_kernel, tt=tt, gp=gp),
        grid=(b, width // w, s // tt),
        in_specs=[tile] * 7 + [vec] * 3,
        out_specs=tile,
        out_shape=jax.ShapeDtypeStruct((b, s, width), BF16),
        scratch_shapes=[pltpu.VMEM((gp, RWKV_GROUP, RWKV_GROUP), F32)],
        compiler_params=_params("parallel", "parallel", "arbitrary"),
        name="rwkv_scan",
    )(r, lw, k, v, kn, bb, g, r_k, ln_w, ln_b)


def _mix_out_kernel(ya_ref, yb_ref, yc_ref, gate_ref, x_ref, pa_ref, pb_ref, pc_ref, wo_ref, gain_ref, o_ref):
    d = x_ref.shape[1]
    m = gate_ref[:, 0:d].astype(F32) * _dot(ya_ref[...], pa_ref[...])
    m = m + gate_ref[:, d:2 * d].astype(F32) * _dot(yb_ref[...], pb_ref[...])
    m = m + gate_ref[:, 2 * d:3 * d].astype(F32) * _dot(yc_ref[...], pc_ref[...])
    o = _dot(m.astype(BF16), wo_ref[...])
    ms = jnp.mean(o * o, axis=-1, keepdims=True)
    o_ref[...] = x_ref[...] + o * lax.rsqrt(ms + RMS_EPS) * gain_ref[...]


def _mix_out(ya, yb, yc, proj2, x2, pa, pb, pc, wo, gain, *, gate_col_block, tm):
    t, d = x2.shape
    const = lambda shape: pl.BlockSpec(shape, lambda m: (0, 0), pipeline_mode=pl.Buffered(1))
    return pl.pallas_call(
        _mix_out_kernel,
        grid=(t // tm,),
        in_specs=[
            pl.BlockSpec((tm, ya.shape[1]), lambda m: (m, 0)),
            pl.BlockSpec((tm, yb.shape[1]), lambda m: (m, 0)),
            pl.BlockSpec((tm, yc.shape[1]), lambda m: (m, 0)),
            pl.BlockSpec((tm, 3 * d), lambda m: (m, gate_col_block)),
            pl.BlockSpec((tm, d), lambda m: (m, 0)),
            const(pa.shape), const(pb.shape), const(pc.shape), const(wo.shape),
            const((1, d)),
        ],
        out_specs=pl.BlockSpec((tm, d), lambda m: (m, 0)),
        out_shape=jax.ShapeDtypeStruct((t, d), F32),
        compiler_params=_params("parallel"),
        name="mix_out",
    )(ya, yb, yc, proj2, x2, pa, pb, pc, wo, gain.reshape(1, d))


def _mlp_kernel(x_ref, g1_ref, wu_ref, wd_ref, g2_ref, o_ref, u_ref, acc_ref):
    f = pl.program_id(1)

    @pl.when(f == 0)
    def _():
        x = x_ref[...]
        ms = jnp.mean(x * x, axis=-1, keepdims=True)
        u_ref[...] = (x * lax.rsqrt(ms + RMS_EPS) * g1_ref[...]).astype(BF16)
        acc_ref[...] = jnp.zeros_like(acc_ref)

    h = jnp.maximum(_dot(u_ref[...], wu_ref[...]), 0.0)
    acc_ref[...] += _dot((h * h).astype(BF16), wd_ref[...])

    @pl.when(f == pl.num_programs(1) - 1)
    def _():
        o = acc_ref[...]
        ms = jnp.mean(o * o, axis=-1, keepdims=True)
        o_ref[...] = x_ref[...] + o * lax.rsqrt(ms + RMS_EPS) * g2_ref[...]


def _mlp(x2, g1, wu, wd, g2, *, tm, tf):
    t, d = x2.shape
    ff = wu.shape[1]
    return pl.pallas_call(
        _mlp_kernel,
        grid=(t // tm, ff // tf),
        in_specs=[
            pl.BlockSpec((tm, d), lambda m, f: (m, 0)),
            pl.BlockSpec((1, d), lambda m, f: (0, 0)),
            pl.BlockSpec((d, tf), lambda m, f: (0, f)),
            pl.BlockSpec((tf, d), lambda m, f: (f, 0)),
            pl.BlockSpec((1, d), lambda m, f: (0, 0)),
        ],
        out_specs=pl.BlockSpec((tm, d), lambda m, f: (m, 0)),
        out_shape=jax.ShapeDtypeStruct((t, d), F32),
        scratch_shapes=[pltpu.VMEM((tm, d), BF16), pltpu.VMEM((tm, d), F32)],
        compiler_params=_params("parallel", "arbitrary"),
        name="mlp",
    )(x2, g1.reshape(1, d), wu, wd, g2.reshape(1, d))


def _pad_rows(w, rows):
    return jnp.pad(w, ((0, rows - w.shape[0]), (0, 0)))


def _pad_cols(w, cols):
    return jnp.pad(w, ((0, 0), (0, cols - w.shape[1])))


def _prepare_w_in(w_in, d, sbw, fxw, rww):
    fx_heads = fxw // ATTN_HEAD_DIM
    o_f = 3 * sbw + 3 * fxw
    o_rw = o_f + fx_heads
    o_wd = o_rw + 3 * rww
    o_ad = o_wd + DECAY_LORA
    o_gd = o_ad + AAA_LORA
    o_gate = o_gd + GATE_LORA
    assert w_in.shape[2] == o_gate + 3 * d
    w_t = jnp.transpose(w_in, (0, 2, 1))
    main = _repack_w_in(w_t, [(0, o_f), (o_rw, 3 * rww), (o_gate, 3 * d)], tn=TILE["repack_n"])

    def rows(lo, hi, padded):
        return jnp.pad(w_t[:, lo:hi], ((0, 0), (0, padded - (hi - lo)), (0, 0)))

    small = jnp.concatenate([rows(o_wd, o_ad, LANE), rows(o_ad, o_gd, LANE), rows(o_gd, o_gate, 2 * LANE),
                             rows(o_f, o_rw, LANE)], axis=1)
    return main, small


def _mixer_layer(x, layer, norm_pre, norm_post, w_main_t, w_small_t, b_forget, mu, w0, w_up, a0, a_up, g_up,
                 k_k, k_a, r_k, ln_w, ln_b, pa, pb, pc, w_out):
    bsz, s, d = x.shape
    t = bsz * s
    sbw = pa.shape[0]
    fxw = pb.shape[0]
    rww = pc.shape[0]
    sb_heads = sbw // ATTN_HEAD_DIM
    fx_heads = fxw // ATTN_HEAD_DIM

    gate_col = 3 * sbw + 3 * fxw + 3 * rww
    mu_rkv = mu[None, :3 * rww]
    mu_small = jnp.concatenate([
        _pad_cols(mu[None, 3 * rww:3 * rww + DECAY_LORA], LANE),
        _pad_cols(mu[None, 3 * rww + DECAY_LORA:3 * rww + DECAY_LORA + AAA_LORA], LANE),
        _pad_cols(mu[None, 3 * rww + DECAY_LORA + AAA_LORA:], 2 * LANE),
        jnp.zeros((1, LANE), F32)], axis=1)

    x2 = x.reshape(t, d)
    proj_m = min(TILE["proj_m"], t)
    proj = _norm_proj(x2, norm_pre, w_main_t, layer, BF16, tm=proj_m, tn=TILE["proj_n"],
                      sigmoid_from_col=gate_col)
    small = _norm_proj(x2, norm_pre, w_small_t, layer, F32, tm=proj_m, tn=SMALL_WIDTH)
    proj3 = proj.reshape(bsz, s, -1)
    small3 = small.reshape(bsz, s, SMALL_WIDTH)

    tq = min(TILE["attn_q"], s)
    qa = 0
    ya = _sb_attention(proj3, q_col=qa, k_col=qa + sb_heads, v_col=qa + 2 * sb_heads, heads=sb_heads, tq=tq)

    cum = _forget_cumsum(small3, _pad_cols(b_forget[None, :], LANE))
    c_hs = cum[:, :, :fx_heads].transpose(0, 2, 1)
    qb = 3 * sb_heads
    yb = _fox_attention(proj3, cum, c_hs.reshape(bsz, fx_heads, s // tq, tq),
                        q_col=qb, k_col=qb + fx_heads, v_col=qb + 2 * fx_heads, heads=fx_heads, tq=tq)

    rkv_col = (3 * sbw + 3 * fxw) // (3 * rww)
    assert rkv_col * 3 * rww == 3 * sbw + 3 * fxw
    row = lambda p: p.reshape(1, rww)
    r, lw, k, v, kn, bb, g = _rwkv_prep(
        proj3, small3, mu_rkv, mu_small, row(w0), _pad_rows(w_up, LANE).astype(BF16), row(a0),
        _pad_rows(a_up, LANE).astype(BF16), _pad_rows(g_up, 2 * LANE).astype(BF16), row(k_k), row(k_a),
        rkv_col=rkv_col, width=rww, tm=min(TILE["rwkv_prep_m"], s))
    yc = _rwkv_scan(r, lw, k, v, kn, bb, g, row(r_k), row(ln_w), row(ln_b), tt=min(TILE["rwkv_scan_t"], s))

    assert gate_col % (3 * d) == 0
    out = _mix_out(ya.reshape(t, sbw), yb.reshape(t, fxw), yc.reshape(t, rww), proj, x2,
                   pa.astype(BF16), pb.astype(BF16), pc.astype(BF16), w_out.astype(BF16), norm_post,
                   gate_col_block=gate_col // (3 * d), tm=min(TILE["mix_m"], t))
    return out.reshape(bsz, s, d)


def kernel(x, norm_mix_pre, norm_mix_post, norm_mlp_pre, norm_mlp_post, w_in, b_forget, rwkv_mu, rwkv_w0,
           rwkv_w_up, rwkv_a0, rwkv_a_up, rwkv_g_up, rwkv_k_k, rwkv_k_a, rwkv_r_k, rwkv_ln_w, rwkv_ln_b,
           w_branch_a, w_branch_b, w_branch_c, w_out, w_mlp_up, w_mlp_down):
    bsz, s, d = x.shape
    t = bsz * s
    w_main_t, w_small_t = _prepare_w_in(w_in, d, w_branch_a.shape[1], w_branch_b.shape[1], w_branch_c.shape[1])
    for l in range(w_in.shape[0]):
        x = _mixer_layer(x, l, norm_mix_pre[l], norm_mix_post[l], w_main_t, w_small_t, b_forget[l], rwkv_mu[l],
                         rwkv_w0[l], rwkv_w_up[l], rwkv_a0[l], rwkv_a_up[l], rwkv_g_up[l], rwkv_k_k[l],
                         rwkv_k_a[l], rwkv_r_k[l], rwkv_ln_w[l], rwkv_ln_b[l], w_branch_a[l], w_branch_b[l],
                         w_branch_c[l], w_out[l])
        x = _mlp(x.reshape(t, d), norm_mlp_pre[l], w_mlp_up[l].astype(BF16), w_mlp_down[l].astype(BF16),
                 norm_mlp_post[l], tm=min(TILE["mlp_m"], t), tf=TILE["mlp_f"]).reshape(bsz, s, d)
    return x
```

```python
import functools

import jax
import jax.numpy as jnp
from jax import lax
from jax.experimental import pallas as pl
from jax.experimental.pallas import tpu as pltpu

F32 = jnp.float32
BF16 = jnp.bfloat16

RMS_EPS = 1e-6
RWKV_GN_EPS = 64e-5
DECAY_SCALE = 0.6065306597126334
KEY_NORM_FLOOR = 1e-12
ATTN_HEAD_DIM = 128
SB_HEADS_PER_STEP = 2
FOX_HEADS_PER_STEP = 1
RWKV_HEAD_DIM = 64
RWKV_CHUNK = 64
RWKV_GROUP = 256
RWKV_GROUPS_PER_STEP = 2
DECAY_LORA = 64
AAA_LORA = 64
GATE_LORA = 160
LANE = 128
SUBLANE = 8
MASK_VALUE = -1e30
LOG2E = 1.4426950408889634
SUFFIX_BLOCK = 256
PROJ_SUB_TILE = 512
VMEM_LIMIT = 56 * 1024 * 1024

TILE = dict(
    repack_n=512,
    proj_m=1024, proj_n=2048,
    attn_q=512,
    rwkv_prep_m=256,
    rwkv_scan_t=512,
    mix_m=256,
    mlp_m=512, mlp_f=1024,
)

SMALL_WD = 0
SMALL_AD = 128
SMALL_GD = 256
SMALL_F = 512
SMALL_WIDTH = 640


def _dot(a, b):
    return jnp.dot(a, b, preferred_element_type=F32)


def _dot_nt(a, b):
    return lax.dot_general(a, b, (((1,), (1,)), ((), ())), preferred_element_type=F32)


def _dot_tn(a, b):
    return lax.dot_general(a, b, (((0,), (0,)), ((), ())), preferred_element_type=F32)


def _bf16_terms(x, parts):
    terms = []
    rem = x
    for p in range(parts):
        h = rem.astype(BF16)
        terms.append(h)
        if p + 1 < parts:
            rem = rem - h.astype(F32)
    return terms


def _split_dot_rhs(m_bf16, x, parts):
    if parts == 1:
        return _dot(m_bf16, x.astype(BF16))
    return _dot(jnp.concatenate([m_bf16] * parts, axis=1), jnp.concatenate(_bf16_terms(x, parts), axis=0))


def _split_dot_lhs(x, m_bf16, parts):
    if parts == 1:
        return _dot(x.astype(BF16), m_bf16)
    return _dot(jnp.concatenate(_bf16_terms(x, parts), axis=1), jnp.concatenate([m_bf16] * parts, axis=0))


def _log_sigmoid(z):
    return jnp.minimum(z, 0.0) - jnp.log(1.0 + jnp.exp(-jnp.abs(z)))


def _sigmoid(z):
    return 1.0 / (1.0 + jnp.exp(-z))


def _params(*sem):
    return pltpu.CompilerParams(dimension_semantics=sem, vmem_limit_bytes=VMEM_LIMIT)


def _repack_kernel(w_ref, tail_ref, o_ref, *, tile_shifts, tail_tile, tn):
    j = pl.program_id(1)
    for shift in sorted(set(tile_shifts)):
        @pl.when(functools.reduce(jnp.logical_or, [j == t for t, s in enumerate(tile_shifts)
                                                   if s == shift and t != tail_tile]))
        def _(shift=shift):
            o_ref[0] = w_ref[0, shift:shift + tn, :].astype(o_ref.dtype)

    @pl.when(j == tail_tile)
    def _():
        shift = tile_shifts[tail_tile]
        o_ref[0] = tail_ref[0, shift:shift + tn, :].astype(o_ref.dtype)


def _repack_w_in(w_t, segments, *, tn):
    layers, n_in, k = w_t.shape
    n_out = sum(width for _, width in segments)
    assert all(width % tn == 0 for _, width in segments)
    win = tn + SUBLANE
    tile_starts = [start + off for start, width in segments for off in range(0, width, tn)]
    tile_shifts = [s % SUBLANE for s in tile_starts]
    tail_tile = len(tile_starts) - 1
    tail_from = tile_starts[tail_tile] - tile_shifts[tail_tile]
    assert all(s - s % SUBLANE + win <= n_in for s in tile_starts[:tail_tile])
    tail = jnp.pad(w_t[:, tail_from:], ((0, 0), (0, tail_from + win - n_in), (0, 0)))
    last_window = (n_in - win) // SUBLANE * SUBLANE

    def window_start(j):
        row = j * tn
        packed = 0
        start = jnp.int32(0)
        for seg_start, width in segments:
            start = jnp.where(row >= packed, seg_start - seg_start % SUBLANE - packed, start)
            packed += width
        return jnp.minimum(row + start, last_window)

    return pl.pallas_call(
        functools.partial(_repack_kernel, tile_shifts=tile_shifts, tail_tile=tail_tile, tn=tn),
        grid=(layers, n_out // tn),
        in_specs=[pl.BlockSpec((pl.Element(1), pl.Element(win), pl.Element(k)),
                               lambda l, j: (l, pl.multiple_of(window_start(j), SUBLANE), 0)),
                  pl.BlockSpec((1, win, k), lambda l, j: (l, 0, 0))],
        out_specs=pl.BlockSpec((1, tn, k), lambda l, j: (l, j, 0)),
        out_shape=jax.ShapeDtypeStruct((layers, n_out, k), BF16),
        compiler_params=_params("parallel", "parallel"),
        name="repack_w_in",
    )(w_t, tail)


def _norm_proj_kernel(x_ref, g_ref, w_ref, o_ref, u_ref, *, sigmoid_from, sub):
    n = pl.program_id(1)

    @pl.when(n == 0)
    def _():
        x = x_ref[...]
        ms = jnp.mean(x * x, axis=-1, keepdims=True)
        u_ref[...] = (x * lax.rsqrt(ms + RMS_EPS) * g_ref[...]).astype(BF16)

    def columns(apply_sigmoid):
        for c0 in range(0, o_ref.shape[1], sub):
            acc = _dot_nt(u_ref[...], w_ref[c0:c0 + sub, :].astype(BF16))
            if apply_sigmoid:
                acc = _sigmoid(acc)
            o_ref[:, c0:c0 + sub] = acc.astype(o_ref.dtype)

    if sigmoid_from is None:
        columns(False)
    else:
        pl.when(n < sigmoid_from)(lambda: columns(False))
        pl.when(n >= sigmoid_from)(lambda: columns(True))


def _norm_proj(x2, gain, w_t, layer, out_dtype, *, tm, tn, sigmoid_from_col=None):
    t, d = x2.shape
    n = w_t.shape[1]
    assert t % tm == 0 and n % tn == 0
    sig = None if sigmoid_from_col is None else sigmoid_from_col // tn
    if sigmoid_from_col is not None:
        assert sigmoid_from_col % tn == 0
    return pl.pallas_call(
        functools.partial(_norm_proj_kernel, sigmoid_from=sig,
                          sub=PROJ_SUB_TILE if tn % PROJ_SUB_TILE == 0 else tn),
        grid=(t // tm, n // tn),
        in_specs=[
            pl.BlockSpec((tm, d), lambda m, j: (m, 0)),
            pl.BlockSpec((1, d), lambda m, j: (0, 0)),
            pl.BlockSpec((None, tn, d), lambda m, j: (layer, j, 0)),
        ],
        out_specs=pl.BlockSpec((tm, tn), lambda m, j: (m, j)),
        out_shape=jax.ShapeDtypeStruct((t, n), out_dtype),
        scratch_shapes=[pltpu.VMEM((tm, d), BF16)],
        compiler_params=_params("parallel", "arbitrary"),
        name="norm_proj",
    )(x2, gain.reshape(1, d), w_t)


def _forget_cumsum_kernel(f_ref, b_ref, c_ref, *, cb):
    s = f_ref.shape[1]
    row = lax.broadcasted_iota(jnp.int32, (cb, cb), 0)
    col = lax.broadcasted_iota(jnp.int32, (cb, cb), 1)
    tri = jnp.where(col <= row, 1.0, 0.0).astype(BF16)
    carry = jnp.zeros((1, LANE), F32)
    for j in range(s // cb):
        lf = _log_sigmoid(f_ref[0, j * cb:(j + 1) * cb, :] + b_ref[...])
        c = _split_dot_rhs(tri, lf, 3) + carry
        c_ref[0, j * cb:(j + 1) * cb, :] = c
        carry = c[cb - 1:cb, :]


def _forget_cumsum(small3, b_forget_pad):
    b, s, _ = small3.shape
    cb = min(256, s)
    return pl.pallas_call(
        functools.partial(_forget_cumsum_kernel, cb=cb),
        grid=(b,),
        in_specs=[
            pl.BlockSpec((1, s, LANE), lambda i: (i, 0, SMALL_F // LANE)),
            pl.BlockSpec((1, LANE), lambda i: (0, 0)),
        ],
        out_specs=pl.BlockSpec((1, s, LANE), lambda i: (i, 0, 0)),
        out_shape=jax.ShapeDtypeStruct((b, s, LANE), F32),
        compiler_params=_params("parallel"),
        name="forget_cumsum",
    )(small3, b_forget_pad)


def _sb_kernel(q_ref, k_ref, v_ref, o_ref, *, tq, scale, hp):
    i = pl.program_id(2)
    dh = ATTN_HEAD_DIM
    sub = min(SUFFIX_BLOCK, tq)
    nsub = tq // sub
    qs = [(q_ref[0, :, h * dh:(h + 1) * dh].astype(F32) * (scale * LOG2E)).astype(BF16) for h in range(hp)]
    srow = lax.broadcasted_iota(jnp.int32, (sub, sub), 0)
    scol = lax.broadcasted_iota(jnp.int32, (sub, sub), 1)
    upper = jnp.where(srow > scol, 1.0, 0.0).astype(BF16)

    def weights(h, j, carry, diagonal):
        start = pl.multiple_of(j * tq, tq)
        kb = k_ref[0, pl.ds(start, tq), h * dh:(h + 1) * dh]
        vb = v_ref[0, pl.ds(start, tq), h * dh:(h + 1) * dh]
        w = _dot_nt(qs[h], kb)
        log_beta = jnp.minimum(w, 0.0) - jnp.log(1.0 + jnp.exp2(-jnp.abs(w))) * LOG2E
        log_rest = log_beta - w
        if diagonal:
            strict = (lax.broadcasted_iota(jnp.int32, (tq, tq), 1)
                      < lax.broadcasted_iota(jnp.int32, (tq, tq), 0))
            log_rest = jnp.where(strict, log_rest, 0.0)
        pieces = [None] * nsub
        for sb in reversed(range(nsub)):
            x = log_rest[:, sb * sub:(sb + 1) * sub]
            pieces[sb] = _split_dot_lhs(x, upper, 2) + carry
            carry = carry + jnp.sum(x, axis=1, keepdims=True)
        between = pieces[0] if nsub == 1 else jnp.concatenate(pieces, axis=1)
        a = jnp.exp2(log_beta + between)
        if diagonal:
            a = jnp.where(strict, a, 0.0)
        return carry, a.astype(BF16), vb

    def blocks(js, state, diagonal):
        state = list(state)
        for j in js:
            for h in range(hp):
                carry, acc = state[h]
                carry, a, vb = weights(h, j, carry, diagonal)
                state[h] = (carry, acc + _dot(a, vb))
        return tuple(state)

    state = tuple((jnp.zeros((tq, 1), F32), jnp.zeros((tq, dh), F32)) for _ in range(hp))
    state = blocks([i], state, True)
    odd = lax.rem(i, 2)
    state = lax.cond(odd == 1, lambda st: blocks([i - 1], st, False), lambda st: st, state)

    def body(jj, st):
        j = i - 1 - odd - 2 * jj
        return blocks([j, j - 1], st, False)

    state = lax.fori_loop(0, i // 2, body, state)
    o_ref[0] = jnp.concatenate([acc.astype(o_ref.dtype) for _, acc in state], axis=1)


def _sb_attention(proj3, *, q_col, k_col, v_col, heads, tq):
    b, s, _ = proj3.shape
    hp = SB_HEADS_PER_STEP
    dh = hp * ATTN_HEAD_DIM
    assert heads % hp == 0 and q_col % hp == 0 and k_col % hp == 0 and v_col % hp == 0
    return pl.pallas_call(
        functools.partial(_sb_kernel, tq=tq, scale=ATTN_HEAD_DIM ** -0.5, hp=hp),
        grid=(b, heads // hp, s // tq),
        in_specs=[
            pl.BlockSpec((1, tq, dh), lambda bi, h, i: (bi, i, q_col // hp + h)),
            pl.BlockSpec((1, s, dh), lambda bi, h, i: (bi, 0, k_col // hp + h)),
            pl.BlockSpec((1, s, dh), lambda bi, h, i: (bi, 0, v_col // hp + h)),
        ],
        out_specs=pl.BlockSpec((1, tq, dh), lambda bi, h, i: (bi, i, h)),
        out_shape=jax.ShapeDtypeStruct((b, s, heads * ATTN_HEAD_DIM), BF16),
        compiler_params=_params("parallel", "parallel", "arbitrary"),
        name="sb_attention",
    )(proj3, proj3, proj3)


def _fox_kernel(q_ref, k_ref, v_ref, cum_ref, ck_ref, o_ref, *, tq, scale, hp):
    hg = pl.program_id(1)
    i = pl.program_id(2)
    dh = ATTN_HEAD_DIM
    lane = lax.broadcasted_iota(jnp.int32, (tq, LANE), 1)
    qs, cqs = [], []
    for h in range(hp):
        qs.append((q_ref[0, :, h * dh:(h + 1) * dh].astype(F32) * (scale * LOG2E)).astype(BF16))
        cqs.append(jnp.sum(jnp.where(lane == hg * hp + h, cum_ref[0], 0.0), axis=1, keepdims=True) * LOG2E)

    def scores(h, j):
        start = pl.multiple_of(j * tq, tq)
        kb = k_ref[0, pl.ds(start, tq), h * dh:(h + 1) * dh]
        ck = ck_ref[0, h, pl.ds(j, 1), :] * LOG2E
        return _dot_nt(qs[h], kb) + (cqs[h] - ck), v_ref[0, pl.ds(start, tq), h * dh:(h + 1) * dh]

    causal = (lax.broadcasted_iota(jnp.int32, (tq, tq), 1)
              <= lax.broadcasted_iota(jnp.int32, (tq, tq), 0))

    def first(h):
        z, vb = scores(h, i)
        z = jnp.where(causal, z, MASK_VALUE)
        m = jnp.max(z, axis=1, keepdims=True)
        p = jnp.exp2(z - m)
        return m, jnp.sum(p, axis=1, keepdims=True), _dot(p.astype(BF16), vb)

    def step(h, js, m, l, acc):
        zv = [scores(h, j) for j in js]
        block_max = functools.reduce(jnp.maximum, [jnp.max(z, axis=1, keepdims=True) for z, _ in zv])
        m_new = jnp.maximum(m, block_max)
        alpha = jnp.exp2(m - m_new)
        ps = [jnp.exp2(z - m_new) for z, _ in zv]
        l = alpha * l + functools.reduce(jnp.add, [jnp.sum(p, axis=1, keepdims=True) for p in ps])
        acc = alpha * acc + functools.reduce(jnp.add, [_dot(p.astype(BF16), vb) for p, (_, vb) in zip(ps, zv)])
        return m_new, l, acc

    def steps(js, state):
        return tuple(step(h, js, *state[h]) for h in range(hp))

    state = tuple(first(h) for h in range(hp))
    odd = lax.rem(i, 2)
    state = lax.cond(odd == 1, lambda st: steps([i - 1], st), lambda st: st, state)

    def body(jj, st):
        j = i - 1 - odd - 2 * jj
        return steps([j, j - 1], st)

    state = lax.fori_loop(0, i // 2, body, state)
    o_ref[0] = jnp.concatenate([(acc / l).astype(o_ref.dtype) for _, l, acc in state], axis=1)


def _fox_attention(proj3, cum, ck, *, q_col, k_col, v_col, heads, tq):
    b, s, _ = proj3.shape
    hp = FOX_HEADS_PER_STEP
    dh = hp * ATTN_HEAD_DIM
    nb = s // tq
    assert heads % hp == 0 and q_col % hp == 0 and k_col % hp == 0 and v_col % hp == 0
    return pl.pallas_call(
        functools.partial(_fox_kernel, tq=tq, scale=ATTN_HEAD_DIM ** -0.5, hp=hp),
        grid=(b, heads // hp, nb),
        in_specs=[
            pl.BlockSpec((1, tq, dh), lambda bi, h, i: (bi, i, q_col // hp + h)),
            pl.BlockSpec((1, s, dh), lambda bi, h, i: (bi, 0, k_col // hp + h)),
            pl.BlockSpec((1, s, dh), lambda bi, h, i: (bi, 0, v_col // hp + h)),
            pl.BlockSpec((1, tq, LANE), lambda bi, h, i: (bi, i, 0)),
            pl.BlockSpec((1, hp, nb, tq), lambda bi, h, i: (bi, h, 0, 0)),
        ],
        out_specs=pl.BlockSpec((1, tq, dh), lambda bi, h, i: (bi, i, h)),
        out_shape=jax.ShapeDtypeStruct((b, s, heads * ATTN_HEAD_DIM), BF16),
        compiler_params=_params("parallel", "parallel", "arbitrary"),
        name="fox_attention",
    )(proj3, proj3, proj3, cum, ck)


def _same_head_mask(n):
    row = lax.broadcasted_iota(jnp.int32, (n, n), 0)
    col = lax.broadcasted_iota(jnp.int32, (n, n), 1)
    shift = RWKV_HEAD_DIM.bit_length() - 1
    return jnp.right_shift(row, shift) == jnp.right_shift(col, shift)


def _rwkv_prep_kernel(z_ref, zs_ref, mu_ref, mus_ref, w0_ref, wup_ref, a0_ref, aup_ref, gup_ref,
                      kk_ref, ka_ref,
                      r_out, lw_out, k_out, v_out, kn_out, b_out, g_out,
                      zc_ref, zsc_ref, *, tm, width):
    si = pl.program_id(1)

    @pl.when(si == 0)
    def _():
        zc_ref[...] = jnp.zeros_like(zc_ref)
        zsc_ref[...] = jnp.zeros_like(zsc_ref)

    z = z_ref[0].astype(F32)
    zs = zs_ref[0]
    first = lax.broadcasted_iota(jnp.int32, (tm, 1), 0) == 0

    def shifted(x, carry_ref):
        prev = jnp.where(first, carry_ref[...], pltpu.roll(x, 1, 0))
        carry_ref[...] = x[tm - 1:tm, :]
        return prev

    z = z + (shifted(z, zc_ref) - z) * mu_ref[...]
    zs = zs + (shifted(zs, zsc_ref) - zs) * mus_ref[...]

    r = z[:, :width]
    k = z[:, width:2 * width]
    v = z[:, 2 * width:]
    wd = zs[:, SMALL_WD:SMALL_WD + LANE]
    ad = zs[:, SMALL_AD:SMALL_AD + LANE]
    gd = zs[:, SMALL_GD:SMALL_GD + 2 * LANE]

    wl = w0_ref[...] + _dot(jnp.tanh(wd).astype(BF16), wup_ref[...])
    lw_out[0] = _sigmoid(wl) * (-DECAY_SCALE)
    a = _sigmoid(a0_ref[...] + _dot(ad.astype(BF16), aup_ref[...]))
    g_out[0] = _dot(_sigmoid(gd).astype(BF16), gup_ref[...]).astype(g_out.dtype)

    r_out[0] = r.astype(r_out.dtype)
    v_out[0] = v.astype(v_out.dtype)
    k_out[0] = (k * (1.0 + (a - 1.0) * ka_ref[...])).astype(k_out.dtype)

    kk_raw = k * kk_ref[...]
    ones_bd = jnp.where(_same_head_mask(RWKV_GROUP), 1.0, 0.0).astype(BF16)
    for gi in range(width // RWKV_GROUP):
        sl = slice(gi * RWKV_GROUP, (gi + 1) * RWKV_GROUP)
        x = kk_raw[:, sl]
        ss = _split_dot_lhs(x * x, ones_bd, 1)
        kn = x * lax.rsqrt(jnp.maximum(ss, KEY_NORM_FLOOR ** 2))
        kn_out[0, :, sl] = kn.astype(kn_out.dtype)
        b_out[0, :, sl] = (kn * a[:, sl]).astype(b_out.dtype)


def _rwkv_prep(proj3, small3, mu_rkv, mu_small, w0, w_up_p, a0, a_up_p, g_up_p, k_k, k_a, *, rkv_col, width, tm):
    b, s, _ = proj3.shape
    full = lambda shape: pl.BlockSpec(shape, lambda bi, si: (0,) * len(shape))
    tile = pl.BlockSpec((1, tm, width), lambda bi, si: (bi, si, 0))
    shp = lambda dt: jax.ShapeDtypeStruct((b, s, width), dt)
    return pl.pallas_call(
        functools.partial(_rwkv_prep_kernel, tm=tm, width=width),
        grid=(b, s // tm),
        in_specs=[
            pl.BlockSpec((1, tm, 3 * width), lambda bi, si: (bi, si, rkv_col)),
            pl.BlockSpec((1, tm, SMALL_WIDTH), lambda bi, si: (bi, si, 0)),
            full((1, 3 * width)), full((1, SMALL_WIDTH)),
            full((1, width)), full((LANE, width)),
            full((1, width)), full((LANE, width)),
            full((2 * LANE, width)),
            full((1, width)), full((1, width)),
        ],
        out_specs=[tile] * 7,
        out_shape=[shp(BF16), shp(F32), shp(BF16), shp(BF16), shp(BF16), shp(BF16), shp(BF16)],
        scratch_shapes=[pltpu.VMEM((1, 3 * width), F32), pltpu.VMEM((1, SMALL_WIDTH), F32)],
        compiler_params=_params("parallel", "arbitrary"),
        name="rwkv_prep",
    )(proj3, small3, mu_rkv, mu_small, w0, w_up_p, a0, a_up_p, g_up_p, k_k, k_a)


def _rwkv_scan_kernel(r_ref, lw_ref, k_ref, v_ref, kn_ref, b_ref, g_ref, rk_ref, lnw_ref, lnb_ref,
                      o_ref, s_ref, *, tt, gp):
    st = pl.program_id(2)

    @pl.when(st == 0)
    def _():
        s_ref[...] = jnp.zeros_like(s_ref)

    c = RWKV_CHUNK
    w = RWKV_GROUP
    heads = w // RWKV_HEAD_DIM
    assert heads * c == w
    same = _same_head_mask(w)
    row = lax.broadcasted_iota(jnp.int32, (w, w), 0)
    col = lax.broadcasted_iota(jnp.int32, (w, w), 1)
    strict = same & (col < row)
    incl = same & (col <= row)

    def tile(x):
        return jnp.concatenate([x] * heads, axis=0)

    def stack(x):
        return jnp.where(same, tile(x), 0.0).astype(BF16)

    def unstack(x):
        out = x[0:c]
        for h in range(1, heads):
            out = out + x[h * c:(h + 1) * c]
        return out

    nchunks = tt // c

    assert tt % w == 0
    tri = jnp.where(incl, 1.0, 0.0).astype(BF16)
    lanes = [slice(gi * w, (gi + 1) * w) for gi in range(gp)]
    cum_all = [jnp.concatenate([_split_dot_rhs(tri, lw_ref[0, r0:r0 + w, ln], 2) for r0 in range(0, tt, w)],
                               axis=0) for ln in lanes]

    def operands(ci, gi):
        sl = slice(ci * c, (ci + 1) * c)
        ln = lanes[gi]
        lw = lw_ref[0, sl, ln]
        r = r_ref[0, sl, ln].astype(F32)
        k = k_ref[0, sl, ln].astype(F32)
        v = v_ref[0, sl, ln].astype(F32)
        kn = kn_ref[0, sl, ln].astype(F32)
        bb = b_ref[0, sl, ln].astype(F32)
        cum = cum_all[gi][sl]
        last = cum[c - 1:c, :]
        e_neg = jnp.exp(-cum)
        e_end = jnp.exp(last - cum)
        return dict(
            gi=gi,
            r_s=stack(r * jnp.exp(cum)), a_s=stack(kn * jnp.exp(cum - lw)), v_s=stack(v),
            bk=jnp.concatenate([(bb * e_neg).astype(BF16), (k * e_neg).astype(BF16)], axis=0),
            b_end=stack(bb * e_end), k_end=stack(k * e_end), w_end=jnp.exp(last))

    first_half = lax.broadcasted_iota(jnp.int32, (w, 2 * c), 1) < c

    def block_diag(x, swapped, use_first, mask):
        half = jnp.where(first_half, x, swapped) if use_first else jnp.where(first_half, swapped, x)
        return jnp.where(mask, jnp.concatenate([half] * (w // (2 * c)), axis=1), 0.0)

    def interactions(ch):
        prod = _dot_nt(jnp.concatenate([ch["a_s"], ch["r_s"]], axis=0), ch["bk"])
        pa, pr = prod[:w], prod[w:]
        sa, sr = pltpu.roll(pa, c, 1), pltpu.roll(pr, c, 1)
        ch["pw"] = -block_diag(pa, sa, True, strict)
        ch["nrm"] = ch["pw"]
        ch["a_ak"] = block_diag(pa, sa, False, strict).astype(BF16)
        ch["a_rb"] = block_diag(pr, sr, True, incl).astype(BF16)
        ch["a_rk"] = block_diag(pr, sr, False, incl).astype(BF16)
        return ch

    def neumann_level(group):
        for ch in group:
            pwb = ch["pw"].astype(BF16)
            ch["pw"] = _dot(pwb, pwb)
        for ch in group:
            ch["nrm"] = ch["nrm"] + ch["pw"] + _dot(ch["nrm"].astype(BF16), ch["pw"].astype(BF16))

    chunks = [operands(ci, gi) for ci in range(nchunks) for gi in range(gp)]
    for ch in chunks:
        interactions(ch)
    for _ in range(c.bit_length() - 2):
        neumann_level(chunks)

    def stage_a(ch):
        ch["nrm"] = ch["nrm"].astype(BF16)
        ch["a_hat"] = (ch["a_s"].astype(F32) + _dot(ch["nrm"], ch["a_s"])).astype(BF16)
        ch["akv"] = _dot(ch["a_ak"], ch["v_s"])
        ch["q_s"] = _dot(ch["a_rk"], ch["v_s"])
        ch["kv"] = _dot_tn(ch["v_s"], ch["k_end"])

    def stage_b(ch):
        ch["v_hat"] = ch["akv"] + _dot(ch["nrm"], ch["akv"].astype(BF16))

    def stage_c(ch, state):
        ch["s0"] = state.astype(BF16)
        ch["u"] = (-(_dot_nt(ch["a_hat"], ch["s0"]) + ch["v_hat"])).astype(BF16)
        return ch["w_end"] * state + _dot_tn(ch["u"], ch["b_end"]) + ch["kv"]

    def stage_out(ch):
        return unstack(_dot_nt(ch["r_s"], ch["s0"]) + _dot(ch["a_rb"], ch["u"]) + ch["q_s"])

    states = [s_ref[gi] for gi in range(gp)]
    ys = [[] for _ in range(gp)]
    for step in range(len(chunks) + 3):
        if step < len(chunks):
            stage_a(chunks[step])
        if 0 <= step - 1 < len(chunks):
            stage_b(chunks[step - 1])
        if 0 <= step - 2 < len(chunks):
            ch = chunks[step - 2]
            states[ch["gi"]] = stage_c(ch, states[ch["gi"]])
        if 0 <= step - 3 < len(chunks):
            ch = chunks[step - 3]
            ys[ch["gi"]].append(stage_out(ch))

    mean_bd = jnp.where(same, 1.0 / RWKV_HEAD_DIM, 0.0).astype(BF16)
    ones_bd = jnp.where(same, 1.0, 0.0).astype(BF16)
    for gi, ln in enumerate(lanes):
        s_ref[gi] = states[gi]
        y = jnp.concatenate(ys[gi], axis=0)
        mean = _split_dot_lhs(y, mean_bd, 2)
        d = y - mean
        var = _split_dot_lhs(d * d, mean_bd, 1)
        y = d * lax.rsqrt(var + RWKV_GN_EPS) * lnw_ref[:, ln] + lnb_ref[:, ln]
        r = r_ref[0, :, ln].astype(F32)
        k = k_ref[0, :, ln].astype(F32)
        v = v_ref[0, :, ln].astype(F32)
        bonus = _split_dot_lhs(r * k * rk_ref[:, ln], ones_bd, 1)
        y = y + bonus * v
        o_ref[0, :, ln] = (y * g_ref[0, :, ln].astype(F32)).astype(o_ref.dtype)


def _rwkv_scan(r, lw, k, v, kn, bb, g, r_k, ln_w, ln_b, *, tt):
    b, s, width = r.shape
    gp = RWKV_GROUPS_PER_STEP
    w = gp * RWKV_GROUP
    assert width % w == 0
    tile = pl.BlockSpec((1, tt, w), lambda bi, gi, si: (bi, si, gi))
    vec = pl.BlockSpec((1, w), lambda bi, gi, si: (0, gi))
    return pl.pallas_call(
        functools.partial(_rwkv_scan_kernel, tt=tt, gp=gp),
        grid=(b, width // w, s // tt),
        in_specs=[tile] * 7 + [vec] * 3,
        out_specs=tile,
        out_shape=jax.ShapeDtypeStruct((b, s, width), BF16),
        scratch_shapes=[pltpu.VMEM((gp, RWKV_GROUP, RWKV_GROUP), F32)],
        compiler_params=_params("parallel", "parallel", "arbitrary"),
        name="rwkv_scan",
    )(r, lw, k, v, kn, bb, g, r_k, ln_w, ln_b)


def _mix_out_kernel(ya_ref, yb_ref, yc_ref, gate_ref, x_ref, pa_ref, pb_ref, pc_ref, wo_ref, gain_ref, o_ref):
    d = x_ref.shape[1]
    m = gate_ref[:, 0:d].astype(F32) * _dot(ya_ref[...], pa_ref[...])
    m = m + gate_ref[:, d:2 * d].astype(F32) * _dot(yb_ref[...], pb_ref[...])
    m = m + gate_ref[:, 2 * d:3 * d].astype(F32) * _dot(yc_ref[...], pc_ref[...])
    o = _dot(m.astype(BF16), wo_ref[...])
    ms = jnp.mean(o * o, axis=-1, keepdims=True)
    o_ref[...] = x_ref[...] + o * lax.rsqrt(ms + RMS_EPS) * gain_ref[...]


def _mix_out(ya, yb, yc, proj2, x2, pa, pb, pc, wo, layer, gain, *, gate_col_block, tm):
    t, d = x2.shape
    weight = lambda w: pl.BlockSpec((None,) + w.shape[1:], lambda m: (layer, 0, 0), pipeline_mode=pl.Buffered(1))
    return pl.pallas_call(
        _mix_out_kernel,
        grid=(t // tm,),
        in_specs=[
            pl.BlockSpec((tm, ya.shape[1]), lambda m: (m, 0)),
            pl.BlockSpec((tm, yb.shape[1]), lambda m: (m, 0)),
            pl.BlockSpec((tm, yc.shape[1]), lambda m: (m, 0)),
            pl.BlockSpec((tm, 3 * d), lambda m: (m, gate_col_block)),
            pl.BlockSpec((tm, d), lambda m: (m, 0)),
            weight(pa), weight(pb), weight(pc), weight(wo),
            pl.BlockSpec((1, d), lambda m: (0, 0)),
        ],
        out_specs=pl.BlockSpec((tm, d), lambda m: (m, 0)),
        out_shape=jax.ShapeDtypeStruct((t, d), F32),
        compiler_params=_params("parallel"),
        name="mix_out",
    )(ya, yb, yc, proj2, x2, pa, pb, pc, wo, gain.reshape(1, d))


def _mlp_kernel(x_ref, g1_ref, wu_ref, wd_ref, g2_ref, o_ref, u_ref, acc_ref):
    f = pl.program_id(1)

    @pl.when(f == 0)
    def _():
        x = x_ref[...]
        ms = jnp.mean(x * x, axis=-1, keepdims=True)
        u_ref[...] = (x * lax.rsqrt(ms + RMS_EPS) * g1_ref[...]).astype(BF16)
        acc_ref[...] = jnp.zeros_like(acc_ref)

    h = jnp.maximum(_dot(u_ref[...], wu_ref[...]), 0.0)
    acc_ref[...] += _dot((h * h).astype(BF16), wd_ref[...])

    @pl.when(f == pl.num_programs(1) - 1)
    def _():
        o = acc_ref[...]
        ms = jnp.mean(o * o, axis=-1, keepdims=True)
        o_ref[...] = x_ref[...] + o * lax.rsqrt(ms + RMS_EPS) * g2_ref[...]


def _mlp(x2, g1, wu, wd, layer, g2, *, tm, tf):
    t, d = x2.shape
    ff = wu.shape[2]
    return pl.pallas_call(
        _mlp_kernel,
        grid=(t // tm, ff // tf),
        in_specs=[
            pl.BlockSpec((tm, d), lambda m, f: (m, 0)),
            pl.BlockSpec((1, d), lambda m, f: (0, 0)),
            pl.BlockSpec((None, d, tf), lambda m, f: (layer, 0, f)),
            pl.BlockSpec((None, tf, d), lambda m, f: (layer, f, 0)),
            pl.BlockSpec((1, d), lambda m, f: (0, 0)),
        ],
        out_specs=pl.BlockSpec((tm, d), lambda m, f: (m, 0)),
        out_shape=jax.ShapeDtypeStruct((t, d), F32),
        scratch_shapes=[pltpu.VMEM((tm, d), BF16), pltpu.VMEM((tm, d), F32)],
        compiler_params=_params("parallel", "arbitrary"),
        name="mlp",
    )(x2, g1.reshape(1, d), wu, wd, g2.reshape(1, d))


def _pad_rows(w, rows):
    return jnp.pad(w, ((0, rows - w.shape[0]), (0, 0)))


def _pad_cols(w, cols):
    return jnp.pad(w, ((0, 0), (0, cols - w.shape[1])))


def _prepare_w_in(w_in, d, sbw, fxw, rww):
    fx_heads = fxw // ATTN_HEAD_DIM
    o_f = 3 * sbw + 3 * fxw
    o_rw = o_f + fx_heads
    o_wd = o_rw + 3 * rww
    o_ad = o_wd + DECAY_LORA
    o_gd = o_ad + AAA_LORA
    o_gate = o_gd + GATE_LORA
    assert w_in.shape[2] == o_gate + 3 * d
    w_t = jnp.transpose(w_in, (0, 2, 1))
    main = _repack_w_in(w_t, [(0, o_f), (o_rw, 3 * rww), (o_gate, 3 * d)], tn=TILE["repack_n"])

    def rows(lo, hi, padded):
        return jnp.pad(w_t[:, lo:hi], ((0, 0), (0, padded - (hi - lo)), (0, 0)))

    small = jnp.concatenate([rows(o_wd, o_ad, LANE), rows(o_ad, o_gd, LANE), rows(o_gd, o_gate, 2 * LANE),
                             rows(o_f, o_rw, LANE)], axis=1)
    return main, small


def _mixer_layer(x, layer, norm_pre, norm_post, w_main_t, w_small_t, b_forget, mu, w0, w_up, a0, a_up, g_up,
                 k_k, k_a, r_k, ln_w, ln_b, pa, pb, pc, w_out):
    bsz, s, d = x.shape
    t = bsz * s
    sbw = pa.shape[1]
    fxw = pb.shape[1]
    rww = pc.shape[1]
    sb_heads = sbw // ATTN_HEAD_DIM
    fx_heads = fxw // ATTN_HEAD_DIM

    gate_col = 3 * sbw + 3 * fxw + 3 * rww
    mu_rkv = mu[None, :3 * rww]
    mu_small = jnp.concatenate([
        _pad_cols(mu[None, 3 * rww:3 * rww + DECAY_LORA], LANE),
        _pad_cols(mu[None, 3 * rww + DECAY_LORA:3 * rww + DECAY_LORA + AAA_LORA], LANE),
        _pad_cols(mu[None, 3 * rww + DECAY_LORA + AAA_LORA:], 2 * LANE),
        jnp.zeros((1, LANE), F32)], axis=1)

    x2 = x.reshape(t, d)
    proj_m = min(TILE["proj_m"], t)
    proj = _norm_proj(x2, norm_pre, w_main_t, layer, BF16, tm=proj_m, tn=TILE["proj_n"],
                      sigmoid_from_col=gate_col)
    small = _norm_proj(x2, norm_pre, w_small_t, layer, F32, tm=proj_m, tn=SMALL_WIDTH)
    proj3 = proj.reshape(bsz, s, -1)
    small3 = small.reshape(bsz, s, SMALL_WIDTH)

    tq = min(TILE["attn_q"], s)
    qa = 0
    ya = _sb_attention(proj3, q_col=qa, k_col=qa + sb_heads, v_col=qa + 2 * sb_heads, heads=sb_heads, tq=tq)

    cum = _forget_cumsum(small3, _pad_cols(b_forget[None, :], LANE))
    c_hs = cum[:, :, :fx_heads].transpose(0, 2, 1)
    qb = 3 * sb_heads
    yb = _fox_attention(proj3, cum, c_hs.reshape(bsz, fx_heads, s // tq, tq),
                        q_col=qb, k_col=qb + fx_heads, v_col=qb + 2 * fx_heads, heads=fx_heads, tq=tq)

    rkv_col = (3 * sbw + 3 * fxw) // (3 * rww)
    assert rkv_col * 3 * rww == 3 * sbw + 3 * fxw
    row = lambda p: p.reshape(1, rww)
    r, lw, k, v, kn, bb, g = _rwkv_prep(
        proj3, small3, mu_rkv, mu_small, row(w0), _pad_rows(w_up, LANE).astype(BF16), row(a0),
        _pad_rows(a_up, LANE).astype(BF16), _pad_rows(g_up, 2 * LANE).astype(BF16), row(k_k), row(k_a),
        rkv_col=rkv_col, width=rww, tm=min(TILE["rwkv_prep_m"], s))
    yc = _rwkv_scan(r, lw, k, v, kn, bb, g, row(r_k), row(ln_w), row(ln_b), tt=min(TILE["rwkv_scan_t"], s))

    assert gate_col % (3 * d) == 0
    out = _mix_out(ya.reshape(t, sbw), yb.reshape(t, fxw), yc.reshape(t, rww), proj, x2,
                   pa, pb, pc, w_out, layer, norm_post,
                   gate_col_block=gate_col // (3 * d), tm=min(TILE["mix_m"], t))
    return out.reshape(bsz, s, d)


def kernel(x, norm_mix_pre, norm_mix_post, norm_mlp_pre, norm_mlp_post, w_in, b_forget, rwkv_mu, rwkv_w0,
           rwkv_w_up, rwkv_a0, rwkv_a_up, rwkv_g_up, rwkv_k_k, rwkv_k_a, rwkv_r_k, rwkv_ln_w, rwkv_ln_b,
           w_branch_a, w_branch_b, w_branch_c, w_out, w_mlp_up, w_mlp_down):
    bsz, s, d = x.shape
    t = bsz * s
    w_main_t, w_small_t = _prepare_w_in(w_in, d, w_branch_a.shape[1], w_branch_b.shape[1], w_branch_c.shape[1])
    pa, pb, pc, wo = (w.astype(BF16) for w in (w_branch_a, w_branch_b, w_branch_c, w_out))
    wu, wd = w_mlp_up.astype(BF16), w_mlp_down.astype(BF16)
    for l in range(w_in.shape[0]):
        x = _mixer_layer(x, l, norm_mix_pre[l], norm_mix_post[l], w_main_t, w_small_t, b_forget[l], rwkv_mu[l],
                         rwkv_w0[l], rwkv_w_up[l], rwkv_a0[l], rwkv_a_up[l], rwkv_g_up[l], rwkv_k_k[l],
                         rwkv_k_a[l], rwkv_r_k[l], rwkv_ln_w[l], rwkv_ln_b[l], pa, pb, pc, wo)
        x = _mlp(x.reshape(t, d), norm_mlp_pre[l], wu, wd, l, norm_mlp_post[l],
                 tm=min(TILE["mlp_m"], t), tf=TILE["mlp_f"]).reshape(bsz, s, d)
    return x
```

```python
import functools

import jax
import jax.numpy as jnp
from jax import lax
from jax.experimental import pallas as pl
from jax.experimental.pallas import tpu as pltpu

F32 = jnp.float32
BF16 = jnp.bfloat16

RMS_EPS = 1e-6
RWKV_GN_EPS = 64e-5
DECAY_SCALE = 0.6065306597126334
KEY_NORM_FLOOR = 1e-12
ATTN_HEAD_DIM = 128
SB_HEADS_PER_STEP = 4
FOX_HEADS_PER_STEP = 1
RWKV_HEAD_DIM = 64
RWKV_CHUNK = 64
RWKV_GROUP = 256
RWKV_GROUPS_PER_STEP = 2
DECAY_LORA = 64
AAA_LORA = 64
GATE_LORA = 160
LANE = 128
SUBLANE = 8
MASK_VALUE = -1e30
LOG2E = 1.4426950408889634
SUFFIX_BLOCK = 256
PROJ_SUB_TILE = 512
VMEM_LIMIT = 56 * 1024 * 1024

TILE = dict(
    repack_n=512,
    proj_m=1024, proj_n=2048,
    attn_q=512,
    rwkv_prep_m=256,
    rwkv_scan_t=512,
    mix_m=256,
    mlp_m=512, mlp_f=1024,
)

SMALL_WD = 0
SMALL_AD = 128
SMALL_GD = 256
SMALL_F = 512
SMALL_WIDTH = 640


def _dot(a, b):
    return jnp.dot(a, b, preferred_element_type=F32)


def _dot_nt(a, b):
    return lax.dot_general(a, b, (((1,), (1,)), ((), ())), preferred_element_type=F32)


def _dot_tn(a, b):
    return lax.dot_general(a, b, (((0,), (0,)), ((), ())), preferred_element_type=F32)


def _bf16_terms(x, parts):
    terms = []
    rem = x
    for p in range(parts):
        h = rem.astype(BF16)
        terms.append(h)
        if p + 1 < parts:
            rem = rem - h.astype(F32)
    return terms


def _split_dot_rhs(m_bf16, x, parts):
    if parts == 1:
        return _dot(m_bf16, x.astype(BF16))
    return _dot(jnp.concatenate([m_bf16] * parts, axis=1), jnp.concatenate(_bf16_terms(x, parts), axis=0))


def _split_dot_lhs(x, m_bf16, parts):
    if parts == 1:
        return _dot(x.astype(BF16), m_bf16)
    return _dot(jnp.concatenate(_bf16_terms(x, parts), axis=1), jnp.concatenate([m_bf16] * parts, axis=0))


def _log_sigmoid(z):
    return jnp.minimum(z, 0.0) - jnp.log(1.0 + jnp.exp(-jnp.abs(z)))


def _sigmoid(z):
    return 0.5 * jnp.tanh(0.5 * z) + 0.5


def _params(*sem):
    return pltpu.CompilerParams(dimension_semantics=sem, vmem_limit_bytes=VMEM_LIMIT)


def _repack_kernel(w_ref, tail_ref, o_ref, *, tile_shifts, tail_tile, tn):
    j = pl.program_id(1)
    for shift in sorted(set(tile_shifts)):
        @pl.when(functools.reduce(jnp.logical_or, [j == t for t, s in enumerate(tile_shifts)
                                                   if s == shift and t != tail_tile]))
        def _(shift=shift):
            o_ref[0] = w_ref[0, shift:shift + tn, :].astype(o_ref.dtype)

    @pl.when(j == tail_tile)
    def _():
        shift = tile_shifts[tail_tile]
        o_ref[0] = tail_ref[0, shift:shift + tn, :].astype(o_ref.dtype)


def _repack_w_in(w_t, segments, *, tn):
    layers, n_in, k = w_t.shape
    n_out = sum(width for _, width in segments)
    assert all(width % tn == 0 for _, width in segments)
    win = tn + SUBLANE
    tile_starts = [start + off for start, width in segments for off in range(0, width, tn)]
    tile_shifts = [s % SUBLANE for s in tile_starts]
    tail_tile = len(tile_starts) - 1
    tail_from = tile_starts[tail_tile] - tile_shifts[tail_tile]
    assert all(s - s % SUBLANE + win <= n_in for s in tile_starts[:tail_tile])
    tail = jnp.pad(w_t[:, tail_from:], ((0, 0), (0, tail_from + win - n_in), (0, 0)))
    last_window = (n_in - win) // SUBLANE * SUBLANE

    def window_start(j):
        row = j * tn
        packed = 0
        start = jnp.int32(0)
        for seg_start, width in segments:
            start = jnp.where(row >= packed, seg_start - seg_start % SUBLANE - packed, start)
            packed += width
        return jnp.minimum(row + start, last_window)

    return pl.pallas_call(
        functools.partial(_repack_kernel, tile_shifts=tile_shifts, tail_tile=tail_tile, tn=tn),
        grid=(layers, n_out // tn),
        in_specs=[pl.BlockSpec((pl.Element(1), pl.Element(win), pl.Element(k)),
                               lambda l, j: (l, pl.multiple_of(window_start(j), SUBLANE), 0)),
                  pl.BlockSpec((1, win, k), lambda l, j: (l, 0, 0))],
        out_specs=pl.BlockSpec((1, tn, k), lambda l, j: (l, j, 0)),
        out_shape=jax.ShapeDtypeStruct((layers, n_out, k), BF16),
        compiler_params=_params("parallel", "parallel"),
        name="repack_w_in",
    )(w_t, tail)


def _norm_proj_kernel(x_ref, g_ref, w_ref, o_ref, u_ref, *, sigmoid_from, sub):
    n = pl.program_id(1)

    @pl.when(n == 0)
    def _():
        x = x_ref[...]
        ms = jnp.mean(x * x, axis=-1, keepdims=True)
        u_ref[...] = (x * lax.rsqrt(ms + RMS_EPS) * g_ref[...]).astype(BF16)

    def columns(apply_sigmoid):
        for c0 in range(0, o_ref.shape[1], sub):
            acc = _dot_nt(u_ref[...], w_ref[c0:c0 + sub, :].astype(BF16))
            if apply_sigmoid:
                acc = _sigmoid(acc)
            o_ref[:, c0:c0 + sub] = acc.astype(o_ref.dtype)

    if sigmoid_from is None:
        columns(False)
    else:
        pl.when(n < sigmoid_from)(lambda: columns(False))
        pl.when(n >= sigmoid_from)(lambda: columns(True))


def _norm_proj(x2, gain, w_t, layer, out_dtype, *, tm, tn, sigmoid_from_col=None):
    t, d = x2.shape
    n = w_t.shape[1]
    assert t % tm == 0 and n % tn == 0
    sig = None if sigmoid_from_col is None else sigmoid_from_col // tn
    if sigmoid_from_col is not None:
        assert sigmoid_from_col % tn == 0
    return pl.pallas_call(
        functools.partial(_norm_proj_kernel, sigmoid_from=sig,
                          sub=PROJ_SUB_TILE if tn % PROJ_SUB_TILE == 0 else tn),
        grid=(t // tm, n // tn),
        in_specs=[
            pl.BlockSpec((tm, d), lambda m, j: (m, 0)),
            pl.BlockSpec((1, d), lambda m, j: (0, 0)),
            pl.BlockSpec((None, tn, d), lambda m, j: (layer, j, 0)),
        ],
        out_specs=pl.BlockSpec((tm, tn), lambda m, j: (m, j)),
        out_shape=jax.ShapeDtypeStruct((t, n), out_dtype),
        scratch_shapes=[pltpu.VMEM((tm, d), BF16)],
        compiler_params=_params("parallel", "arbitrary"),
        name="norm_proj",
    )(x2, gain.reshape(1, d), w_t)


def _forget_cumsum_kernel(f_ref, b_ref, c_ref, *, cb):
    s = f_ref.shape[1]
    row = lax.broadcasted_iota(jnp.int32, (cb, cb), 0)
    col = lax.broadcasted_iota(jnp.int32, (cb, cb), 1)
    tri = jnp.where(col <= row, 1.0, 0.0).astype(BF16)
    carry = jnp.zeros((1, LANE), F32)
    for j in range(s // cb):
        lf = _log_sigmoid(f_ref[0, j * cb:(j + 1) * cb, :] + b_ref[...])
        c = _split_dot_rhs(tri, lf, 3) + carry
        c_ref[0, j * cb:(j + 1) * cb, :] = c
        carry = c[cb - 1:cb, :]


def _forget_cumsum(small3, b_forget_pad):
    b, s, _ = small3.shape
    cb = min(256, s)
    return pl.pallas_call(
        functools.partial(_forget_cumsum_kernel, cb=cb),
        grid=(b,),
        in_specs=[
            pl.BlockSpec((1, s, LANE), lambda i: (i, 0, SMALL_F // LANE)),
            pl.BlockSpec((1, LANE), lambda i: (0, 0)),
        ],
        out_specs=pl.BlockSpec((1, s, LANE), lambda i: (i, 0, 0)),
        out_shape=jax.ShapeDtypeStruct((b, s, LANE), F32),
        compiler_params=_params("parallel"),
        name="forget_cumsum",
    )(small3, b_forget_pad)


def _sb_kernel(q_ref, k_ref, v_ref, o_ref, *, tq, scale, hp):
    i = pl.program_id(2)
    dh = ATTN_HEAD_DIM
    sub = min(SUFFIX_BLOCK, tq)
    nsub = tq // sub
    qs = [(q_ref[0, :, h * dh:(h + 1) * dh].astype(F32) * (scale * LOG2E)).astype(BF16) for h in range(hp)]
    srow = lax.broadcasted_iota(jnp.int32, (sub, sub), 0)
    scol = lax.broadcasted_iota(jnp.int32, (sub, sub), 1)
    upper = jnp.where(srow > scol, 1.0, 0.0).astype(BF16)

    def weights(h, j, carry, diagonal):
        start = pl.multiple_of(j * tq, tq)
        kb = k_ref[0, pl.ds(start, tq), h * dh:(h + 1) * dh]
        vb = v_ref[0, pl.ds(start, tq), h * dh:(h + 1) * dh]
        w = _dot_nt(qs[h], kb)
        log_beta = jnp.minimum(w, 0.0) - jnp.log(1.0 + jnp.exp2(-jnp.abs(w))) * LOG2E
        log_rest = log_beta - w
        if diagonal:
            strict = (lax.broadcasted_iota(jnp.int32, (tq, tq), 1)
                      < lax.broadcasted_iota(jnp.int32, (tq, tq), 0))
            log_rest = jnp.where(strict, log_rest, 0.0)
        pieces = [None] * nsub
        for sb in reversed(range(nsub)):
            x = log_rest[:, sb * sub:(sb + 1) * sub]
            pieces[sb] = _split_dot_lhs(x, upper, 2) + carry
            carry = carry + jnp.sum(x, axis=1, keepdims=True)
        between = pieces[0] if nsub == 1 else jnp.concatenate(pieces, axis=1)
        a = jnp.exp2(log_beta + between)
        if diagonal:
            a = jnp.where(strict, a, 0.0)
        return carry, a.astype(BF16), vb

    def blocks(js, state, diagonal):
        state = list(state)
        for j in js:
            for h in range(hp):
                carry, acc = state[h]
                carry, a, vb = weights(h, j, carry, diagonal)
                state[h] = (carry, acc + _dot(a, vb))
        return tuple(state)

    state = tuple((jnp.zeros((tq, 1), F32), jnp.zeros((tq, dh), F32)) for _ in range(hp))
    state = blocks([i], state, True)
    odd = lax.rem(i, 2)
    state = lax.cond(odd == 1, lambda st: blocks([i - 1], st, False), lambda st: st, state)

    def body(jj, st):
        j = i - 1 - odd - 2 * jj
        return blocks([j, j - 1], st, False)

    state = lax.fori_loop(0, i // 2, body, state)
    o_ref[0] = jnp.concatenate([acc.astype(o_ref.dtype) for _, acc in state], axis=1)


def _sb_attention(proj3, *, q_col, k_col, v_col, heads, tq):
    b, s, _ = proj3.shape
    hp = SB_HEADS_PER_STEP
    dh = hp * ATTN_HEAD_DIM
    assert heads % hp == 0 and q_col % hp == 0 and k_col % hp == 0 and v_col % hp == 0
    return pl.pallas_call(
        functools.partial(_sb_kernel, tq=tq, scale=ATTN_HEAD_DIM ** -0.5, hp=hp),
        grid=(b, heads // hp, s // tq),
        in_specs=[
            pl.BlockSpec((1, tq, dh), lambda bi, h, i: (bi, i, q_col // hp + h)),
            pl.BlockSpec((1, s, dh), lambda bi, h, i: (bi, 0, k_col // hp + h)),
            pl.BlockSpec((1, s, dh), lambda bi, h, i: (bi, 0, v_col // hp + h)),
        ],
        out_specs=pl.BlockSpec((1, tq, dh), lambda bi, h, i: (bi, i, h)),
        out_shape=jax.ShapeDtypeStruct((b, s, heads * ATTN_HEAD_DIM), BF16),
        compiler_params=_params("parallel", "parallel", "arbitrary"),
        name="sb_attention",
    )(proj3, proj3, proj3)


def _fox_kernel(q_ref, k_ref, v_ref, cum_ref, ck_ref, o_ref, *, tq, scale, hp):
    hg = pl.program_id(1)
    i = pl.program_id(2)
    dh = ATTN_HEAD_DIM
    lane = lax.broadcasted_iota(jnp.int32, (tq, LANE), 1)
    qs, cqs = [], []
    for h in range(hp):
        qs.append((q_ref[0, :, h * dh:(h + 1) * dh].astype(F32) * (scale * LOG2E)).astype(BF16))
        cqs.append(jnp.sum(jnp.where(lane == hg * hp + h, cum_ref[0], 0.0), axis=1, keepdims=True) * LOG2E)

    def scores(h, j):
        start = pl.multiple_of(j * tq, tq)
        kb = k_ref[0, pl.ds(start, tq), h * dh:(h + 1) * dh]
        ck = ck_ref[0, h, pl.ds(j, 1), :] * LOG2E
        return _dot_nt(qs[h], kb) + (cqs[h] - ck), v_ref[0, pl.ds(start, tq), h * dh:(h + 1) * dh]

    causal = (lax.broadcasted_iota(jnp.int32, (tq, tq), 1)
              <= lax.broadcasted_iota(jnp.int32, (tq, tq), 0))

    def first(h):
        z, vb = scores(h, i)
        z = jnp.where(causal, z, MASK_VALUE)
        m = jnp.max(z, axis=1, keepdims=True)
        p = jnp.exp2(z - m)
        return m, jnp.sum(p, axis=1, keepdims=True), _dot(p.astype(BF16), vb)

    def step(h, js, m, l, acc):
        zv = [scores(h, j) for j in js]
        block_max = functools.reduce(jnp.maximum, [jnp.max(z, axis=1, keepdims=True) for z, _ in zv])
        m_new = jnp.maximum(m, block_max)
        alpha = jnp.exp2(m - m_new)
        ps = [jnp.exp2(z - m_new) for z, _ in zv]
        l = alpha * l + functools.reduce(jnp.add, [jnp.sum(p, axis=1, keepdims=True) for p in ps])
        acc = alpha * acc + functools.reduce(jnp.add, [_dot(p.astype(BF16), vb) for p, (_, vb) in zip(ps, zv)])
        return m_new, l, acc

    def steps(js, state):
        return tuple(step(h, js, *state[h]) for h in range(hp))

    state = tuple(first(h) for h in range(hp))
    odd = lax.rem(i, 2)
    state = lax.cond(odd == 1, lambda st: steps([i - 1], st), lambda st: st, state)

    def body(jj, st):
        j = i - 1 - odd - 2 * jj
        return steps([j, j - 1], st)

    state = lax.fori_loop(0, i // 2, body, state)
    o_ref[0] = jnp.concatenate([(acc / l).astype(o_ref.dtype) for _, l, acc in state], axis=1)


def _fox_attention(proj3, cum, ck, *, q_col, k_col, v_col, heads, tq):
    b, s, _ = proj3.shape
    hp = FOX_HEADS_PER_STEP
    dh = hp * ATTN_HEAD_DIM
    nb = s // tq
    assert heads % hp == 0 and q_col % hp == 0 and k_col % hp == 0 and v_col % hp == 0
    return pl.pallas_call(
        functools.partial(_fox_kernel, tq=tq, scale=ATTN_HEAD_DIM ** -0.5, hp=hp),
        grid=(b, heads // hp, nb),
        in_specs=[
            pl.BlockSpec((1, tq, dh), lambda bi, h, i: (bi, i, q_col // hp + h)),
            pl.BlockSpec((1, s, dh), lambda bi, h, i: (bi, 0, k_col // hp + h)),
            pl.BlockSpec((1, s, dh), lambda bi, h, i: (bi, 0, v_col // hp + h)),
            pl.BlockSpec((1, tq, LANE), lambda bi, h, i: (bi, i, 0)),
            pl.BlockSpec((1, hp, nb, tq), lambda bi, h, i: (bi, h, 0, 0)),
        ],
        out_specs=pl.BlockSpec((1, tq, dh), lambda bi, h, i: (bi, i, h)),
        out_shape=jax.ShapeDtypeStruct((b, s, heads * ATTN_HEAD_DIM), BF16),
        compiler_params=_params("parallel", "parallel", "arbitrary"),
        name="fox_attention",
    )(proj3, proj3, proj3, cum, ck)


def _same_head_mask(n):
    row = lax.broadcasted_iota(jnp.int32, (n, n), 0)
    col = lax.broadcasted_iota(jnp.int32, (n, n), 1)
    shift = RWKV_HEAD_DIM.bit_length() - 1
    return jnp.right_shift(row, shift) == jnp.right_shift(col, shift)


def _rwkv_prep_kernel(z_ref, zs_ref, mu_ref, mus_ref, w0_ref, wup_ref, a0_ref, aup_ref, gup_ref,
                      kk_ref, ka_ref,
                      r_out, lw_out, k_out, v_out, kn_out, b_out, g_out,
                      zc_ref, zsc_ref, *, tm, width):
    si = pl.program_id(1)

    @pl.when(si == 0)
    def _():
        zc_ref[...] = jnp.zeros_like(zc_ref)
        zsc_ref[...] = jnp.zeros_like(zsc_ref)

    z = z_ref[0].astype(F32)
    zs = zs_ref[0]
    first = lax.broadcasted_iota(jnp.int32, (tm, 1), 0) == 0

    def shifted(x, carry_ref):
        prev = jnp.where(first, carry_ref[...], pltpu.roll(x, 1, 0))
        carry_ref[...] = x[tm - 1:tm, :]
        return prev

    z = z + (shifted(z, zc_ref) - z) * mu_ref[...]
    zs = zs + (shifted(zs, zsc_ref) - zs) * mus_ref[...]

    r = z[:, :width]
    k = z[:, width:2 * width]
    v = z[:, 2 * width:]
    wd = zs[:, SMALL_WD:SMALL_WD + LANE]
    ad = zs[:, SMALL_AD:SMALL_AD + LANE]
    gd = zs[:, SMALL_GD:SMALL_GD + 2 * LANE]

    wl = w0_ref[...] + _dot(jnp.tanh(wd).astype(BF16), wup_ref[...])
    lw_out[0] = _sigmoid(wl) * (-DECAY_SCALE)
    a = _sigmoid(a0_ref[...] + _dot(ad.astype(BF16), aup_ref[...]))
    g_out[0] = _dot(_sigmoid(gd).astype(BF16), gup_ref[...]).astype(g_out.dtype)

    r_out[0] = r.astype(r_out.dtype)
    v_out[0] = v.astype(v_out.dtype)
    k_out[0] = (k * (1.0 + (a - 1.0) * ka_ref[...])).astype(k_out.dtype)

    kk_raw = k * kk_ref[...]
    ones_bd = jnp.where(_same_head_mask(RWKV_GROUP), 1.0, 0.0).astype(BF16)
    for gi in range(width // RWKV_GROUP):
        sl = slice(gi * RWKV_GROUP, (gi + 1) * RWKV_GROUP)
        x = kk_raw[:, sl]
        ss = _split_dot_lhs(x * x, ones_bd, 1)
        kn = x * lax.rsqrt(jnp.maximum(ss, KEY_NORM_FLOOR ** 2))
        kn_out[0, :, sl] = kn.astype(kn_out.dtype)
        b_out[0, :, sl] = (kn * a[:, sl]).astype(b_out.dtype)


def _rwkv_prep(proj3, small3, mu_rkv, mu_small, w0, w_up_p, a0, a_up_p, g_up_p, k_k, k_a, *, rkv_col, width, tm):
    b, s, _ = proj3.shape
    full = lambda shape: pl.BlockSpec(shape, lambda bi, si: (0,) * len(shape))
    tile = pl.BlockSpec((1, tm, width), lambda bi, si: (bi, si, 0))
    shp = lambda dt: jax.ShapeDtypeStruct((b, s, width), dt)
    return pl.pallas_call(
        functools.partial(_rwkv_prep_kernel, tm=tm, width=width),
        grid=(b, s // tm),
        in_specs=[
            pl.BlockSpec((1, tm, 3 * width), lambda bi, si: (bi, si, rkv_col)),
            pl.BlockSpec((1, tm, SMALL_WIDTH), lambda bi, si: (bi, si, 0)),
            full((1, 3 * width)), full((1, SMALL_WIDTH)),
            full((1, width)), full((LANE, width)),
            full((1, width)), full((LANE, width)),
            full((2 * LANE, width)),
            full((1, width)), full((1, width)),
        ],
        out_specs=[tile] * 7,
        out_shape=[shp(BF16), shp(F32), shp(BF16), shp(BF16), shp(BF16), shp(BF16), shp(BF16)],
        scratch_shapes=[pltpu.VMEM((1, 3 * width), F32), pltpu.VMEM((1, SMALL_WIDTH), F32)],
        compiler_params=_params("parallel", "arbitrary"),
        name="rwkv_prep",
    )(proj3, small3, mu_rkv, mu_small, w0, w_up_p, a0, a_up_p, g_up_p, k_k, k_a)


def _rwkv_scan_kernel(r_ref, lw_ref, k_ref, v_ref, kn_ref, b_ref, g_ref, rk_ref, lnw_ref, lnb_ref,
                      o_ref, s_ref, *, tt, gp):
    st = pl.program_id(2)

    @pl.when(st == 0)
    def _():
        s_ref[...] = jnp.zeros_like(s_ref)

    c = RWKV_CHUNK
    w = RWKV_GROUP
    heads = w // RWKV_HEAD_DIM
    assert heads * c == w
    same = _same_head_mask(w)
    row = lax.broadcasted_iota(jnp.int32, (w, w), 0)
    col = lax.broadcasted_iota(jnp.int32, (w, w), 1)
    strict = same & (col < row)
    incl = same & (col <= row)

    def tile(x):
        return jnp.concatenate([x] * heads, axis=0)

    def stack(x):
        return jnp.where(same, tile(x), 0.0).astype(BF16)

    def unstack(x):
        out = x[0:c]
        for h in range(1, heads):
            out = out + x[h * c:(h + 1) * c]
        return out

    nchunks = tt // c

    assert tt % w == 0
    tri = jnp.where(incl, 1.0, 0.0).astype(BF16)
    lanes = [slice(gi * w, (gi + 1) * w) for gi in range(gp)]
    cum_all = [jnp.concatenate([_split_dot_rhs(tri, lw_ref[0, r0:r0 + w, ln], 2) for r0 in range(0, tt, w)],
                               axis=0) for ln in lanes]

    def operands(ci, gi):
        sl = slice(ci * c, (ci + 1) * c)
        ln = lanes[gi]
        lw = lw_ref[0, sl, ln]
        r = r_ref[0, sl, ln].astype(F32)
        k = k_ref[0, sl, ln].astype(F32)
        v = v_ref[0, sl, ln].astype(F32)
        kn = kn_ref[0, sl, ln].astype(F32)
        bb = b_ref[0, sl, ln].astype(F32)
        cum = cum_all[gi][sl]
        last = cum[c - 1:c, :]
        e_neg = jnp.exp(-cum)
        e_end = jnp.exp(last - cum)
        return dict(
            gi=gi,
            r_s=stack(r * jnp.exp(cum)), a_s=stack(kn * jnp.exp(cum - lw)), v_s=stack(v),
            bk=jnp.concatenate([(bb * e_neg).astype(BF16), (k * e_neg).astype(BF16)], axis=0),
            b_end=stack(bb * e_end), k_end=stack(k * e_end), w_end=jnp.exp(last))

    first_half = lax.broadcasted_iota(jnp.int32, (w, 2 * c), 1) < c

    def block_diag(x, swapped, use_first, mask):
        half = jnp.where(first_half, x, swapped) if use_first else jnp.where(first_half, swapped, x)
        return jnp.where(mask, jnp.concatenate([half] * (w // (2 * c)), axis=1), 0.0)

    def interactions(ch):
        prod = _dot_nt(jnp.concatenate([ch["a_s"], ch["r_s"]], axis=0), ch["bk"])
        pa, pr = prod[:w], prod[w:]
        sa, sr = pltpu.roll(pa, c, 1), pltpu.roll(pr, c, 1)
        ch["pw"] = -block_diag(pa, sa, True, strict)
        ch["nrm"] = ch["pw"]
        ch["a_ak"] = block_diag(pa, sa, False, strict).astype(BF16)
        ch["a_rb"] = block_diag(pr, sr, True, incl).astype(BF16)
        ch["a_rk"] = block_diag(pr, sr, False, incl).astype(BF16)
        return ch

    def neumann_level(group):
        for ch in group:
            pwb = ch["pw"].astype(BF16)
            ch["pw"] = _dot(pwb, pwb)
        for ch in group:
            ch["nrm"] = ch["nrm"] + ch["pw"] + _dot(ch["nrm"].astype(BF16), ch["pw"].astype(BF16))

    chunks = [operands(ci, gi) for ci in range(nchunks) for gi in range(gp)]
    for ch in chunks:
        interactions(ch)
    for _ in range(c.bit_length() - 2):
        neumann_level(chunks)

    def stage_a(ch):
        ch["nrm"] = ch["nrm"].astype(BF16)
        ch["a_hat"] = (ch["a_s"].astype(F32) + _dot(ch["nrm"], ch["a_s"])).astype(BF16)
        ch["akv"] = _dot(ch["a_ak"], ch["v_s"])
        ch["q_s"] = _dot(ch["a_rk"], ch["v_s"])
        ch["kv"] = _dot_tn(ch["v_s"], ch["k_end"])

    def stage_b(ch):
        ch["v_hat"] = ch["akv"] + _dot(ch["nrm"], ch["akv"].astype(BF16))

    def stage_c(ch, state):
        ch["s0"] = state.astype(BF16)
        ch["u"] = (-(_dot_nt(ch["a_hat"], ch["s0"]) + ch["v_hat"])).astype(BF16)
        return ch["w_end"] * state + _dot_tn(ch["u"], ch["b_end"]) + ch["kv"]

    def stage_out(ch):
        return unstack(_dot_nt(ch["r_s"], ch["s0"]) + _dot(ch["a_rb"], ch["u"]) + ch["q_s"])

    states = [s_ref[gi] for gi in range(gp)]
    ys = [[] for _ in range(gp)]
    for step in range(len(chunks) + 3):
        if step < len(chunks):
            stage_a(chunks[step])
        if 0 <= step - 1 < len(chunks):
            stage_b(chunks[step - 1])
        if 0 <= step - 2 < len(chunks):
            ch = chunks[step - 2]
            states[ch["gi"]] = stage_c(ch, states[ch["gi"]])
        if 0 <= step - 3 < len(chunks):
            ch = chunks[step - 3]
            ys[ch["gi"]].append(stage_out(ch))

    mean_bd = jnp.where(same, 1.0 / RWKV_HEAD_DIM, 0.0).astype(BF16)
    ones_bd = jnp.where(same, 1.0, 0.0).astype(BF16)
    for gi, ln in enumerate(lanes):
        s_ref[gi] = states[gi]
        y = jnp.concatenate(ys[gi], axis=0)
        mean = _split_dot_lhs(y, mean_bd, 2)
        d = y - mean
        var = _split_dot_lhs(d * d, mean_bd, 1)
        y = d * lax.rsqrt(var + RWKV_GN_EPS) * lnw_ref[:, ln] + lnb_ref[:, ln]
        r = r_ref[0, :, ln].astype(F32)
        k = k_ref[0, :, ln].astype(F32)
        v = v_ref[0, :, ln].astype(F32)
        bonus = _split_dot_lhs(r * k * rk_ref[:, ln], ones_bd, 1)
        y = y + bonus * v
        o_ref[0, :, ln] = (y * g_ref[0, :, ln].astype(F32)).astype(o_ref.dtype)


def _rwkv_scan(r, lw, k, v, kn, bb, g, r_k, ln_w, ln_b, *, tt):
    b, s, width = r.shape
    gp = RWKV_GROUPS_PER_STEP
    w = gp * RWKV_GROUP
    assert width % w == 0
    tile = pl.BlockSpec((1, tt, w), lambda bi, gi, si: (bi, si, gi))
    vec = pl.BlockSpec((1, w), lambda bi, gi, si: (0, gi))
    return pl.pallas_call(
        functools.partial(_rwkv_scan_kernel, tt=tt, gp=gp),
        grid=(b, width // w, s // tt),
        in_specs=[tile] * 7 + [vec] * 3,
        out_specs=tile,
        out_shape=jax.ShapeDtypeStruct((b, s, width), BF16),
        scratch_shapes=[pltpu.VMEM((gp, RWKV_GROUP, RWKV_GROUP), F32)],
        compiler_params=_params("parallel", "parallel", "arbitrary"),
        name="rwkv_scan",
    )(r, lw, k, v, kn, bb, g, r_k, ln_w, ln_b)


def _mix_out_kernel(ya_ref, yb_ref, yc_ref, gate_ref, x_ref, pa_ref, pb_ref, pc_ref, wo_ref, gain_ref, o_ref):
    d = x_ref.shape[1]
    m = gate_ref[:, 0:d].astype(F32) * _dot(ya_ref[...], pa_ref[...])
    m = m + gate_ref[:, d:2 * d].astype(F32) * _dot(yb_ref[...], pb_ref[...])
    m = m + gate_ref[:, 2 * d:3 * d].astype(F32) * _dot(yc_ref[...], pc_ref[...])
    o = _dot(m.astype(BF16), wo_ref[...])
    ms = jnp.mean(o * o, axis=-1, keepdims=True)
    o_ref[...] = x_ref[...] + o * lax.rsqrt(ms + RMS_EPS) * gain_ref[...]


def _mix_out(ya, yb, yc, proj2, x2, pa, pb, pc, wo, layer, gain, *, gate_col_block, tm):
    t, d = x2.shape
    weight = lambda w: pl.BlockSpec((None,) + w.shape[1:], lambda m: (layer, 0, 0), pipeline_mode=pl.Buffered(1))
    return pl.pallas_call(
        _mix_out_kernel,
        grid=(t // tm,),
        in_specs=[
            pl.BlockSpec((tm, ya.shape[1]), lambda m: (m, 0)),
            pl.BlockSpec((tm, yb.shape[1]), lambda m: (m, 0)),
            pl.BlockSpec((tm, yc.shape[1]), lambda m: (m, 0)),
            pl.BlockSpec((tm, 3 * d), lambda m: (m, gate_col_block)),
            pl.BlockSpec((tm, d), lambda m: (m, 0)),
            weight(pa), weight(pb), weight(pc), weight(wo),
            pl.BlockSpec((1, d), lambda m: (0, 0)),
        ],
        out_specs=pl.BlockSpec((tm, d), lambda m: (m, 0)),
        out_shape=jax.ShapeDtypeStruct((t, d), F32),
        compiler_params=_params("parallel"),
        name="mix_out",
    )(ya, yb, yc, proj2, x2, pa, pb, pc, wo, gain.reshape(1, d))


def _mlp_kernel(x_ref, g1_ref, wu_ref, wd_ref, g2_ref, o_ref, u_ref, acc_ref):
    f = pl.program_id(1)

    @pl.when(f == 0)
    def _():
        x = x_ref[...]
        ms = jnp.mean(x * x, axis=-1, keepdims=True)
        u_ref[...] = (x * lax.rsqrt(ms + RMS_EPS) * g1_ref[...]).astype(BF16)
        acc_ref[...] = jnp.zeros_like(acc_ref)

    h = jnp.maximum(_dot(u_ref[...], wu_ref[...]), 0.0)
    acc_ref[...] += _dot((h * h).astype(BF16), wd_ref[...])

    @pl.when(f == pl.num_programs(1) - 1)
    def _():
        o = acc_ref[...]
        ms = jnp.mean(o * o, axis=-1, keepdims=True)
        o_ref[...] = x_ref[...] + o * lax.rsqrt(ms + RMS_EPS) * g2_ref[...]


def _mlp(x2, g1, wu, wd, layer, g2, *, tm, tf):
    t, d = x2.shape
    ff = wu.shape[2]
    return pl.pallas_call(
        _mlp_kernel,
        grid=(t // tm, ff // tf),
        in_specs=[
            pl.BlockSpec((tm, d), lambda m, f: (m, 0)),
            pl.BlockSpec((1, d), lambda m, f: (0, 0)),
            pl.BlockSpec((None, d, tf), lambda m, f: (layer, 0, f)),
            pl.BlockSpec((None, tf, d), lambda m, f: (layer, f, 0)),
            pl.BlockSpec((1, d), lambda m, f: (0, 0)),
        ],
        out_specs=pl.BlockSpec((tm, d), lambda m, f: (m, 0)),
        out_shape=jax.ShapeDtypeStruct((t, d), F32),
        scratch_shapes=[pltpu.VMEM((tm, d), BF16), pltpu.VMEM((tm, d), F32)],
        compiler_params=_params("parallel", "arbitrary"),
        name="mlp",
    )(x2, g1.reshape(1, d), wu, wd, g2.reshape(1, d))


def _pad_rows(w, rows):
    return jnp.pad(w, ((0, rows - w.shape[0]), (0, 0)))


def _pad_cols(w, cols):
    return jnp.pad(w, ((0, 0), (0, cols - w.shape[1])))


def _prepare_w_in(w_in, d, sbw, fxw, rww):
    fx_heads = fxw // ATTN_HEAD_DIM
    o_f = 3 * sbw + 3 * fxw
    o_rw = o_f + fx_heads
    o_wd = o_rw + 3 * rww
    o_ad = o_wd + DECAY_LORA
    o_gd = o_ad + AAA_LORA
    o_gate = o_gd + GATE_LORA
    assert w_in.shape[2] == o_gate + 3 * d
    w_t = jnp.transpose(w_in, (0, 2, 1))
    main = _repack_w_in(w_t, [(0, o_f), (o_rw, 3 * rww), (o_gate, 3 * d)], tn=TILE["repack_n"])

    def rows(lo, hi, padded):
        return jnp.pad(w_t[:, lo:hi], ((0, 0), (0, padded - (hi - lo)), (0, 0)))

    small = jnp.concatenate([rows(o_wd, o_ad, LANE), rows(o_ad, o_gd, LANE), rows(o_gd, o_gate, 2 * LANE),
                             rows(o_f, o_rw, LANE)], axis=1)
    return main, small


def _mixer_layer(x, layer, norm_pre, norm_post, w_main_t, w_small_t, b_forget, mu, w0, w_up, a0, a_up, g_up,
                 k_k, k_a, r_k, ln_w, ln_b, pa, pb, pc, w_out):
    bsz, s, d = x.shape
    t = bsz * s
    sbw = pa.shape[1]
    fxw = pb.shape[1]
    rww = pc.shape[1]
    sb_heads = sbw // ATTN_HEAD_DIM
    fx_heads = fxw // ATTN_HEAD_DIM

    gate_col = 3 * sbw + 3 * fxw + 3 * rww
    mu_rkv = mu[None, :3 * rww]
    mu_small = jnp.concatenate([
        _pad_cols(mu[None, 3 * rww:3 * rww + DECAY_LORA], LANE),
        _pad_cols(mu[None, 3 * rww + DECAY_LORA:3 * rww + DECAY_LORA + AAA_LORA], LANE),
        _pad_cols(mu[None, 3 * rww + DECAY_LORA + AAA_LORA:], 2 * LANE),
        jnp.zeros((1, LANE), F32)], axis=1)

    x2 = x.reshape(t, d)
    proj_m = min(TILE["proj_m"], t)
    proj = _norm_proj(x2, norm_pre, w_main_t, layer, BF16, tm=proj_m, tn=TILE["proj_n"],
                      sigmoid_from_col=gate_col)
    small = _norm_proj(x2, norm_pre, w_small_t, layer, F32, tm=proj_m, tn=SMALL_WIDTH)
    proj3 = proj.reshape(bsz, s, -1)
    small3 = small.reshape(bsz, s, SMALL_WIDTH)

    tq = min(TILE["attn_q"], s)
    qa = 0
    ya = _sb_attention(proj3, q_col=qa, k_col=qa + sb_heads, v_col=qa + 2 * sb_heads, heads=sb_heads, tq=tq)

    cum = _forget_cumsum(small3, _pad_cols(b_forget[None, :], LANE))
    c_hs = cum[:, :, :fx_heads].transpose(0, 2, 1)
    qb = 3 * sb_heads
    yb = _fox_attention(proj3, cum, c_hs.reshape(bsz, fx_heads, s // tq, tq),
                        q_col=qb, k_col=qb + fx_heads, v_col=qb + 2 * fx_heads, heads=fx_heads, tq=tq)

    rkv_col = (3 * sbw + 3 * fxw) // (3 * rww)
    assert rkv_col * 3 * rww == 3 * sbw + 3 * fxw
    row = lambda p: p.reshape(1, rww)
    r, lw, k, v, kn, bb, g = _rwkv_prep(
        proj3, small3, mu_rkv, mu_small, row(w0), _pad_rows(w_up, LANE).astype(BF16), row(a0),
        _pad_rows(a_up, LANE).astype(BF16), _pad_rows(g_up, 2 * LANE).astype(BF16), row(k_k), row(k_a),
        rkv_col=rkv_col, width=rww, tm=min(TILE["rwkv_prep_m"], s))
    yc = _rwkv_scan(r, lw, k, v, kn, bb, g, row(r_k), row(ln_w), row(ln_b), tt=min(TILE["rwkv_scan_t"], s))

    assert gate_col % (3 * d) == 0
    out = _mix_out(ya.reshape(t, sbw), yb.reshape(t, fxw), yc.reshape(t, rww), proj, x2,
                   pa, pb, pc, w_out, layer, norm_post,
                   gate_col_block=gate_col // (3 * d), tm=min(TILE["mix_m"], t))
    return out.reshape(bsz, s, d)


def kernel(x, norm_mix_pre, norm_mix_post, norm_mlp_pre, norm_mlp_post, w_in, b_forget, rwkv_mu, rwkv_w0,
           rwkv_w_up, rwkv_a0, rwkv_a_up, rwkv_g_up, rwkv_k_k, rwkv_k_a, rwkv_r_k, rwkv_ln_w, rwkv_ln_b,
           w_branch_a, w_branch_b, w_branch_c, w_out, w_mlp_up, w_mlp_down):
    bsz, s, d = x.shape
    t = bsz * s
    w_main_t, w_small_t = _prepare_w_in(w_in, d, w_branch_a.shape[1], w_branch_b.shape[1], w_branch_c.shape[1])
    pa, pb, pc, wo = (w.astype(BF16) for w in (w_branch_a, w_branch_b, w_branch_c, w_out))
    wu, wd = w_mlp_up.astype(BF16), w_mlp_down.astype(BF16)
    for l in range(w_in.shape[0]):
        x = _mixer_layer(x, l, norm_mix_pre[l], norm_mix_post[l], w_main_t, w_small_t, b_forget[l], rwkv_mu[l],
                         rwkv_w0[l], rwkv_w_up[l], rwkv_a0[l], rwkv_a_up[l], rwkv_g_up[l], rwkv_k_k[l],
                         rwkv_k_a[l], rwkv_r_k[l], rwkv_ln_w[l], rwkv_ln_b[l], pa, pb, pc, wo)
        x = _mlp(x.reshape(t, d), norm_mlp_pre[l], wu, wd, l, norm_mlp_post[l],
                 tm=min(TILE["mlp_m"], t), tf=TILE["mlp_f"]).reshape(bsz, s, d)
    return x
```

```python
import functools

import jax
import jax.numpy as jnp
from jax import lax
from jax.experimental import pallas as pl
from jax.experimental.pallas import tpu as pltpu

F32 = jnp.float32
BF16 = jnp.bfloat16

RMS_EPS = 1e-6
RWKV_GN_EPS = 64e-5
DECAY_SCALE = 0.6065306597126334
KEY_NORM_FLOOR = 1e-12
ATTN_HEAD_DIM = 128
SB_HEADS_PER_STEP = 4
FOX_HEADS_PER_STEP = 1
RWKV_HEAD_DIM = 64
RWKV_CHUNK = 64
RWKV_GROUP = 256
RWKV_GROUPS_PER_STEP = 2
DECAY_LORA = 64
AAA_LORA = 64
GATE_LORA = 160
LANE = 128
SUBLANE = 8
MASK_VALUE = -1e30
LOG2E = 1.4426950408889634
SUFFIX_BLOCK = 256
PROJ_SUB_TILE = 512
VMEM_LIMIT = 56 * 1024 * 1024

TILE = dict(
    repack_n=512,
    proj_m=1024, proj_n=2048,
    attn_q=512,
    rwkv_prep_m=512,
    rwkv_scan_t=512,
    mix_m=256,
    mlp_m=512, mlp_f=1024,
)

SMALL_WD = 0
SMALL_AD = 128
SMALL_GD = 256
SMALL_F = 512
SMALL_WIDTH = 640


def _dot(a, b):
    return jnp.dot(a, b, preferred_element_type=F32)


def _dot_nt(a, b):
    return lax.dot_general(a, b, (((1,), (1,)), ((), ())), preferred_element_type=F32)


def _dot_tn(a, b):
    return lax.dot_general(a, b, (((0,), (0,)), ((), ())), preferred_element_type=F32)


def _bf16_terms(x, parts):
    terms = []
    rem = x
    for p in range(parts):
        h = rem.astype(BF16)
        terms.append(h)
        if p + 1 < parts:
            rem = rem - h.astype(F32)
    return terms


def _split_dot_rhs(m_bf16, x, parts):
    if parts == 1:
        return _dot(m_bf16, x.astype(BF16))
    return _dot(jnp.concatenate([m_bf16] * parts, axis=1), jnp.concatenate(_bf16_terms(x, parts), axis=0))


def _split_dot_lhs(x, m_bf16, parts):
    if parts == 1:
        return _dot(x.astype(BF16), m_bf16)
    return _dot(jnp.concatenate(_bf16_terms(x, parts), axis=1), jnp.concatenate([m_bf16] * parts, axis=0))


def _log_sigmoid(z):
    return jnp.minimum(z, 0.0) - jnp.log(1.0 + jnp.exp(-jnp.abs(z)))


def _sigmoid(z):
    return 0.5 * jnp.tanh(0.5 * z) + 0.5


def _params(*sem):
    return pltpu.CompilerParams(dimension_semantics=sem, vmem_limit_bytes=VMEM_LIMIT)


def _repack_kernel(w_ref, tail_ref, o_ref, *, tile_shifts, tail_tile, tn):
    j = pl.program_id(1)
    for shift in sorted(set(tile_shifts)):
        @pl.when(functools.reduce(jnp.logical_or, [j == t for t, s in enumerate(tile_shifts)
                                                   if s == shift and t != tail_tile]))
        def _(shift=shift):
            o_ref[0] = w_ref[0, shift:shift + tn, :].astype(o_ref.dtype)

    @pl.when(j == tail_tile)
    def _():
        shift = tile_shifts[tail_tile]
        o_ref[0] = tail_ref[0, shift:shift + tn, :].astype(o_ref.dtype)


def _repack_w_in(w_t, segments, *, tn):
    layers, n_in, k = w_t.shape
    n_out = sum(width for _, width in segments)
    assert all(width % tn == 0 for _, width in segments)
    win = tn + SUBLANE
    tile_starts = [start + off for start, width in segments for off in range(0, width, tn)]
    tile_shifts = [s % SUBLANE for s in tile_starts]
    tail_tile = len(tile_starts) - 1
    tail_from = tile_starts[tail_tile] - tile_shifts[tail_tile]
    assert all(s - s % SUBLANE + win <= n_in for s in tile_starts[:tail_tile])
    tail = jnp.pad(w_t[:, tail_from:], ((0, 0), (0, tail_from + win - n_in), (0, 0)))
    last_window = (n_in - win) // SUBLANE * SUBLANE

    def window_start(j):
        row = j * tn
        packed = 0
        start = jnp.int32(0)
        for seg_start, width in segments:
            start = jnp.where(row >= packed, seg_start - seg_start % SUBLANE - packed, start)
            packed += width
        return jnp.minimum(row + start, last_window)

    return pl.pallas_call(
        functools.partial(_repack_kernel, tile_shifts=tile_shifts, tail_tile=tail_tile, tn=tn),
        grid=(layers, n_out // tn),
        in_specs=[pl.BlockSpec((pl.Element(1), pl.Element(win), pl.Element(k)),
                               lambda l, j: (l, pl.multiple_of(window_start(j), SUBLANE), 0)),
                  pl.BlockSpec((1, win, k), lambda l, j: (l, 0, 0))],
        out_specs=pl.BlockSpec((1, tn, k), lambda l, j: (l, j, 0)),
        out_shape=jax.ShapeDtypeStruct((layers, n_out, k), BF16),
        compiler_params=_params("parallel", "parallel"),
        name="repack_w_in",
    )(w_t, tail)


def _norm_proj_kernel(x_ref, g_ref, w_ref, o_ref, u_ref, *, sigmoid_from, sub):
    n = pl.program_id(1)

    @pl.when(n == 0)
    def _():
        x = x_ref[...]
        ms = jnp.mean(x * x, axis=-1, keepdims=True)
        u_ref[...] = (x * lax.rsqrt(ms + RMS_EPS) * g_ref[...]).astype(BF16)

    def columns(apply_sigmoid):
        for c0 in range(0, o_ref.shape[1], sub):
            acc = _dot_nt(u_ref[...], w_ref[c0:c0 + sub, :].astype(BF16))
            if apply_sigmoid:
                acc = _sigmoid(acc)
            o_ref[:, c0:c0 + sub] = acc.astype(o_ref.dtype)

    if sigmoid_from is None:
        columns(False)
    else:
        pl.when(n < sigmoid_from)(lambda: columns(False))
        pl.when(n >= sigmoid_from)(lambda: columns(True))


def _norm_proj(x2, gain, w_t, layer, out_dtype, *, tm, tn, sigmoid_from_col=None):
    t, d = x2.shape
    n = w_t.shape[1]
    assert t % tm == 0 and n % tn == 0
    sig = None if sigmoid_from_col is None else sigmoid_from_col // tn
    if sigmoid_from_col is not None:
        assert sigmoid_from_col % tn == 0
    return pl.pallas_call(
        functools.partial(_norm_proj_kernel, sigmoid_from=sig,
                          sub=PROJ_SUB_TILE if tn % PROJ_SUB_TILE == 0 else tn),
        grid=(t // tm, n // tn),
        in_specs=[
            pl.BlockSpec((tm, d), lambda m, j: (m, 0)),
            pl.BlockSpec((1, d), lambda m, j: (0, 0)),
            pl.BlockSpec((None, tn, d), lambda m, j: (layer, j, 0)),
        ],
        out_specs=pl.BlockSpec((tm, tn), lambda m, j: (m, j)),
        out_shape=jax.ShapeDtypeStruct((t, n), out_dtype),
        scratch_shapes=[pltpu.VMEM((tm, d), BF16)],
        compiler_params=_params("parallel", "arbitrary"),
        name="norm_proj",
    )(x2, gain.reshape(1, d), w_t)


def _forget_cumsum_kernel(f_ref, b_ref, c_ref, *, cb):
    s = f_ref.shape[1]
    row = lax.broadcasted_iota(jnp.int32, (cb, cb), 0)
    col = lax.broadcasted_iota(jnp.int32, (cb, cb), 1)
    tri = jnp.where(col <= row, 1.0, 0.0).astype(BF16)
    carry = jnp.zeros((1, LANE), F32)
    for j in range(s // cb):
        lf = _log_sigmoid(f_ref[0, j * cb:(j + 1) * cb, :] + b_ref[...])
        c = _split_dot_rhs(tri, lf, 3) + carry
        c_ref[0, j * cb:(j + 1) * cb, :] = c
        carry = c[cb - 1:cb, :]


def _forget_cumsum(small3, b_forget_pad):
    b, s, _ = small3.shape
    cb = min(256, s)
    return pl.pallas_call(
        functools.partial(_forget_cumsum_kernel, cb=cb),
        grid=(b,),
        in_specs=[
            pl.BlockSpec((1, s, LANE), lambda i: (i, 0, SMALL_F // LANE)),
            pl.BlockSpec((1, LANE), lambda i: (0, 0)),
        ],
        out_specs=pl.BlockSpec((1, s, LANE), lambda i: (i, 0, 0)),
        out_shape=jax.ShapeDtypeStruct((b, s, LANE), F32),
        compiler_params=_params("parallel"),
        name="forget_cumsum",
    )(small3, b_forget_pad)


def _sb_kernel(q_ref, k_ref, v_ref, o_ref, *, tq, scale, hp):
    i = pl.program_id(2)
    dh = ATTN_HEAD_DIM
    sub = min(SUFFIX_BLOCK, tq)
    nsub = tq // sub
    qs = [(q_ref[0, :, h * dh:(h + 1) * dh].astype(F32) * (scale * LOG2E)).astype(BF16) for h in range(hp)]
    srow = lax.broadcasted_iota(jnp.int32, (sub, sub), 0)
    scol = lax.broadcasted_iota(jnp.int32, (sub, sub), 1)
    upper = jnp.where(srow > scol, 1.0, 0.0).astype(BF16)

    def weights(h, j, carry, diagonal):
        start = pl.multiple_of(j * tq, tq)
        kb = k_ref[0, pl.ds(start, tq), h * dh:(h + 1) * dh]
        vb = v_ref[0, pl.ds(start, tq), h * dh:(h + 1) * dh]
        w = _dot_nt(qs[h], kb)
        log_beta = jnp.minimum(w, 0.0) - jnp.log(1.0 + jnp.exp2(-jnp.abs(w))) * LOG2E
        log_rest = log_beta - w
        if diagonal:
            strict = (lax.broadcasted_iota(jnp.int32, (tq, tq), 1)
                      < lax.broadcasted_iota(jnp.int32, (tq, tq), 0))
            log_rest = jnp.where(strict, log_rest, 0.0)
        pieces = [None] * nsub
        for sb in reversed(range(nsub)):
            x = log_rest[:, sb * sub:(sb + 1) * sub]
            pieces[sb] = _split_dot_lhs(x, upper, 2) + carry
            carry = carry + jnp.sum(x, axis=1, keepdims=True)
        between = pieces[0] if nsub == 1 else jnp.concatenate(pieces, axis=1)
        a = jnp.exp2(log_beta + between)
        if diagonal:
            a = jnp.where(strict, a, 0.0)
        return carry, a.astype(BF16), vb

    def blocks(js, state, diagonal):
        state = list(state)
        for j in js:
            for h in range(hp):
                carry, acc = state[h]
                carry, a, vb = weights(h, j, carry, diagonal)
                state[h] = (carry, acc + _dot(a, vb))
        return tuple(state)

    state = tuple((jnp.zeros((tq, 1), F32), jnp.zeros((tq, dh), F32)) for _ in range(hp))
    state = blocks([i], state, True)
    odd = lax.rem(i, 2)
    state = lax.cond(odd == 1, lambda st: blocks([i - 1], st, False), lambda st: st, state)

    def body(jj, st):
        j = i - 1 - odd - 2 * jj
        return blocks([j, j - 1], st, False)

    state = lax.fori_loop(0, i // 2, body, state)
    o_ref[0] = jnp.concatenate([acc.astype(o_ref.dtype) for _, acc in state], axis=1)


def _sb_attention(proj3, *, q_col, k_col, v_col, heads, tq):
    b, s, _ = proj3.shape
    hp = SB_HEADS_PER_STEP
    dh = hp * ATTN_HEAD_DIM
    assert heads % hp == 0 and q_col % hp == 0 and k_col % hp == 0 and v_col % hp == 0
    return pl.pallas_call(
        functools.partial(_sb_kernel, tq=tq, scale=ATTN_HEAD_DIM ** -0.5, hp=hp),
        grid=(b, heads // hp, s // tq),
        in_specs=[
            pl.BlockSpec((1, tq, dh), lambda bi, h, i: (bi, i, q_col // hp + h)),
            pl.BlockSpec((1, s, dh), lambda bi, h, i: (bi, 0, k_col // hp + h)),
            pl.BlockSpec((1, s, dh), lambda bi, h, i: (bi, 0, v_col // hp + h)),
        ],
        out_specs=pl.BlockSpec((1, tq, dh), lambda bi, h, i: (bi, i, h)),
        out_shape=jax.ShapeDtypeStruct((b, s, heads * ATTN_HEAD_DIM), BF16),
        compiler_params=_params("parallel", "parallel", "arbitrary"),
        name="sb_attention",
    )(proj3, proj3, proj3)


def _fox_kernel(q_ref, k_ref, v_ref, cum_ref, ck_ref, o_ref, *, tq, scale, hp):
    hg = pl.program_id(1)
    i = pl.program_id(2)
    dh = ATTN_HEAD_DIM
    lane = lax.broadcasted_iota(jnp.int32, (tq, LANE), 1)
    qs, cqs = [], []
    for h in range(hp):
        qs.append((q_ref[0, :, h * dh:(h + 1) * dh].astype(F32) * (scale * LOG2E)).astype(BF16))
        cqs.append(jnp.sum(jnp.where(lane == hg * hp + h, cum_ref[0], 0.0), axis=1, keepdims=True) * LOG2E)

    def scores(h, j):
        start = pl.multiple_of(j * tq, tq)
        kb = k_ref[0, pl.ds(start, tq), h * dh:(h + 1) * dh]
        ck = ck_ref[0, h, pl.ds(j, 1), :] * LOG2E
        return _dot_nt(qs[h], kb) + (cqs[h] - ck), v_ref[0, pl.ds(start, tq), h * dh:(h + 1) * dh]

    causal = (lax.broadcasted_iota(jnp.int32, (tq, tq), 1)
              <= lax.broadcasted_iota(jnp.int32, (tq, tq), 0))

    def first(h):
        z, vb = scores(h, i)
        z = jnp.where(causal, z, MASK_VALUE)
        m = jnp.max(z, axis=1, keepdims=True)
        p = jnp.exp2(z - m)
        return m, jnp.sum(p, axis=1, keepdims=True), _dot(p.astype(BF16), vb)

    def step(h, js, m, l, acc):
        zv = [scores(h, j) for j in js]
        block_max = functools.reduce(jnp.maximum, [jnp.max(z, axis=1, keepdims=True) for z, _ in zv])
        m_new = jnp.maximum(m, block_max)
        alpha = jnp.exp2(m - m_new)
        ps = [jnp.exp2(z - m_new) for z, _ in zv]
        l = alpha * l + functools.reduce(jnp.add, [jnp.sum(p, axis=1, keepdims=True) for p in ps])
        acc = alpha * acc + functools.reduce(jnp.add, [_dot(p.astype(BF16), vb) for p, (_, vb) in zip(ps, zv)])
        return m_new, l, acc

    def steps(js, state):
        return tuple(step(h, js, *state[h]) for h in range(hp))

    state = tuple(first(h) for h in range(hp))
    odd = lax.rem(i, 2)
    state = lax.cond(odd == 1, lambda st: steps([i - 1], st), lambda st: st, state)

    def body(jj, st):
        j = i - 1 - odd - 2 * jj
        return steps([j, j - 1], st)

    state = lax.fori_loop(0, i // 2, body, state)
    o_ref[0] = jnp.concatenate([(acc / l).astype(o_ref.dtype) for _, l, acc in state], axis=1)


def _fox_attention(proj3, cum, ck, *, q_col, k_col, v_col, heads, tq):
    b, s, _ = proj3.shape
    hp = FOX_HEADS_PER_STEP
    dh = hp * ATTN_HEAD_DIM
    nb = s // tq
    assert heads % hp == 0 and q_col % hp == 0 and k_col % hp == 0 and v_col % hp == 0
    return pl.pallas_call(
        functools.partial(_fox_kernel, tq=tq, scale=ATTN_HEAD_DIM ** -0.5, hp=hp),
        grid=(b, heads // hp, nb),
        in_specs=[
            pl.BlockSpec((1, tq, dh), lambda bi, h, i: (bi, i, q_col // hp + h)),
            pl.BlockSpec((1, s, dh), lambda bi, h, i: (bi, 0, k_col // hp + h)),
            pl.BlockSpec((1, s, dh), lambda bi, h, i: (bi, 0, v_col // hp + h)),
            pl.BlockSpec((1, tq, LANE), lambda bi, h, i: (bi, i, 0)),
            pl.BlockSpec((1, hp, nb, tq), lambda bi, h, i: (bi, h, 0, 0)),
        ],
        out_specs=pl.BlockSpec((1, tq, dh), lambda bi, h, i: (bi, i, h)),
        out_shape=jax.ShapeDtypeStruct((b, s, heads * ATTN_HEAD_DIM), BF16),
        compiler_params=_params("parallel", "parallel", "arbitrary"),
        name="fox_attention",
    )(proj3, proj3, proj3, cum, ck)


def _same_head_mask(n):
    row = lax.broadcasted_iota(jnp.int32, (n, n), 0)
    col = lax.broadcasted_iota(jnp.int32, (n, n), 1)
    shift = RWKV_HEAD_DIM.bit_length() - 1
    return jnp.right_shift(row, shift) == jnp.right_shift(col, shift)


def _rwkv_prep_kernel(z_ref, zs_ref, mu_ref, mus_ref, w0_ref, wup_ref, a0_ref, aup_ref, gup_ref,
                      kk_ref, ka_ref,
                      r_out, lw_out, k_out, v_out, kn_out, b_out, g_out,
                      zc_ref, zsc_ref, *, tm, width):
    si = pl.program_id(1)

    @pl.when(si == 0)
    def _():
        zc_ref[...] = jnp.zeros_like(zc_ref)
        zsc_ref[...] = jnp.zeros_like(zsc_ref)

    z = z_ref[0].astype(F32)
    zs = zs_ref[0]
    first = lax.broadcasted_iota(jnp.int32, (tm, 1), 0) == 0

    def shifted(x, carry_ref):
        prev = jnp.where(first, carry_ref[...], pltpu.roll(x, 1, 0))
        carry_ref[...] = x[tm - 1:tm, :]
        return prev

    z = z + (shifted(z, zc_ref) - z) * mu_ref[...]
    zs = zs + (shifted(zs, zsc_ref) - zs) * mus_ref[...]

    r = z[:, :width]
    k = z[:, width:2 * width]
    v = z[:, 2 * width:]
    wd = zs[:, SMALL_WD:SMALL_WD + LANE]
    ad = zs[:, SMALL_AD:SMALL_AD + LANE]
    gd = zs[:, SMALL_GD:SMALL_GD + 2 * LANE]

    wl = w0_ref[...] + _dot(jnp.tanh(wd).astype(BF16), wup_ref[...])
    lw_out[0] = _sigmoid(wl) * (-DECAY_SCALE)
    a = _sigmoid(a0_ref[...] + _dot(ad.astype(BF16), aup_ref[...]))
    g_out[0] = _dot(_sigmoid(gd).astype(BF16), gup_ref[...]).astype(g_out.dtype)

    r_out[0] = r.astype(r_out.dtype)
    v_out[0] = v.astype(v_out.dtype)
    k_out[0] = (k * (1.0 + (a - 1.0) * ka_ref[...])).astype(k_out.dtype)

    kk_raw = k * kk_ref[...]
    ones_bd = jnp.where(_same_head_mask(RWKV_GROUP), 1.0, 0.0).astype(BF16)
    for gi in range(width // RWKV_GROUP):
        sl = slice(gi * RWKV_GROUP, (gi + 1) * RWKV_GROUP)
        x = kk_raw[:, sl]
        ss = _split_dot_lhs(x * x, ones_bd, 1)
        kn = x * lax.rsqrt(jnp.maximum(ss, KEY_NORM_FLOOR ** 2))
        kn_out[0, :, sl] = kn.astype(kn_out.dtype)
        b_out[0, :, sl] = (kn * a[:, sl]).astype(b_out.dtype)


def _rwkv_prep(proj3, small3, mu_rkv, mu_small, w0, w_up_p, a0, a_up_p, g_up_p, k_k, k_a, *, rkv_col, width, tm):
    b, s, _ = proj3.shape
    full = lambda shape: pl.BlockSpec(shape, lambda bi, si: (0,) * len(shape))
    tile = pl.BlockSpec((1, tm, width), lambda bi, si: (bi, si, 0))
    shp = lambda dt: jax.ShapeDtypeStruct((b, s, width), dt)
    return pl.pallas_call(
        functools.partial(_rwkv_prep_kernel, tm=tm, width=width),
        grid=(b, s // tm),
        in_specs=[
            pl.BlockSpec((1, tm, 3 * width), lambda bi, si: (bi, si, rkv_col)),
            pl.BlockSpec((1, tm, SMALL_WIDTH), lambda bi, si: (bi, si, 0)),
            full((1, 3 * width)), full((1, SMALL_WIDTH)),
            full((1, width)), full((LANE, width)),
            full((1, width)), full((LANE, width)),
            full((2 * LANE, width)),
            full((1, width)), full((1, width)),
        ],
        out_specs=[tile] * 7,
        out_shape=[shp(BF16), shp(F32), shp(BF16), shp(BF16), shp(BF16), shp(BF16), shp(BF16)],
        scratch_shapes=[pltpu.VMEM((1, 3 * width), F32), pltpu.VMEM((1, SMALL_WIDTH), F32)],
        compiler_params=_params("parallel", "arbitrary"),
        name="rwkv_prep",
    )(proj3, small3, mu_rkv, mu_small, w0, w_up_p, a0, a_up_p, g_up_p, k_k, k_a)


def _rwkv_scan_kernel(r_ref, lw_ref, k_ref, v_ref, kn_ref, b_ref, g_ref, rk_ref, lnw_ref, lnb_ref,
                      o_ref, s_ref, *, tt, gp):
    st = pl.program_id(2)

    @pl.when(st == 0)
    def _():
        s_ref[...] = jnp.zeros_like(s_ref)

    c = RWKV_CHUNK
    w = RWKV_GROUP
    heads = w // RWKV_HEAD_DIM
    assert heads * c == w
    same = _same_head_mask(w)
    row = lax.broadcasted_iota(jnp.int32, (w, w), 0)
    col = lax.broadcasted_iota(jnp.int32, (w, w), 1)
    strict = same & (col < row)
    incl = same & (col <= row)

    def tile(x):
        return jnp.concatenate([x] * heads, axis=0)

    def stack(x):
        return jnp.where(same, tile(x), 0.0).astype(BF16)

    def unstack(x):
        out = x[0:c]
        for h in range(1, heads):
            out = out + x[h * c:(h + 1) * c]
        return out

    nchunks = tt // c

    assert tt % w == 0
    tri = jnp.where(incl, 1.0, 0.0).astype(BF16)
    lanes = [slice(gi * w, (gi + 1) * w) for gi in range(gp)]
    cum_all = [jnp.concatenate([_split_dot_rhs(tri, lw_ref[0, r0:r0 + w, ln], 2) for r0 in range(0, tt, w)],
                               axis=0) for ln in lanes]

    def operands(ci, gi):
        sl = slice(ci * c, (ci + 1) * c)
        ln = lanes[gi]
        lw = lw_ref[0, sl, ln]
        r = r_ref[0, sl, ln].astype(F32)
        k = k_ref[0, sl, ln].astype(F32)
        v = v_ref[0, sl, ln].astype(F32)
        kn = kn_ref[0, sl, ln].astype(F32)
        bb = b_ref[0, sl, ln].astype(F32)
        cum = cum_all[gi][sl]
        last = cum[c - 1:c, :]
        e_neg = jnp.exp(-cum)
        e_end = jnp.exp(last - cum)
        return dict(
            gi=gi,
            r_s=stack(r * jnp.exp(cum)), a_s=stack(kn * jnp.exp(cum - lw)), v_s=stack(v),
            bk=jnp.concatenate([(bb * e_neg).astype(BF16), (k * e_neg).astype(BF16)], axis=0),
            b_end=stack(bb * e_end), k_end=stack(k * e_end), w_end=jnp.exp(last))

    first_half = lax.broadcasted_iota(jnp.int32, (w, 2 * c), 1) < c

    def block_diag(x, swapped, use_first, mask):
        half = jnp.where(first_half, x, swapped) if use_first else jnp.where(first_half, swapped, x)
        return jnp.where(mask, jnp.concatenate([half] * (w // (2 * c)), axis=1), 0.0)

    def interactions(ch):
        prod = _dot_nt(jnp.concatenate([ch["a_s"], ch["r_s"]], axis=0), ch["bk"])
        pa, pr = prod[:w], prod[w:]
        sa, sr = pltpu.roll(pa, c, 1), pltpu.roll(pr, c, 1)
        ch["pw"] = -block_diag(pa, sa, True, strict)
        ch["nrm"] = ch["pw"]
        ch["a_ak"] = block_diag(pa, sa, False, strict).astype(BF16)
        ch["a_rb"] = block_diag(pr, sr, True, incl).astype(BF16)
        ch["a_rk"] = block_diag(pr, sr, False, incl).astype(BF16)
        return ch

    def neumann_level(group):
        for ch in group:
            pwb = ch["pw"].astype(BF16)
            ch["pw"] = _dot(pwb, pwb)
        for ch in group:
            ch["nrm"] = ch["nrm"] + ch["pw"] + _dot(ch["nrm"].astype(BF16), ch["pw"].astype(BF16))

    chunks = [operands(ci, gi) for ci in range(nchunks) for gi in range(gp)]
    for ch in chunks:
        interactions(ch)
    for _ in range(c.bit_length() - 2):
        neumann_level(chunks)

    def stage_a(ch):
        ch["nrm"] = ch["nrm"].astype(BF16)
        ch["a_hat"] = (ch["a_s"].astype(F32) + _dot(ch["nrm"], ch["a_s"])).astype(BF16)
        ch["akv"] = _dot(ch["a_ak"], ch["v_s"])
        ch["q_s"] = _dot(ch["a_rk"], ch["v_s"])
        ch["kv"] = _dot_tn(ch["v_s"], ch["k_end"])

    def stage_b(ch):
        ch["v_hat"] = ch["akv"] + _dot(ch["nrm"], ch["akv"].astype(BF16))

    def stage_c(ch, state):
        ch["s0"] = state.astype(BF16)
        ch["u"] = (-(_dot_nt(ch["a_hat"], ch["s0"]) + ch["v_hat"])).astype(BF16)
        return ch["w_end"] * state + _dot_tn(ch["u"], ch["b_end"]) + ch["kv"]

    def stage_out(ch):
        return unstack(_dot_nt(ch["r_s"], ch["s0"]) + _dot(ch["a_rb"], ch["u"]) + ch["q_s"])

    states = [s_ref[gi] for gi in range(gp)]
    ys = [[] for _ in range(gp)]
    for step in range(len(chunks) + 3):
        if step < len(chunks):
            stage_a(chunks[step])
        if 0 <= step - 1 < len(chunks):
            stage_b(chunks[step - 1])
        if 0 <= step - 2 < len(chunks):
            ch = chunks[step - 2]
            states[ch["gi"]] = stage_c(ch, states[ch["gi"]])
        if 0 <= step - 3 < len(chunks):
            ch = chunks[step - 3]
            ys[ch["gi"]].append(stage_out(ch))

    mean_bd = jnp.where(same, 1.0 / RWKV_HEAD_DIM, 0.0).astype(BF16)
    ones_bd = jnp.where(same, 1.0, 0.0).astype(BF16)
    for gi, ln in enumerate(lanes):
        s_ref[gi] = states[gi]
        y = jnp.concatenate(ys[gi], axis=0)
        mean = _split_dot_lhs(y, mean_bd, 2)
        d = y - mean
        var = _split_dot_lhs(d * d, mean_bd, 1)
        y = d * lax.rsqrt(var + RWKV_GN_EPS) * lnw_ref[:, ln] + lnb_ref[:, ln]
        r = r_ref[0, :, ln].astype(F32)
        k = k_ref[0, :, ln].astype(F32)
        v = v_ref[0, :, ln].astype(F32)
        bonus = _split_dot_lhs(r * k * rk_ref[:, ln], ones_bd, 1)
        y = y + bonus * v
        o_ref[0, :, ln] = (y * g_ref[0, :, ln].astype(F32)).astype(o_ref.dtype)


def _rwkv_scan(r, lw, k, v, kn, bb, g, r_k, ln_w, ln_b, *, tt):
    b, s, width = r.shape
    gp = RWKV_GROUPS_PER_STEP
    w = gp * RWKV_GROUP
    assert width % w == 0
    tile = pl.BlockSpec((1, tt, w), lambda bi, gi, si: (bi, si, gi))
    vec = pl.BlockSpec((1, w), lambda bi, gi, si: (0, gi))
    return pl.pallas_call(
        functools.partial(_rwkv_scan_kernel, tt=tt, gp=gp),
        grid=(b, width // w, s // tt),
        in_specs=[tile] * 7 + [vec] * 3,
        out_specs=tile,
        out_shape=jax.ShapeDtypeStruct((b, s, width), BF16),
        scratch_shapes=[pltpu.VMEM((gp, RWKV_GROUP, RWKV_GROUP), F32)],
        compiler_params=_params("parallel", "parallel", "arbitrary"),
        name="rwkv_scan",
    )(r, lw, k, v, kn, bb, g, r_k, ln_w, ln_b)


def _mix_out_kernel(ya_ref, yb_ref, yc_ref, gate_ref, x_ref, pa_ref, pb_ref, pc_ref, wo_ref, gain_ref, o_ref):
    d = x_ref.shape[1]
    m = gate_ref[:, 0:d].astype(F32) * _dot(ya_ref[...], pa_ref[...])
    m = m + gate_ref[:, d:2 * d].astype(F32) * _dot(yb_ref[...], pb_ref[...])
    m = m + gate_ref[:, 2 * d:3 * d].astype(F32) * _dot(yc_ref[...], pc_ref[...])
    o = _dot(m.astype(BF16), wo_ref[...])
    ms = jnp.mean(o * o, axis=-1, keepdims=True)
    o_ref[...] = x_ref[...] + o * lax.rsqrt(ms + RMS_EPS) * gain_ref[...]


def _mix_out(ya, yb, yc, proj2, x2, pa, pb, pc, wo, layer, gain, *, gate_col_block, tm):
    t, d = x2.shape
    weight = lambda w: pl.BlockSpec((None,) + w.shape[1:], lambda m: (layer, 0, 0), pipeline_mode=pl.Buffered(1))
    return pl.pallas_call(
        _mix_out_kernel,
        grid=(t // tm,),
        in_specs=[
            pl.BlockSpec((tm, ya.shape[1]), lambda m: (m, 0)),
            pl.BlockSpec((tm, yb.shape[1]), lambda m: (m, 0)),
            pl.BlockSpec((tm, yc.shape[1]), lambda m: (m, 0)),
            pl.BlockSpec((tm, 3 * d), lambda m: (m, gate_col_block)),
            pl.BlockSpec((tm, d), lambda m: (m, 0)),
            weight(pa), weight(pb), weight(pc), weight(wo),
            pl.BlockSpec((1, d), lambda m: (0, 0)),
        ],
        out_specs=pl.BlockSpec((tm, d), lambda m: (m, 0)),
        out_shape=jax.ShapeDtypeStruct((t, d), F32),
        compiler_params=_params("parallel"),
        name="mix_out",
    )(ya, yb, yc, proj2, x2, pa, pb, pc, wo, gain.reshape(1, d))


def _mlp_kernel(x_ref, g1_ref, wu_ref, wd_ref, g2_ref, o_ref, u_ref, acc_ref):
    f = pl.program_id(1)

    @pl.when(f == 0)
    def _():
        x = x_ref[...]
        ms = jnp.mean(x * x, axis=-1, keepdims=True)
        u_ref[...] = (x * lax.rsqrt(ms + RMS_EPS) * g1_ref[...]).astype(BF16)
        acc_ref[...] = jnp.zeros_like(acc_ref)

    h = jnp.maximum(_dot(u_ref[...], wu_ref[...]), 0.0)
    acc_ref[...] += _dot((h * h).astype(BF16), wd_ref[...])

    @pl.when(f == pl.num_programs(1) - 1)
    def _():
        o = acc_ref[...]
        ms = jnp.mean(o * o, axis=-1, keepdims=True)
        o_ref[...] = x_ref[...] + o * lax.rsqrt(ms + RMS_EPS) * g2_ref[...]


def _mlp(x2, g1, wu, wd, layer, g2, *, tm, tf):
    t, d = x2.shape
    ff = wu.shape[2]
    return pl.pallas_call(
        _mlp_kernel,
        grid=(t // tm, ff // tf),
        in_specs=[
            pl.BlockSpec((tm, d), lambda m, f: (m, 0)),
            pl.BlockSpec((1, d), lambda m, f: (0, 0)),
            pl.BlockSpec((None, d, tf), lambda m, f: (layer, 0, f)),
            pl.BlockSpec((None, tf, d), lambda m, f: (layer, f, 0)),
            pl.BlockSpec((1, d), lambda m, f: (0, 0)),
        ],
        out_specs=pl.BlockSpec((tm, d), lambda m, f: (m, 0)),
        out_shape=jax.ShapeDtypeStruct((t, d), F32),
        scratch_shapes=[pltpu.VMEM((tm, d), BF16), pltpu.VMEM((tm, d), F32)],
        compiler_params=_params("parallel", "arbitrary"),
        name="mlp",
    )(x2, g1.reshape(1, d), wu, wd, g2.reshape(1, d))


def _pad_rows(w, rows):
    return jnp.pad(w, ((0, rows - w.shape[0]), (0, 0)))


def _pad_cols(w, cols):
    return jnp.pad(w, ((0, 0), (0, cols - w.shape[1])))


def _prepare_w_in(w_in, d, sbw, fxw, rww):
    fx_heads = fxw // ATTN_HEAD_DIM
    o_f = 3 * sbw + 3 * fxw
    o_rw = o_f + fx_heads
    o_wd = o_rw + 3 * rww
    o_ad = o_wd + DECAY_LORA
    o_gd = o_ad + AAA_LORA
    o_gate = o_gd + GATE_LORA
    assert w_in.shape[2] == o_gate + 3 * d
    w_t = jnp.transpose(w_in, (0, 2, 1))
    main = _repack_w_in(w_t, [(0, o_f), (o_rw, 3 * rww), (o_gate, 3 * d)], tn=TILE["repack_n"])

    def rows(lo, hi, padded):
        return jnp.pad(w_t[:, lo:hi], ((0, 0), (0, padded - (hi - lo)), (0, 0)))

    small = jnp.concatenate([rows(o_wd, o_ad, LANE), rows(o_ad, o_gd, LANE), rows(o_gd, o_gate, 2 * LANE),
                             rows(o_f, o_rw, LANE)], axis=1)
    return main, small


def _mixer_layer(x, layer, norm_pre, norm_post, w_main_t, w_small_t, b_forget, mu, w0, w_up, a0, a_up, g_up,
                 k_k, k_a, r_k, ln_w, ln_b, pa, pb, pc, w_out):
    bsz, s, d = x.shape
    t = bsz * s
    sbw = pa.shape[1]
    fxw = pb.shape[1]
    rww = pc.shape[1]
    sb_heads = sbw // ATTN_HEAD_DIM
    fx_heads = fxw // ATTN_HEAD_DIM

    gate_col = 3 * sbw + 3 * fxw + 3 * rww
    mu_rkv = mu[None, :3 * rww]
    mu_small = jnp.concatenate([
        _pad_cols(mu[None, 3 * rww:3 * rww + DECAY_LORA], LANE),
        _pad_cols(mu[None, 3 * rww + DECAY_LORA:3 * rww + DECAY_LORA + AAA_LORA], LANE),
        _pad_cols(mu[None, 3 * rww + DECAY_LORA + AAA_LORA:], 2 * LANE),
        jnp.zeros((1, LANE), F32)], axis=1)

    x2 = x.reshape(t, d)
    proj_m = min(TILE["proj_m"], t)
    proj = _norm_proj(x2, norm_pre, w_main_t, layer, BF16, tm=proj_m, tn=TILE["proj_n"],
                      sigmoid_from_col=gate_col)
    small = _norm_proj(x2, norm_pre, w_small_t, layer, F32, tm=proj_m, tn=SMALL_WIDTH)
    proj3 = proj.reshape(bsz, s, -1)
    small3 = small.reshape(bsz, s, SMALL_WIDTH)

    tq = min(TILE["attn_q"], s)
    qa = 0
    ya = _sb_attention(proj3, q_col=qa, k_col=qa + sb_heads, v_col=qa + 2 * sb_heads, heads=sb_heads, tq=tq)

    cum = _forget_cumsum(small3, _pad_cols(b_forget[None, :], LANE))
    c_hs = cum[:, :, :fx_heads].transpose(0, 2, 1)
    qb = 3 * sb_heads
    yb = _fox_attention(proj3, cum, c_hs.reshape(bsz, fx_heads, s // tq, tq),
                        q_col=qb, k_col=qb + fx_heads, v_col=qb + 2 * fx_heads, heads=fx_heads, tq=tq)

    rkv_col = (3 * sbw + 3 * fxw) // (3 * rww)
    assert rkv_col * 3 * rww == 3 * sbw + 3 * fxw
    row = lambda p: p.reshape(1, rww)
    r, lw, k, v, kn, bb, g = _rwkv_prep(
        proj3, small3, mu_rkv, mu_small, row(w0), _pad_rows(w_up, LANE).astype(BF16), row(a0),
        _pad_rows(a_up, LANE).astype(BF16), _pad_rows(g_up, 2 * LANE).astype(BF16), row(k_k), row(k_a),
        rkv_col=rkv_col, width=rww, tm=min(TILE["rwkv_prep_m"], s))
    yc = _rwkv_scan(r, lw, k, v, kn, bb, g, row(r_k), row(ln_w), row(ln_b), tt=min(TILE["rwkv_scan_t"], s))

    assert gate_col % (3 * d) == 0
    out = _mix_out(ya.reshape(t, sbw), yb.reshape(t, fxw), yc.reshape(t, rww), proj, x2,
                   pa, pb, pc, w_out, layer, norm_post,
                   gate_col_block=gate_col // (3 * d), tm=min(TILE["mix_m"], t))
    return out.reshape(bsz, s, d)


def kernel(x, norm_mix_pre, norm_mix_post, norm_mlp_pre, norm_mlp_post, w_in, b_forget, rwkv_mu, rwkv_w0,
           rwkv_w_up, rwkv_a0, rwkv_a_up, rwkv_g_up, rwkv_k_k, rwkv_k_a, rwkv_r_k, rwkv_ln_w, rwkv_ln_b,
           w_branch_a, w_branch_b, w_branch_c, w_out, w_mlp_up, w_mlp_down):
    bsz, s, d = x.shape
    t = bsz * s
    w_main_t, w_small_t = _prepare_w_in(w_in, d, w_branch_a.shape[1], w_branch_b.shape[1], w_branch_c.shape[1])
    pa, pb, pc, wo = (w.astype(BF16) for w in (w_branch_a, w_branch_b, w_branch_c, w_out))
    wu, wd = w_mlp_up.astype(BF16), w_mlp_down.astype(BF16)
    for l in range(w_in.shape[0]):
        x = _mixer_layer(x, l, norm_mix_pre[l], norm_mix_post[l], w_main_t, w_small_t, b_forget[l], rwkv_mu[l],
                         rwkv_w0[l], rwkv_w_up[l], rwkv_a0[l], rwkv_a_up[l], rwkv_g_up[l], rwkv_k_k[l],
                         rwkv_k_a[l], rwkv_r_k[l], rwkv_ln_w[l], rwkv_ln_b[l], pa, pb, pc, wo)
        x = _mlp(x.reshape(t, d), norm_mlp_pre[l], wu, wd, l, norm_mlp_post[l],
                 tm=min(TILE["mlp_m"], t), tf=TILE["mlp_f"]).reshape(bsz, s, d)
    return x
```
